```python
import math
import jax
import jax.numpy as jnp
from jax import lax
import numpy as np

D_MODEL = 1024
BATCH = 2
SEQ = 16384
DEPTH = 4

CHUNK = 64
N_A_LAYERS = max(1, DEPTH // 2)
N_B_LAYERS = DEPTH - N_A_LAYERS

GLA_HEADS = 4
GLA_DK = D_MODEL // 2 // GLA_HEADS
GLA_DV = D_MODEL // GLA_HEADS
GLA_GATE_RANK = 16
GLA_GATE_TAU = 16.0
GLA_HK = GLA_HEADS * GLA_DK
GLA_HV = GLA_HEADS * GLA_DV
GLA_IN = 2 * GLA_HK + 2 * GLA_HV + GLA_GATE_RANK

DIFF_HEADS = 8
DIFF_DH = D_MODEL // (2 * DIFF_HEADS)
DIFF_DV = 2 * DIFF_DH
DIFF_QK = DIFF_HEADS * DIFF_DH
Q_BLOCK = 128

REL_BUCKETS = 32
REL_MAX_DIST = 128

MOE_GROUPS = 4
MOE_EXPERTS_PER_GROUP = 4
MOE_EXPERTS = MOE_GROUPS * MOE_EXPERTS_PER_GROUP
MOE_TOPK = 2
MOE_FF = D_MODEL // 2

DEEPNORM_ALPHA = (2.0 * DEPTH) ** 0.25
DEEPNORM_BETA = (8.0 * DEPTH) ** -0.25
LN_EPS = 1e-5
NEG_INF = -1e30

kernel_name = 'hybrid_gla_diffattn_hmoe_yoco'


def layer_norm(x, g, b):
    xf = x.astype(jnp.float32)
    mu = jnp.mean(xf, -1, keepdims=True)
    var = jnp.mean(jnp.square(xf - mu), -1, keepdims=True)
    return ((xf - mu) * lax.rsqrt(var + LN_EPS)).astype(x.dtype) * g + b


def rms_norm(x, g):
    xf = x.astype(jnp.float32)
    return (xf * lax.rsqrt(jnp.mean(xf * xf, -1, keepdims=True) + LN_EPS)).astype(x.dtype) * g


def gla_mixer(x, w_in, w_gate2, b_gate, g_norm, w_out):
    B, S, _ = x.shape
    nc = S // CHUNK
    h = x @ w_in
    q, k, v, r, lr = jnp.split(h, [GLA_HK, 2 * GLA_HK, 2 * GLA_HK + GLA_HV, 2 * GLA_HK + 2 * GLA_HV], axis=-1)
    log_a = jax.nn.log_sigmoid((lr @ w_gate2 + b_gate).astype(jnp.float32)) / GLA_GATE_TAU

    def to_chunks(t, d):
        return t.reshape(B, nc, CHUNK, GLA_HEADS, d).transpose(1, 0, 3, 2, 4)

    qc = to_chunks(q * (GLA_DK ** -0.5), GLA_DK)
    kc = to_chunks(k, GLA_DK)
    vc = to_chunks(v, GLA_DV)
    cum = jnp.cumsum(to_chunks(log_a, GLA_DK), axis=3)
    last = cum[:, :, :, -1:, :]
    k_dec = kc * jnp.exp(last - cum).astype(kc.dtype)
    chunk_decay = jnp.exp(last[:, :, :, 0, :])

    def step(state, inp):
        q_c, k_c, v_c, dec = inp
        state = dec[..., None] * state + jnp.einsum('bhck,bhcv->bhkv', k_c.astype(jnp.float32), v_c.astype(jnp.float32))
        o = jnp.einsum('bhck,bhkv->bhcv', q_c.astype(jnp.float32), state)
        return state, o

    s0 = jnp.zeros((B, GLA_HEADS, GLA_DK, GLA_DV), jnp.float32)
    _, o = lax.scan(step, s0, (qc, k_dec, vc, chunk_decay))
    o = o.astype(x.dtype).transpose(1, 0, 3, 2, 4).reshape(B, S, GLA_HEADS, GLA_DV)
    o = rms_norm(o, g_norm.reshape(GLA_HEADS, GLA_DV)).reshape(B, S, GLA_HV)
    return (o * jax.nn.silu(r)) @ w_out


def shared_kv(x, w_kv):
    B, S, _ = x.shape
    kv = x @ w_kv
    k1, k2, v = jnp.split(kv, [DIFF_QK, 2 * DIFF_QK], axis=-1)
    k1 = k1.reshape(B, S, DIFF_HEADS, DIFF_DH).transpose(0, 2, 1, 3)
    k2 = k2.reshape(B, S, DIFF_HEADS, DIFF_DH).transpose(0, 2, 1, 3)
    v = v.reshape(B, S, DIFF_HEADS, DIFF_DV).transpose(0, 2, 1, 3)
    return k1, k2, v


def rel_bucket(rel):
    nb = REL_BUCKETS // 2
    max_exact = nb // 2
    base = jnp.where(rel > 0, nb, 0)
    n = jnp.abs(rel)
    large = max_exact + (jnp.log(jnp.maximum(n, 1).astype(jnp.float32) / max_exact)
                         / math.log(REL_MAX_DIST / max_exact) * (nb - max_exact)).astype(jnp.int32)
    large = jnp.minimum(large, nb - 1)
    return base + jnp.where(n < max_exact, n, large)


def diff_attention(x, w_q, lam_q1, lam_k1, lam_q2, lam_k2, g_sub, w_out, k1, k2, v, rel_table, lambda_init):
    B, S, _ = x.shape
    nb = S // Q_BLOCK
    q1, q2 = jnp.split(x @ w_q, 2, axis=-1)

    def blocks(t):
        return t.reshape(B, nb, Q_BLOCK, DIFF_HEADS, DIFF_DH).transpose(1, 0, 3, 2, 4)

    lam = (jnp.exp(jnp.sum(lam_q1 * lam_k1)) - jnp.exp(jnp.sum(lam_q2 * lam_k2)) + lambda_init).astype(jnp.float32)
    kpos = jnp.arange(S)
    scale = DIFF_DH ** -0.5

    def attend(args):
        i, q1b, q2b = args
        qpos = i * Q_BLOCK + jnp.arange(Q_BLOCK)
        bias = rel_table[rel_bucket(kpos[None, :] - qpos[:, None])].transpose(2, 0, 1).astype(jnp.float32)
        visible = (kpos[None, :] // CHUNK) <= (qpos[:, None] // CHUNK)

        def probs(qb, kk):
            s = jnp.einsum('bhqd,bhkd->bhqk', qb, kk).astype(jnp.float32) * scale + bias
            return jax.nn.softmax(jnp.where(visible, s, NEG_INF), axis=-1)

        w = probs(q1b, k1) - lam * probs(q2b, k2)
        return jnp.einsum('bhqk,bhkv->bhqv', w.astype(v.dtype), v)

    o = lax.map(attend, (jnp.arange(nb), blocks(q1), blocks(q2)))
    o = rms_norm(o, g_sub) * (1.0 - lambda_init)
    o = o.transpose(1, 0, 3, 2, 4).reshape(B, S, DIFF_HEADS * DIFF_DV)
    return o @ w_out


def hier_moe(x, w_group, b_group, w_router, b_router, w_gate, w_up, w_down):
    B, S, D = x.shape
    t = x.reshape(-1, D)
    g_logits = (t @ w_group).astype(jnp.float32) + b_group
    g_prob = jax.nn.softmax(g_logits, axis=-1)
    g_idx = jnp.argmax(g_logits, axis=-1)
    g_w = jnp.take_along_axis(g_prob, g_idx[:, None], axis=-1)
    e_logits = (t @ w_router).astype(jnp.float32).reshape(-1, MOE_GROUPS, MOE_EXPERTS_PER_GROUP) + b_router
    e_logits = jnp.take_along_axis(e_logits, g_idx[:, None, None], axis=1)[:, 0]
    top_w, top_i = lax.top_k(jax.nn.softmax(e_logits, axis=-1), MOE_TOPK)
    top_w = top_w / jnp.sum(top_w, -1, keepdims=True)
    within = jnp.sum(jax.nn.one_hot(top_i, MOE_EXPERTS_PER_GROUP, dtype=jnp.float32) * top_w[..., None], axis=1)
    combine = (jax.nn.one_hot(g_idx, MOE_GROUPS, dtype=jnp.float32)[:, :, None]
               * within[:, None, :] * g_w[:, :, None]).reshape(-1, MOE_EXPERTS).astype(t.dtype)
    y = jnp.zeros_like(t)
    for e in range(MOE_EXPERTS):
        hdn = jax.nn.silu(t @ w_gate[e]) * (t @ w_up[e])
        y = y + combine[:, e:e + 1] * (hdn @ w_down[e])
    return y.reshape(B, S, D)


def setup_inputs(seed: int = 0) -> dict:
    key = jax.random.key(seed)
    ks = jax.random.split(key, 24)

    def nrm(k, shape, scale):
        return jax.random.normal(k, shape, jnp.float32) * scale

    D = D_MODEL
    return {
        'x': nrm(ks[0], (BATCH, SEQ, D), 1.0),
        'a_w_in': nrm(ks[1], (N_A_LAYERS, D, GLA_IN), D ** -0.5),
        'a_w_gate2': nrm(ks[2], (N_A_LAYERS, GLA_GATE_RANK, GLA_HK), GLA_GATE_RANK ** -0.5),
        'a_b_gate': nrm(ks[3], (N_A_LAYERS, GLA_HK), 0.1),
        'a_g_norm': 1.0 + nrm(ks[4], (N_A_LAYERS, GLA_HV), 0.02),
        'a_w_out': nrm(ks[5], (N_A_LAYERS, GLA_HV, D), GLA_HV ** -0.5 * DEEPNORM_BETA),
        'kv_w': nrm(ks[6], (D, 2 * DIFF_QK + DIFF_HEADS * DIFF_DV), D ** -0.5),
        'b_w_q': nrm(ks[7], (N_B_LAYERS, D, 2 * DIFF_QK), D ** -0.5),
        'b_lam_q1': nrm(ks[8], (N_B_LAYERS, DIFF_DH), 0.1),
        'b_lam_k1': nrm(ks[9], (N_B_LAYERS, DIFF_DH), 0.1),
        'b_lam_q2': nrm(ks[10], (N_B_LAYERS, DIFF_DH), 0.1),
        'b_lam_k2': nrm(ks[11], (N_B_LAYERS, DIFF_DH), 0.1),
        'b_g_sub': 1.0 + nrm(ks[12], (N_B_LAYERS, DIFF_DV), 0.02),
        'b_w_out': nrm(ks[13], (N_B_LAYERS, DIFF_HEADS * DIFF_DV, D), (DIFF_HEADS * DIFF_DV) ** -0.5 * DEEPNORM_BETA),
        'rel_table': nrm(ks[14], (REL_BUCKETS, DIFF_HEADS), 0.2),
        'moe_w_group': nrm(ks[15], (DEPTH, D, MOE_GROUPS), D ** -0.5),
        'moe_b_group': nrm(ks[16], (DEPTH, MOE_GROUPS), 0.01),
        'moe_w_router': nrm(ks[17], (DEPTH, D, MOE_EXPERTS), D ** -0.5),
        'moe_b_router': nrm(ks[18], (DEPTH, MOE_GROUPS, MOE_EXPERTS_PER_GROUP), 0.01),
        'moe_w_gate': nrm(ks[19], (DEPTH, MOE_EXPERTS, D, MOE_FF), D ** -0.5),
        'moe_w_up': nrm(ks[20], (DEPTH, MOE_EXPERTS, D, MOE_FF), D ** -0.5),
        'moe_w_down': nrm(ks[21], (DEPTH, MOE_EXPERTS, MOE_FF, D), MOE_FF ** -0.5 * DEEPNORM_BETA),
        'ln_g': 1.0 + nrm(ks[22], (DEPTH, 2, D), 0.02),
        'ln_b': nrm(ks[23], (DEPTH, 2, D), 0.02),
    }


def reference(x, a_w_in, a_w_gate2, a_b_gate, a_g_norm, a_w_out, kv_w, b_w_q, b_lam_q1, b_lam_k1,
              b_lam_q2, b_lam_k2, b_g_sub, b_w_out, rel_table, moe_w_group, moe_b_group, moe_w_router,
              moe_b_router, moe_w_gate, moe_w_up, moe_w_down, ln_g, ln_b):
    h = x
    k1 = k2 = v = None
    for layer in range(DEPTH):
        if layer < N_A_LAYERS:
            mix = gla_mixer(h, a_w_in[layer], a_w_gate2[layer], a_b_gate[layer], a_g_norm[layer], a_w_out[layer])
        else:
            j = layer - N_A_LAYERS
            lambda_init = 0.8 - 0.6 * math.exp(-0.3 * layer)
            mix = diff_attention(h, b_w_q[j], b_lam_q1[j], b_lam_k1[j], b_lam_q2[j], b_lam_k2[j], b_g_sub[j],
                                 b_w_out[j], k1, k2, v, rel_table, lambda_init)
        h = layer_norm(DEEPNORM_ALPHA * h + mix, ln_g[layer, 0], ln_b[layer, 0])
        ffn = hier_moe(h, moe_w_group[layer], moe_b_group[layer], moe_w_router[layer], moe_b_router[layer],
                       moe_w_gate[layer], moe_w_up[layer], moe_w_down[layer])
        h = layer_norm(DEEPNORM_ALPHA * h + ffn, ln_g[layer, 1], ln_b[layer, 1])
        if layer == N_A_LAYERS - 1:
            k1, k2, v = shared_kv(h, kv_w)
    return h
```

```python
import functools
import math

import jax
import jax.numpy as jnp
from jax import lax
from jax.experimental import pallas as pl
from jax.experimental.pallas import tpu as pltpu

F32 = jnp.float32
BF16 = jnp.bfloat16

D_MODEL = 1024
DEPTH = 4
CHUNK = 64
N_A_LAYERS = 2

GLA_HEADS = 4
GLA_DK = 128
GLA_DV = 256
GLA_GATE_RANK = 16
GLA_GATE_TAU = 16.0
GLA_HK = GLA_HEADS * GLA_DK
GLA_HV = GLA_HEADS * GLA_DV

DIFF_HEADS = 8
DIFF_DH = 64
DIFF_DV = 128
DIFF_QK = DIFF_HEADS * DIFF_DH

REL_BUCKETS = 32

MOE_GROUPS = 4
MOE_EPG = 4
MOE_EXPERTS = 16
MOE_FF = 512

DEEPNORM_ALPHA = (2.0 * DEPTH) ** 0.25
LN_EPS = 1e-5
NEG_INF = -1e30

LANES = 128
ATT_BLK = 256
VMEM_LIMIT = 48 * 1024 * 1024

_HIGHEST = lax.Precision.HIGHEST


def _layer_norm(y, g, b):
    mu = jnp.mean(y, -1, keepdims=True)
    d = y - mu
    var = jnp.mean(d * d, -1, keepdims=True)
    return d * lax.rsqrt(var + LN_EPS) * g + b


def _sigmoid(x):
    return 1.0 / (1.0 + jnp.exp(-x))


def _full(shape):
    return pl.BlockSpec(shape, lambda *_: (0,) * len(shape))


def _gla_kernel(x_ref, w_main_ref, w_lr_ref, w_g2_ref, b_g_ref, gn_ref, w_out_ref, lng_ref, lnb_ref,
                tri_ref, o_ref, st_ref, mix_ref):
    tb = x_ref.shape[1]

    @pl.when(pl.program_id(1) == 0)
    def _():
        st_ref[...] = jnp.zeros_like(st_ref)

    x = x_ref[0]
    xb = x.astype(BF16)
    hp = jnp.dot(xb, w_main_ref[...], preferred_element_type=F32)
    lr = jnp.dot(xb, w_lr_ref[...], preferred_element_type=F32)
    glog = jnp.dot(lr.astype(BF16), w_g2_ref[...], preferred_element_type=F32) + b_g_ref[...]
    la = (jnp.minimum(glog, 0.0) - jnp.log(1.0 + jnp.exp(-jnp.abs(glog)))) * (1.0 / GLA_GATE_TAU)
    la_hi = la.astype(BF16)
    la_lo = (la - la_hi.astype(F32)).astype(BF16)
    tri = tri_ref[...]
    cum = (jnp.dot(tri, la_hi, preferred_element_type=F32)
           + jnp.dot(tri, la_lo, preferred_element_type=F32))

    for c in range(tb // CHUNK):
        r0 = c * CHUNK
        cum_c = cum[r0:r0 + CHUNK]
        last = cum_c[CHUNK - 1:CHUNK]
        kdec = (hp[r0:r0 + CHUNK, GLA_HK:2 * GLA_HK] * jnp.exp(last - cum_c)).astype(BF16)
        dec = jnp.exp(last)
        qc = (hp[r0:r0 + CHUNK, :GLA_HK] * (GLA_DK ** -0.5)).astype(BF16)
        vc = hp[r0:r0 + CHUNK, 2 * GLA_HK:2 * GLA_HK + GLA_HV].astype(BF16)
        for h in range(GLA_HEADS):
            ks = slice(h * GLA_DK, (h + 1) * GLA_DK)
            vs = slice(h * GLA_DV, (h + 1) * GLA_DV)
            upd = lax.dot_general(vc[:, vs], kdec[:, ks], (((0,), (0,)), ((), ())),
                                  preferred_element_type=F32)
            st = st_ref[h] * dec[:, ks] + upd
            st_ref[h] = st
            o = lax.dot_general(qc[:, ks], st.astype(BF16), (((1,), (1,)), ((), ())),
                                preferred_element_type=F32)
            o = o * lax.rsqrt(jnp.mean(o * o, -1, keepdims=True) + LN_EPS) * gn_ref[:, vs]
            mix_ref[r0:r0 + CHUNK, vs] = o

    r = hp[:, 2 * GLA_HK + GLA_HV:]
    gated = mix_ref[...] * (r * _sigmoid(r))
    mix = jnp.dot(gated.astype(BF16), w_out_ref[...], preferred_element_type=F32)
    o_ref[0] = _layer_norm(DEEPNORM_ALPHA * x + mix, lng_ref[...], lnb_ref[...])


def _gla_layer(h, w_in, w_gate2, b_gate, g_norm, w_out, ln_g, ln_b, *, tb=256):
    B, S, D = h.shape
    n_main = 2 * GLA_HK + 2 * GLA_HV
    w_main = w_in[:, :n_main].astype(BF16)
    w_lr = jnp.zeros((D, LANES), F32).at[:, :GLA_GATE_RANK].set(w_in[:, n_main:]).astype(BF16)
    w_g2 = jnp.zeros((LANES, GLA_HK), F32).at[:GLA_GATE_RANK].set(w_gate2).astype(BF16)
    idx = jnp.arange(tb)
    tri = ((idx[:, None] >= idx[None, :]) & (idx[:, None] // CHUNK == idx[None, :] // CHUNK)).astype(BF16)
    return pl.pallas_call(
        _gla_kernel,
        grid=(B, S // tb),
        in_specs=[
            pl.BlockSpec((1, tb, D), lambda b, j: (b, j, 0)),
            _full((D, n_main)), _full((D, LANES)), _full((LANES, GLA_HK)), _full((1, GLA_HK)),
            _full((1, GLA_HV)), _full((GLA_HV, D)), _full((1, D)), _full((1, D)), _full((tb, tb)),
        ],
        out_specs=pl.BlockSpec((1, tb, D), lambda b, j: (b, j, 0)),
        out_shape=jax.ShapeDtypeStruct((B, S, D), F32),
        scratch_shapes=[pltpu.VMEM((GLA_HEADS, GLA_DV, GLA_DK), F32), pltpu.VMEM((tb, GLA_HV), F32)],
        compiler_params=pltpu.CompilerParams(dimension_semantics=("parallel", "arbitrary"),
                                             vmem_limit_bytes=VMEM_LIMIT),
        name="gla_layer",
    )(h, w_main, w_lr, w_g2, b_gate.reshape(1, -1), g_norm.reshape(1, -1), w_out.astype(BF16),
      ln_g.reshape(1, -1), ln_b.reshape(1, -1), tri)


def _route(logits):
    lane = lax.broadcasted_iota(jnp.int32, logits.shape, 1)
    big = jnp.int32(1 << 20)
    gl = jnp.where(lane < MOE_GROUPS, logits, -jnp.inf)
    gmax = jnp.max(gl, -1, keepdims=True)
    gidx = jnp.min(jnp.where(gl == gmax, lane, big), -1, keepdims=True)
    gw = 1.0 / jnp.sum(jnp.exp(gl - gmax), -1, keepdims=True)
    lo = MOE_GROUPS + gidx * MOE_EPG
    emask = (lane >= lo) & (lane < lo + MOE_EPG)
    el = jnp.where(emask, logits, -jnp.inf)
    pe = jnp.exp(el - jnp.max(el, -1, keepdims=True))
    prob = pe / jnp.sum(pe, -1, keepdims=True)
    pm = jnp.where(emask, prob, -1.0)
    p1 = jnp.max(pm, -1, keepdims=True)
    i1 = jnp.min(jnp.where(pm == p1, lane, big), -1, keepdims=True)
    pm2 = jnp.where(lane == i1, -1.0, pm)
    p2 = jnp.max(pm2, -1, keepdims=True)
    i2 = jnp.min(jnp.where(pm2 == p2, lane, big), -1, keepdims=True)
    den = p1 + p2
    comb = jnp.where(lane == i1, p1 / den, 0.0) + jnp.where(lane == i2, p2 / den, 0.0)
    return comb * gw


def _moe_kernel(x_ref, wr_ref, br_ref, wg_ref, wu_ref, wd_ref, lng_ref, lnb_ref, o_ref, comb_ref, acc_ref):
    e = pl.program_id(1)
    x = x_ref[...]

    @pl.when(e == 0)
    def _():
        logits = jnp.dot(x, wr_ref[...], precision=_HIGHEST, preferred_element_type=F32) + br_ref[...]
        comb_ref[...] = _route(logits)
        acc_ref[...] = jnp.zeros_like(acc_ref)

    xb = x.astype(BF16)
    g = jnp.dot(xb, wg_ref[0], preferred_element_type=F32)
    u = jnp.dot(xb, wu_ref[0], preferred_element_type=F32)
    hdn = (g * _sigmoid(g)) * u
    y = jnp.dot(hdn.astype(BF16), wd_ref[0], preferred_element_type=F32)
    comb = comb_ref[...]
    lane = lax.broadcasted_iota(jnp.int32, comb.shape, 1)
    c = jnp.sum(jnp.where(lane == e + MOE_GROUPS, comb, 0.0), -1, keepdims=True)
    acc_ref[...] += c * y

    @pl.when(e == MOE_EXPERTS - 1)
    def _():
        o_ref[...] = _layer_norm(DEEPNORM_ALPHA * x + acc_ref[...], lng_ref[...], lnb_ref[...])


def _moe_layer(h, w_group, b_group, w_router, b_router, w_gate, w_up, w_down, ln_g, ln_b, *, tb=1024):
    T, D = h.shape
    wr = jnp.zeros((D, LANES), F32).at[:, :MOE_GROUPS].set(w_group)
    wr = wr.at[:, MOE_GROUPS:MOE_GROUPS + MOE_EXPERTS].set(w_router)
    br = jnp.zeros((1, LANES), F32).at[0, :MOE_GROUPS].set(b_group)
    br = br.at[0, MOE_GROUPS:MOE_GROUPS + MOE_EXPERTS].set(b_router.reshape(-1))
    return pl.pallas_call(
        _moe_kernel,
        grid=(T // tb, MOE_EXPERTS),
        in_specs=[
            pl.BlockSpec((tb, D), lambda i, e: (i, 0)),
            _full((D, LANES)), _full((1, LANES)),
            pl.BlockSpec((1, D, MOE_FF), lambda i, e: (e, 0, 0)),
            pl.BlockSpec((1, D, MOE_FF), lambda i, e: (e, 0, 0)),
            pl.BlockSpec((1, MOE_FF, D), lambda i, e: (e, 0, 0)),
            _full((1, D)), _full((1, D)),
        ],
        out_specs=pl.BlockSpec((tb, D), lambda i, e: (i, 0)),
        out_shape=jax.ShapeDtypeStruct((T, D), F32),
        scratch_shapes=[pltpu.VMEM((tb, LANES), F32), pltpu.VMEM((tb, D), F32)],
        compiler_params=pltpu.CompilerParams(dimension_semantics=("parallel", "arbitrary"),
                                             vmem_limit_bytes=VMEM_LIMIT),
        name="moe_layer",
    )(h, wr, br, w_gate.astype(BF16), w_up.astype(BF16), w_down.astype(BF16),
      ln_g.reshape(1, -1), ln_b.reshape(1, -1))


def _kv_kernel(x_ref, wk_ref, wvt_ref, k_ref, vt_ref):
    xb = x_ref[0].astype(BF16)
    k_ref[0] = jnp.dot(xb, wk_ref[...], preferred_element_type=F32).astype(BF16)
    vt = lax.dot_general(wvt_ref[...], xb, (((1,), (1,)), ((), ())), preferred_element_type=F32)
    for j in range(vt_ref.shape[1]):
        vt_ref[0, j] = vt[:, j * ATT_BLK:(j + 1) * ATT_BLK].astype(BF16)


def _kv_proj(h, kv_w, *, ts=512):
    B, S, D = h.shape
    k1 = kv_w[:, :DIFF_QK].reshape(D, DIFF_HEADS, DIFF_DH)
    k2 = kv_w[:, DIFF_QK:2 * DIFF_QK].reshape(D, DIFF_HEADS, DIFF_DH)
    wk = jnp.stack([k1, k2], axis=2).reshape(D, 2 * DIFF_QK).astype(BF16)
    wvt = kv_w[:, 2 * DIFF_QK:].T.astype(BF16)
    nv = DIFF_HEADS * DIFF_DV
    nj = ts // ATT_BLK
    return pl.pallas_call(
        _kv_kernel,
        grid=(B, S // ts),
        in_specs=[pl.BlockSpec((1, ts, D), lambda b, j: (b, j, 0)), _full((D, 2 * DIFF_QK)), _full((nv, D))],
        out_specs=[pl.BlockSpec((1, ts, 2 * DIFF_QK), lambda b, j: (b, j, 0)),
                   pl.BlockSpec((1, nj, nv, ATT_BLK), lambda b, j: (b, j, 0, 0))],
        out_shape=[jax.ShapeDtypeStruct((B, S, 2 * DIFF_QK), BF16),
                   jax.ShapeDtypeStruct((B, S // ATT_BLK, nv, ATT_BLK), BF16)],
        compiler_params=pltpu.CompilerParams(dimension_semantics=("parallel", "parallel"),
                                             vmem_limit_bytes=VMEM_LIMIT),
        name="kv_proj",
    )(h, wk, wvt)


def _q_kernel(x_ref, wqt_ref, qt_ref):
    xb = x_ref[0].astype(BF16)
    qt = lax.dot_general(wqt_ref[...], xb, (((1,), (1,)), ((), ())), preferred_element_type=F32)
    qt = qt * (DIFF_DH ** -0.5)
    for j in range(qt_ref.shape[1]):
        qt_ref[0, j] = qt[:, j * ATT_BLK:(j + 1) * ATT_BLK].astype(BF16)


def _q_proj(h, w_q, *, ts=512):
    B, S, D = h.shape
    q1 = w_q[:, :DIFF_QK].reshape(D, DIFF_HEADS, DIFF_DH)
    q2 = w_q[:, DIFF_QK:].reshape(D, DIFF_HEADS, DIFF_DH)
    wqt = jnp.stack([q1, q2], axis=2).reshape(D, 2 * DIFF_QK).T.astype(BF16)
    nq = 2 * DIFF_QK
    nj = ts // ATT_BLK
    return pl.pallas_call(
        _q_kernel,
        grid=(B, S // ts),
        in_specs=[pl.BlockSpec((1, ts, D), lambda b, j: (b, j, 0)), _full((nq, D))],
        out_specs=pl.BlockSpec((1, nj, nq, ATT_BLK), lambda b, j: (b, j, 0, 0)),
        out_shape=jax.ShapeDtypeStruct((B, S // ATT_BLK, nq, ATT_BLK), BF16),
        compiler_params=pltpu.CompilerParams(dimension_semantics=("parallel", "parallel"),
                                             vmem_limit_bytes=VMEM_LIMIT),
        name="q_proj",
    )(h, wqt)


def _bias_kernel(tab_ref, o_ref):
    h = pl.program_id(0)
    kk = lax.broadcasted_iota(jnp.int32, (ATT_BLK, ATT_BLK), 0)
    qq = lax.broadcasted_iota(jnp.int32, (ATT_BLK, ATT_BLK), 1)
    nb = REL_BUCKETS // 2
    far = tab_ref[nb - 1, h]
    for t in range(2):
        rel = kk - qq - t * ATT_BLK
        n = jnp.abs(rel)
        large = nb // 2
        for thr in (12, 16, 23, 32, 46, 64, 91):
            large = large + (n >= thr).astype(jnp.int32)
        bucket = jnp.where(rel > 0, nb, 0) + jnp.where(n < nb // 2, n, large)
        bias = jnp.zeros((ATT_BLK, ATT_BLK), F32)
        for b in range(REL_BUCKETS):
            bias = jnp.where(bucket == b, tab_ref[b, h], bias)
        bias = bias - far
        if t == 0:
            visible = (kk // CHUNK) <= (qq // CHUNK)
            bias = jnp.where(visible, bias, NEG_INF)
        o_ref[0, t] = bias


def _bias_tiles(rel_table):
    return pl.pallas_call(
        _bias_kernel,
        grid=(DIFF_HEADS,),
        in_specs=[pl.BlockSpec(memory_space=pltpu.SMEM)],
        out_specs=pl.BlockSpec((1, 2, ATT_BLK, ATT_BLK), lambda h: (h, 0, 0, 0)),
        out_shape=jax.ShapeDtypeStruct((DIFF_HEADS, 2, ATT_BLK, ATT_BLK), F32),
        name="bias_tiles",
    )(rel_table)


def _attn_kernel(lam_ref, qt_ref, k_ref, vt_ref, bias_ref, g_ref, o_ref, acc1_ref, acc2_ref, ml_ref, *,
                 lambda_init):
    qi = pl.program_id(2)
    qt = qt_ref[0, 0]
    z = jnp.zeros((DIFF_DH, ATT_BLK), BF16)
    w = jnp.concatenate([jnp.concatenate([qt[:DIFF_DH], z], 0), jnp.concatenate([z, qt[DIFF_DH:]], 0)], 1)

    ml_ref[0:2, :] = jnp.full((2, ATT_BLK), NEG_INF, F32)
    ml_ref[2:4, :] = jnp.zeros((2, ATT_BLK), F32)
    acc1_ref[...] = jnp.zeros_like(acc1_ref)
    acc2_ref[...] = jnp.zeros_like(acc2_ref)

    def tile(kb, bias):
        kt = k_ref[0, pl.ds(pl.multiple_of(kb * ATT_BLK, ATT_BLK), ATT_BLK), :]
        vt = vt_ref[0, kb]
        s = jnp.dot(kt, w, preferred_element_type=F32)
        for i, acc_ref in ((0, acc1_ref), (1, acc2_ref)):
            si = s[:, i * ATT_BLK:(i + 1) * ATT_BLK]
            if bias is not None:
                si = si + bias
            m_old = ml_ref[i:i + 1, :]
            m_new = jnp.maximum(m_old, jnp.max(si, 0, keepdims=True))
            a = jnp.exp(m_old - m_new)
            p = jnp.exp(si - m_new)
            ml_ref[2 + i:3 + i, :] = a * ml_ref[2 + i:3 + i, :] + jnp.sum(p, 0, keepdims=True)
            ml_ref[i:i + 1, :] = m_new
            acc_ref[...] = a * acc_ref[...] + jnp.dot(vt, p.astype(BF16), preferred_element_type=F32)

    tile(qi, bias_ref[0, 0])

    @pl.when(qi >= 1)
    def _():
        tile(qi - 1, bias_ref[0, 1])

    def far_body(kb, carry):
        tile(kb, None)
        return carry

    lax.fori_loop(0, jnp.maximum(qi - 1, 0), far_body, 0)

    lam = (jnp.exp(jnp.sum(lam_ref[0:1, :] * lam_ref[1:2, :], keepdims=True))
           - jnp.exp(jnp.sum(lam_ref[2:3, :] * lam_ref[3:4, :], keepdims=True)) + lambda_init)
    ot = acc1_ref[...] / ml_ref[2:3, :] - lam * (acc2_ref[...] / ml_ref[3:4, :])
    ot = ot * lax.rsqrt(jnp.mean(ot * ot, 0, keepdims=True) + LN_EPS) * g_ref[...] * (1.0 - lambda_init)
    o_ref[0] = ot.T.astype(BF16)


def _diff_attention(qt, k12, vt, bias, lam4, g_sub, lambda_init):
    B, nqb, _, _ = qt.shape
    S = k12.shape[1]
    kernel = functools.partial(_attn_kernel, lambda_init=lambda_init)
    return pl.pallas_call(
        kernel,
        grid=(B, DIFF_HEADS, nqb),
        in_specs=[
            _full((4, DIFF_DH)),
            pl.BlockSpec((1, 1, 2 * DIFF_DH, ATT_BLK), lambda b, h, i: (b, i, h, 0)),
            pl.BlockSpec((1, S, 2 * DIFF_DH), lambda b, h, i: (b, 0, h)),
            pl.BlockSpec((1, nqb, DIFF_DV, ATT_BLK), lambda b, h, i: (b, 0, h, 0)),
            pl.BlockSpec((1, 2, ATT_BLK, ATT_BLK), lambda b, h, i: (h, 0, 0, 0)),
            _full((DIFF_DV, 1)),
        ],
        out_specs=pl.BlockSpec((1, ATT_BLK, DIFF_DV), lambda b, h, i: (b, i, h)),
        out_shape=jax.ShapeDtypeStruct((B, S, DIFF_HEADS * DIFF_DV), BF16),
        scratch_shapes=[pltpu.VMEM((DIFF_DV, ATT_BLK), F32), pltpu.VMEM((DIFF_DV, ATT_BLK), F32),
                        pltpu.VMEM((8, ATT_BLK), F32)],
        compiler_params=pltpu.CompilerParams(dimension_semantics=("parallel", "parallel", "arbitrary"),
                                             vmem_limit_bytes=VMEM_LIMIT),
        name="diff_attn",
    )(lam4, qt, k12, vt, bias, g_sub.reshape(-1, 1))


def _proj_ln_kernel(h_ref, a_ref, w_ref, lng_ref, lnb_ref, o_ref):
    mix = jnp.dot(a_ref[...], w_ref[...], preferred_element_type=F32)
    o_ref[...] = _layer_norm(DEEPNORM_ALPHA * h_ref[...] + mix, lng_ref[...], lnb_ref[...])


def _proj_ln(h, a, w_out, ln_g, ln_b, *, tb=512):
    T, D = h.shape
    return pl.pallas_call(
        _proj_ln_kernel,
        grid=(T // tb,),
        in_specs=[pl.BlockSpec((tb, D), lambda i: (i, 0)), pl.BlockSpec((tb, a.shape[1]), lambda i: (i, 0)),
                  _full(w_out.shape), _full((1, D)), _full((1, D))],
        out_specs=pl.BlockSpec((tb, D), lambda i: (i, 0)),
        out_shape=jax.ShapeDtypeStruct((T, D), F32),
        compiler_params=pltpu.CompilerParams(dimension_semantics=("parallel",), vmem_limit_bytes=VMEM_LIMIT),
        name="attn_out_ln",
    )(h, a, w_out.astype(BF16), ln_g.reshape(1, -1), ln_b.reshape(1, -1))


def kernel(x, a_w_in, a_w_gate2, a_b_gate, a_g_norm, a_w_out, kv_w, b_w_q, b_lam_q1, b_lam_k1, b_lam_q2, b_lam_k2, b_g_sub, b_w_out, rel_table, moe_w_group, moe_b_group, moe_w_router, moe_b_router, moe_w_gate, moe_w_up, moe_w_down, ln_g, ln_b):
    B, S, D = x.shape
    h = x
    bias = None
    k12 = vt = None
    for layer in range(DEPTH):
        if layer < N_A_LAYERS:
            h = _gla_layer(h, a_w_in[layer], a_w_gate2[layer], a_b_gate[layer], a_g_norm[layer],
                           a_w_out[layer], ln_g[layer, 0], ln_b[layer, 0])
        else:
            j = layer - N_A_LAYERS
            lambda_init = 0.8 - 0.6 * math.exp(-0.3 * layer)
            if bias is None:
                bias = _bias_tiles(rel_table)
            qt = _q_proj(h, b_w_q[j])
            lam4 = jnp.stack([b_lam_q1[j], b_lam_k1[j], b_lam_q2[j], b_lam_k2[j]])
            a = _diff_attention(qt, k12, vt, bias, lam4, b_g_sub[j], lambda_init)
            h = _proj_ln(h.reshape(B * S, D), a.reshape(B * S, -1), b_w_out[j], ln_g[layer, 0],
                         ln_b[layer, 0]).reshape(B, S, D)
        h = _moe_layer(h.reshape(B * S, D), moe_w_group[layer], moe_b_group[layer], moe_w_router[layer],
                       moe_b_router[layer], moe_w_gate[layer], moe_w_up[layer], moe_w_down[layer],
                       ln_g[layer, 1], ln_b[layer, 1]).reshape(B, S, D)
        if layer == N_A_LAYERS - 1:
            k12, vt = _kv_proj(h, kv_w)
    return h
```

```python
import functools
import math

import jax
import jax.numpy as jnp
from jax import lax
from jax.experimental import pallas as pl
from jax.experimental.pallas import tpu as pltpu

F32 = jnp.float32
BF16 = jnp.bfloat16

D_MODEL = 1024
DEPTH = 4
CHUNK = 64
N_A_LAYERS = 2

GLA_HEADS = 4
GLA_DK = 128
GLA_DV = 256
GLA_GATE_RANK = 16
GLA_GATE_TAU = 16.0
GLA_HK = GLA_HEADS * GLA_DK
GLA_HV = GLA_HEADS * GLA_DV

DIFF_HEADS = 8
DIFF_DH = 64
DIFF_DV = 128
DIFF_QK = DIFF_HEADS * DIFF_DH

REL_BUCKETS = 32

MOE_GROUPS = 4
MOE_EPG = 4
MOE_EXPERTS = 16
MOE_FF = 512

DEEPNORM_ALPHA = (2.0 * DEPTH) ** 0.25
LN_EPS = 1e-5
NEG_INF = -1e30

LANES = 128
ATT_BLK = 512
LOG2E = math.log2(math.e)
VMEM_LIMIT = 48 * 1024 * 1024

_HIGHEST = lax.Precision.HIGHEST


def _layer_norm(y, g, b):
    mu = jnp.mean(y, -1, keepdims=True)
    d = y - mu
    var = jnp.mean(d * d, -1, keepdims=True)
    return d * lax.rsqrt(var + LN_EPS) * g + b


def _sigmoid(x):
    return 1.0 / (1.0 + jnp.exp(-x))


def _full(shape):
    return pl.BlockSpec(shape, lambda *_: (0,) * len(shape))


def _gla_kernel(x_ref, w_main_ref, w_lr_ref, w_g2_ref, b_g_ref, gn_ref, w_out_ref, lng_ref, lnb_ref,
                tri_ref, o_ref, st_ref, mix_ref):
    tb = x_ref.shape[1]

    @pl.when(pl.program_id(1) == 0)
    def _():
        st_ref[...] = jnp.zeros_like(st_ref)

    x = x_ref[0]
    xb = x.astype(BF16)
    hp = jnp.dot(xb, w_main_ref[...], preferred_element_type=F32)
    lr = jnp.dot(xb, w_lr_ref[...], preferred_element_type=F32)
    glog = jnp.dot(lr.astype(BF16), w_g2_ref[...], preferred_element_type=F32) + b_g_ref[...]
    la = (jnp.minimum(glog, 0.0) - jnp.log(1.0 + jnp.exp(-jnp.abs(glog)))) * (1.0 / GLA_GATE_TAU)
    la_hi = la.astype(BF16)
    la_lo = (la - la_hi.astype(F32)).astype(BF16)
    tri = tri_ref[...]
    cum = (jnp.dot(tri, la_hi, preferred_element_type=F32)
           + jnp.dot(tri, la_lo, preferred_element_type=F32))

    for c in range(tb // CHUNK):
        r0 = c * CHUNK
        cum_c = cum[r0:r0 + CHUNK]
        last = cum_c[CHUNK - 1:CHUNK]
        kdec = (hp[r0:r0 + CHUNK, GLA_HK:2 * GLA_HK] * jnp.exp(last - cum_c)).astype(BF16)
        dec = jnp.exp(last)
        qc = (hp[r0:r0 + CHUNK, :GLA_HK] * (GLA_DK ** -0.5)).astype(BF16)
        vc = hp[r0:r0 + CHUNK, 2 * GLA_HK:2 * GLA_HK + GLA_HV].astype(BF16)
        for h in range(GLA_HEADS):
            ks = slice(h * GLA_DK, (h + 1) * GLA_DK)
            vs = slice(h * GLA_DV, (h + 1) * GLA_DV)
            upd = lax.dot_general(vc[:, vs], kdec[:, ks], (((0,), (0,)), ((), ())),
                                  preferred_element_type=F32)
            st = st_ref[h] * dec[:, ks] + upd
            st_ref[h] = st
            o = lax.dot_general(qc[:, ks], st.astype(BF16), (((1,), (1,)), ((), ())),
                                preferred_element_type=F32)
            o = o * lax.rsqrt(jnp.mean(o * o, -1, keepdims=True) + LN_EPS) * gn_ref[:, vs]
            mix_ref[r0:r0 + CHUNK, vs] = o

    r = hp[:, 2 * GLA_HK + GLA_HV:]
    gated = mix_ref[...] * (r * _sigmoid(r))
    mix = jnp.dot(gated.astype(BF16), w_out_ref[...], preferred_element_type=F32)
    o_ref[0] = _layer_norm(DEEPNORM_ALPHA * x + mix, lng_ref[...], lnb_ref[...])


def _gla_layer(h, w_in, w_gate2, b_gate, g_norm, w_out, ln_g, ln_b, *, tb=256):
    B, S, D = h.shape
    n_main = 2 * GLA_HK + 2 * GLA_HV
    w_main = w_in[:, :n_main].astype(BF16)
    w_lr = jnp.zeros((D, LANES), F32).at[:, :GLA_GATE_RANK].set(w_in[:, n_main:]).astype(BF16)
    w_g2 = jnp.zeros((LANES, GLA_HK), F32).at[:GLA_GATE_RANK].set(w_gate2).astype(BF16)
    idx = jnp.arange(tb)
    tri = ((idx[:, None] >= idx[None, :]) & (idx[:, None] // CHUNK == idx[None, :] // CHUNK)).astype(BF16)
    return pl.pallas_call(
        _gla_kernel,
        grid=(B, S // tb),
        in_specs=[
            pl.BlockSpec((1, tb, D), lambda b, j: (b, j, 0)),
            _full((D, n_main)), _full((D, LANES)), _full((LANES, GLA_HK)), _full((1, GLA_HK)),
            _full((1, GLA_HV)), _full((GLA_HV, D)), _full((1, D)), _full((1, D)), _full((tb, tb)),
        ],
        out_specs=pl.BlockSpec((1, tb, D), lambda b, j: (b, j, 0)),
        out_shape=jax.ShapeDtypeStruct((B, S, D), F32),
        scratch_shapes=[pltpu.VMEM((GLA_HEADS, GLA_DV, GLA_DK), F32), pltpu.VMEM((tb, GLA_HV), F32)],
        compiler_params=pltpu.CompilerParams(dimension_semantics=("parallel", "arbitrary"),
                                             vmem_limit_bytes=VMEM_LIMIT),
        name="gla_layer",
    )(h, w_main, w_lr, w_g2, b_gate.reshape(1, -1), g_norm.reshape(1, -1), w_out.astype(BF16),
      ln_g.reshape(1, -1), ln_b.reshape(1, -1), tri)


def _route(logits):
    lane = lax.broadcasted_iota(jnp.int32, logits.shape, 1)
    big = jnp.int32(1 << 20)
    gl = jnp.where(lane < MOE_GROUPS, logits, -jnp.inf)
    gmax = jnp.max(gl, -1, keepdims=True)
    gidx = jnp.min(jnp.where(gl == gmax, lane, big), -1, keepdims=True)
    gw = 1.0 / jnp.sum(jnp.exp(gl - gmax), -1, keepdims=True)
    lo = MOE_GROUPS + gidx * MOE_EPG
    emask = (lane >= lo) & (lane < lo + MOE_EPG)
    el = jnp.where(emask, logits, -jnp.inf)
    pe = jnp.exp(el - jnp.max(el, -1, keepdims=True))
    prob = pe / jnp.sum(pe, -1, keepdims=True)
    pm = jnp.where(emask, prob, -1.0)
    p1 = jnp.max(pm, -1, keepdims=True)
    i1 = jnp.min(jnp.where(pm == p1, lane, big), -1, keepdims=True)
    pm2 = jnp.where(lane == i1, -1.0, pm)
    p2 = jnp.max(pm2, -1, keepdims=True)
    i2 = jnp.min(jnp.where(pm2 == p2, lane, big), -1, keepdims=True)
    den = p1 + p2
    comb = jnp.where(lane == i1, p1 / den, 0.0) + jnp.where(lane == i2, p2 / den, 0.0)
    return comb * gw


def _moe_kernel(x_ref, wr_ref, br_ref, wg_ref, wu_ref, wd_ref, lng_ref, lnb_ref, o_ref, comb_ref, acc_ref):
    e = pl.program_id(1)
    x = x_ref[...]

    @pl.when(e == 0)
    def _():
        logits = jnp.dot(x, wr_ref[...], precision=_HIGHEST, preferred_element_type=F32) + br_ref[...]
        comb_ref[...] = _route(logits)
        acc_ref[...] = jnp.zeros_like(acc_ref)

    xb = x.astype(BF16)
    g = jnp.dot(xb, wg_ref[0], preferred_element_type=F32)
    u = jnp.dot(xb, wu_ref[0], preferred_element_type=F32)
    hdn = (g * _sigmoid(g)) * u
    y = jnp.dot(hdn.astype(BF16), wd_ref[0], preferred_element_type=F32)
    comb = comb_ref[...]
    lane = lax.broadcasted_iota(jnp.int32, comb.shape, 1)
    c = jnp.sum(jnp.where(lane == e + MOE_GROUPS, comb, 0.0), -1, keepdims=True)
    acc_ref[...] += c * y

    @pl.when(e == MOE_EXPERTS - 1)
    def _():
        o_ref[...] = _layer_norm(DEEPNORM_ALPHA * x + acc_ref[...], lng_ref[...], lnb_ref[...])


def _moe_layer(h, w_group, b_group, w_router, b_router, w_gate, w_up, w_down, ln_g, ln_b, *, tb=1024):
    T, D = h.shape
    wr = jnp.zeros((D, LANES), F32).at[:, :MOE_GROUPS].set(w_group)
    wr = wr.at[:, MOE_GROUPS:MOE_GROUPS + MOE_EXPERTS].set(w_router)
    br = jnp.zeros((1, LANES), F32).at[0, :MOE_GROUPS].set(b_group)
    br = br.at[0, MOE_GROUPS:MOE_GROUPS + MOE_EXPERTS].set(b_router.reshape(-1))
    return pl.pallas_call(
        _moe_kernel,
        grid=(T // tb, MOE_EXPERTS),
        in_specs=[
            pl.BlockSpec((tb, D), lambda i, e: (i, 0)),
            _full((D, LANES)), _full((1, LANES)),
            pl.BlockSpec((1, D, MOE_FF), lambda i, e: (e, 0, 0)),
            pl.BlockSpec((1, D, MOE_FF), lambda i, e: (e, 0, 0)),
            pl.BlockSpec((1, MOE_FF, D), lambda i, e: (e, 0, 0)),
            _full((1, D)), _full((1, D)),
        ],
        out_specs=pl.BlockSpec((tb, D), lambda i, e: (i, 0)),
        out_shape=jax.ShapeDtypeStruct((T, D), F32),
        scratch_shapes=[pltpu.VMEM((tb, LANES), F32), pltpu.VMEM((tb, D), F32)],
        compiler_params=pltpu.CompilerParams(dimension_semantics=("parallel", "arbitrary"),
                                             vmem_limit_bytes=VMEM_LIMIT),
        name="moe_layer",
    )(h, wr, br, w_gate.astype(BF16), w_up.astype(BF16), w_down.astype(BF16),
      ln_g.reshape(1, -1), ln_b.reshape(1, -1))


def _kv_kernel(x_ref, wk_ref, wvt_ref, k_ref, vt_ref):
    xb = x_ref[0].astype(BF16)
    k_ref[0] = jnp.dot(xb, wk_ref[...], preferred_element_type=F32).astype(BF16)
    vt = lax.dot_general(wvt_ref[...], xb, (((1,), (1,)), ((), ())), preferred_element_type=F32)
    for j in range(vt_ref.shape[1]):
        vt_ref[0, j] = vt[:, j * ATT_BLK:(j + 1) * ATT_BLK].astype(BF16)


def _kv_proj(h, kv_w, *, ts=512):
    B, S, D = h.shape
    k1 = kv_w[:, :DIFF_QK].reshape(D, DIFF_HEADS, DIFF_DH)
    k2 = kv_w[:, DIFF_QK:2 * DIFF_QK].reshape(D, DIFF_HEADS, DIFF_DH)
    wk = jnp.stack([k1, k2], axis=2).reshape(D, 2 * DIFF_QK).astype(BF16)
    wvt = kv_w[:, 2 * DIFF_QK:].T.astype(BF16)
    nv = DIFF_HEADS * DIFF_DV
    nj = ts // ATT_BLK
    return pl.pallas_call(
        _kv_kernel,
        grid=(B, S // ts),
        in_specs=[pl.BlockSpec((1, ts, D), lambda b, j: (b, j, 0)), _full((D, 2 * DIFF_QK)), _full((nv, D))],
        out_specs=[pl.BlockSpec((1, ts, 2 * DIFF_QK), lambda b, j: (b, j, 0)),
                   pl.BlockSpec((1, nj, nv, ATT_BLK), lambda b, j: (b, j, 0, 0))],
        out_shape=[jax.ShapeDtypeStruct((B, S, 2 * DIFF_QK), BF16),
                   jax.ShapeDtypeStruct((B, S // ATT_BLK, nv, ATT_BLK), BF16)],
        compiler_params=pltpu.CompilerParams(dimension_semantics=("parallel", "parallel"),
                                             vmem_limit_bytes=VMEM_LIMIT),
        name="kv_proj",
    )(h, wk, wvt)


def _q_kernel(x_ref, wqt_ref, qt_ref):
    xb = x_ref[0].astype(BF16)
    qt = lax.dot_general(wqt_ref[...], xb, (((1,), (1,)), ((), ())), preferred_element_type=F32)
    qt = qt * (DIFF_DH ** -0.5 * LOG2E)
    for j in range(qt_ref.shape[1]):
        qt_ref[0, j] = qt[:, j * ATT_BLK:(j + 1) * ATT_BLK].astype(BF16)


def _q_proj(h, w_q, *, ts=512):
    B, S, D = h.shape
    q1 = w_q[:, :DIFF_QK].reshape(D, DIFF_HEADS, DIFF_DH)
    q2 = w_q[:, DIFF_QK:].reshape(D, DIFF_HEADS, DIFF_DH)
    wqt = jnp.stack([q1, q2], axis=2).reshape(D, 2 * DIFF_QK).T.astype(BF16)
    nq = 2 * DIFF_QK
    nj = ts // ATT_BLK
    return pl.pallas_call(
        _q_kernel,
        grid=(B, S // ts),
        in_specs=[pl.BlockSpec((1, ts, D), lambda b, j: (b, j, 0)), _full((nq, D))],
        out_specs=pl.BlockSpec((1, nj, nq, ATT_BLK), lambda b, j: (b, j, 0, 0)),
        out_shape=jax.ShapeDtypeStruct((B, S // ATT_BLK, nq, ATT_BLK), BF16),
        compiler_params=pltpu.CompilerParams(dimension_semantics=("parallel", "parallel"),
                                             vmem_limit_bytes=VMEM_LIMIT),
        name="q_proj",
    )(h, wqt)


def _bias_kernel(tab_ref, o_ref):
    h = pl.program_id(0)
    kk = lax.broadcasted_iota(jnp.int32, (ATT_BLK, ATT_BLK), 0)
    qq = lax.broadcasted_iota(jnp.int32, (ATT_BLK, ATT_BLK), 1)
    nb = REL_BUCKETS // 2
    far = tab_ref[nb - 1, h]
    for t in range(2):
        rel = kk - qq - t * ATT_BLK
        n = jnp.abs(rel)
        large = nb // 2
        for thr in (12, 16, 23, 32, 46, 64, 91):
            large = large + (n >= thr).astype(jnp.int32)
        bucket = jnp.where(rel > 0, nb, 0) + jnp.where(n < nb // 2, n, large)
        bias = jnp.zeros((ATT_BLK, ATT_BLK), F32)
        for b in range(REL_BUCKETS):
            bias = jnp.where(bucket == b, tab_ref[b, h], bias)
        bias = (bias - far) * LOG2E
        if t == 0:
            visible = (kk // CHUNK) <= (qq // CHUNK)
            bias = jnp.where(visible, bias, NEG_INF)
        o_ref[0, t] = bias


def _bias_tiles(rel_table):
    return pl.pallas_call(
        _bias_kernel,
        grid=(DIFF_HEADS,),
        in_specs=[pl.BlockSpec(memory_space=pltpu.SMEM)],
        out_specs=pl.BlockSpec((1, 2, ATT_BLK, ATT_BLK), lambda h: (h, 0, 0, 0)),
        out_shape=jax.ShapeDtypeStruct((DIFF_HEADS, 2, ATT_BLK, ATT_BLK), F32),
        name="bias_tiles",
    )(rel_table)


def _attn_kernel(lam_ref, qt_ref, k_ref, vt_ref, bias_ref, g_ref, o_ref, w_ref, s0_ref, s1_ref, t0_ref, t1_ref,
                 acc_ref, ml_ref, *, lambda_init):
    blk = ATT_BLK
    qi = pl.program_id(2)
    n_far = jnp.maximum(qi - 1, 0)
    qt = qt_ref[0, 0]
    z = jnp.zeros((DIFF_DH, blk), BF16)
    w_ref[...] = jnp.concatenate(
        [jnp.concatenate([qt[:DIFF_DH], z], 0), jnp.concatenate([z, qt[DIFF_DH:]], 0)], 1)
    ml_ref[0:1, :] = jnp.full((1, 2 * blk), NEG_INF, F32)
    ml_ref[1:2, :] = jnp.zeros((1, 2 * blk), F32)
    acc_ref[...] = jnp.zeros_like(acc_ref)

    prev_block = jnp.maximum(qi - 1, 0)
    slots = ((s0_ref, t0_ref), (s1_ref, t1_ref))

    def scores(kb, slot, bias=None, extra=None):
        s_ref, t_ref = slots[slot]
        kt = k_ref[0, pl.ds(pl.multiple_of(kb * blk, blk), blk), :]
        s = jnp.dot(kt, w_ref[...], preferred_element_type=F32)
        if bias is not None:
            s = s + jnp.concatenate([bias, bias], axis=1)
        if extra is not None:
            s = s + extra
        s_ref[...] = s
        t_ref[0:1, :] = jnp.max(s, 0, keepdims=True)

    def accumulate(kb, slot):
        s_ref, t_ref = slots[slot]
        m_old = ml_ref[0:1, :]
        m_new = jnp.maximum(m_old, t_ref[0:1, :])
        a = jnp.exp2(m_old - m_new)
        p = jnp.exp2(s_ref[...] - m_new)
        ml_ref[1:2, :] = a * ml_ref[1:2, :] + jnp.sum(p, 0, keepdims=True)
        ml_ref[0:1, :] = m_new
        pb = p.astype(BF16)
        vt = vt_ref[0, kb]
        for i in range(2):
            cols = slice(i * blk, (i + 1) * blk)
            acc_ref[i] = a[:, cols] * acc_ref[i] + jnp.dot(vt, pb[:, cols], preferred_element_type=F32)

    scores(qi, 0, bias=bias_ref[0, 0])
    scores(prev_block, 1, bias=bias_ref[0, 1], extra=jnp.where(qi >= 1, 0.0, -jnp.inf).astype(F32))
    accumulate(qi, 0)

    def pair(j, carry):
        kb = 2 * j
        scores(kb, 0)
        accumulate(jnp.where(j == 0, prev_block, kb - 1), 1)
        scores(kb + 1, 1)
        accumulate(kb, 0)
        return carry

    n_pairs = n_far // 2
    lax.fori_loop(0, n_pairs, pair, 0)
    last_slot1 = jnp.where(n_pairs == 0, prev_block, 2 * n_pairs - 1)

    @pl.when(n_far % 2 == 1)
    def _():
        scores(n_far - 1, 0)
        accumulate(last_slot1, 1)
        accumulate(n_far - 1, 0)

    @pl.when(n_far % 2 == 0)
    def _():
        accumulate(last_slot1, 1)


    lam = (jnp.exp(jnp.sum(lam_ref[0:1, :] * lam_ref[1:2, :], keepdims=True))
           - jnp.exp(jnp.sum(lam_ref[2:3, :] * lam_ref[3:4, :], keepdims=True)) + lambda_init)
    l = ml_ref[1:2, :]
    ot = acc_ref[0] / l[:, :blk] - lam * (acc_ref[1] / l[:, blk:])
    ot = ot * lax.rsqrt(jnp.mean(ot * ot, 0, keepdims=True) + LN_EPS) * g_ref[...] * (1.0 - lambda_init)
    o_ref[0] = ot.T.astype(BF16)


def _diff_attention(qt, k12, vt, bias, lam4, g_sub, lambda_init):
    B, nqb, _, _ = qt.shape
    S = k12.shape[1]
    kernel = functools.partial(_attn_kernel, lambda_init=lambda_init)
    return pl.pallas_call(
        kernel,
        grid=(B, DIFF_HEADS, nqb),
        in_specs=[
            _full((4, DIFF_DH)),
            pl.BlockSpec((1, 1, 2 * DIFF_DH, ATT_BLK), lambda b, h, i: (b, i, h, 0)),
            pl.BlockSpec((1, S, 2 * DIFF_DH), lambda b, h, i: (b, 0, h)),
            pl.BlockSpec((1, nqb, DIFF_DV, ATT_BLK), lambda b, h, i: (b, 0, h, 0)),
            pl.BlockSpec((1, 2, ATT_BLK, ATT_BLK), lambda b, h, i: (h, 0, 0, 0)),
            _full((DIFF_DV, 1)),
        ],
        out_specs=pl.BlockSpec((1, ATT_BLK, DIFF_DV), lambda b, h, i: (b, i, h)),
        out_shape=jax.ShapeDtypeStruct((B, S, DIFF_HEADS * DIFF_DV), BF16),
        scratch_shapes=[pltpu.VMEM((2 * DIFF_DH, 2 * ATT_BLK), BF16),
                        pltpu.VMEM((ATT_BLK, 2 * ATT_BLK), F32),
                        pltpu.VMEM((ATT_BLK, 2 * ATT_BLK), F32),
                        pltpu.VMEM((8, 2 * ATT_BLK), F32),
                        pltpu.VMEM((8, 2 * ATT_BLK), F32),
                        pltpu.VMEM((2, DIFF_DV, ATT_BLK), F32),
                        pltpu.VMEM((8, 2 * ATT_BLK), F32)],
        compiler_params=pltpu.CompilerParams(dimension_semantics=("parallel", "parallel", "arbitrary"),
                                             vmem_limit_bytes=VMEM_LIMIT),
        name="diff_attn",
    )(lam4, qt, k12, vt, bias, g_sub.reshape(-1, 1))


def _proj_ln_kernel(h_ref, a_ref, w_ref, lng_ref, lnb_ref, o_ref):
    mix = jnp.dot(a_ref[...], w_ref[...], preferred_element_type=F32)
    o_ref[...] = _layer_norm(DEEPNORM_ALPHA * h_ref[...] + mix, lng_ref[...], lnb_ref[...])


def _proj_ln(h, a, w_out, ln_g, ln_b, *, tb=512):
    T, D = h.shape
    return pl.pallas_call(
        _proj_ln_kernel,
        grid=(T // tb,),
        in_specs=[pl.BlockSpec((tb, D), lambda i: (i, 0)), pl.BlockSpec((tb, a.shape[1]), lambda i: (i, 0)),
                  _full(w_out.shape), _full((1, D)), _full((1, D))],
        out_specs=pl.BlockSpec((tb, D), lambda i: (i, 0)),
        out_shape=jax.ShapeDtypeStruct((T, D), F32),
        compiler_params=pltpu.CompilerParams(dimension_semantics=("parallel",), vmem_limit_bytes=VMEM_LIMIT),
        name="attn_out_ln",
    )(h, a, w_out.astype(BF16), ln_g.reshape(1, -1), ln_b.reshape(1, -1))


def kernel(x, a_w_in, a_w_gate2, a_b_gate, a_g_norm, a_w_out, kv_w, b_w_q, b_lam_q1, b_lam_k1, b_lam_q2, b_lam_k2, b_g_sub, b_w_out, rel_table, moe_w_group, moe_b_group, moe_w_router, moe_b_router, moe_w_gate, moe_w_up, moe_w_down, ln_g, ln_b):
    B, S, D = x.shape
    h = x
    bias = None
    k12 = vt = None
    for layer in range(DEPTH):
        if layer < N_A_LAYERS:
            h = _gla_layer(h, a_w_in[layer], a_w_gate2[layer], a_b_gate[layer], a_g_norm[layer],
                           a_w_out[layer], ln_g[layer, 0], ln_b[layer, 0])
        else:
            j = layer - N_A_LAYERS
            lambda_init = 0.8 - 0.6 * math.exp(-0.3 * layer)
            if bias is None:
                bias = _bias_tiles(rel_table)
            qt = _q_proj(h, b_w_q[j])
            lam4 = jnp.stack([b_lam_q1[j], b_lam_k1[j], b_lam_q2[j], b_lam_k2[j]])
            a = _diff_attention(qt, k12, vt, bias, lam4, b_g_sub[j], lambda_init)
            h = _proj_ln(h.reshape(B * S, D), a.reshape(B * S, -1), b_w_out[j], ln_g[layer, 0],
                         ln_b[layer, 0]).reshape(B, S, D)
        h = _moe_layer(h.reshape(B * S, D), moe_w_group[layer], moe_b_group[layer], moe_w_router[layer],
                       moe_b_router[layer], moe_w_gate[layer], moe_w_up[layer], moe_w_down[layer],
                       ln_g[layer, 1], ln_b[layer, 1]).reshape(B, S, D)
        if layer == N_A_LAYERS - 1:
            k12, vt = _kv_proj(h, kv_w)
    return h
```

```python
import functools
import math

import jax
import jax.numpy as jnp
from jax import lax
from jax.experimental import pallas as pl
from jax.experimental.pallas import tpu as pltpu

F32 = jnp.float32
BF16 = jnp.bfloat16

D_MODEL = 1024
DEPTH = 4
CHUNK = 64
N_A_LAYERS = 2

GLA_HEADS = 4
GLA_DK = 128
GLA_DV = 256
GLA_GATE_RANK = 16
GLA_GATE_TAU = 16.0
GLA_HK = GLA_HEADS * GLA_DK
GLA_HV = GLA_HEADS * GLA_DV

DIFF_HEADS = 8
DIFF_DH = 64
DIFF_DV = 128
DIFF_QK = DIFF_HEADS * DIFF_DH

REL_BUCKETS = 32

MOE_GROUPS = 4
MOE_EPG = 4
MOE_EXPERTS = 16
MOE_FF = 512

DEEPNORM_ALPHA = (2.0 * DEPTH) ** 0.25
LN_EPS = 1e-5
NEG_INF = -1e30

LANES = 128
ATT_BLK = 512
LOG2E = math.log2(math.e)
VMEM_LIMIT = 48 * 1024 * 1024
MOE_VMEM_LIMIT = 58 * 1024 * 1024
MOE_SUB = 128
MOE_EPS = 2

_HIGHEST = lax.Precision.HIGHEST


def _layer_norm(y, g, b):
    mu = jnp.mean(y, -1, keepdims=True)
    d = y - mu
    var = jnp.mean(d * d, -1, keepdims=True)
    return d * lax.rsqrt(var + LN_EPS) * g + b


def _sigmoid(x):
    return 1.0 / (1.0 + jnp.exp(-x))


def _full(shape):
    return pl.BlockSpec(shape, lambda *_: (0,) * len(shape))


def _gla_kernel(x_ref, w_main_ref, w_lr_ref, w_g2_ref, b_g_ref, gn_ref, w_out_ref, lng_ref, lnb_ref,
                tri_ref, o_ref, st_ref, mix_ref):
    tb = x_ref.shape[1]

    @pl.when(pl.program_id(1) == 0)
    def _():
        st_ref[...] = jnp.zeros_like(st_ref)

    x = x_ref[0]
    xb = x.astype(BF16)
    hp = jnp.dot(xb, w_main_ref[...], preferred_element_type=F32)
    lr = jnp.dot(xb, w_lr_ref[...], preferred_element_type=F32)
    glog = jnp.dot(lr.astype(BF16), w_g2_ref[...], preferred_element_type=F32) + b_g_ref[...]
    la = (jnp.minimum(glog, 0.0) - jnp.log(1.0 + jnp.exp(-jnp.abs(glog)))) * (1.0 / GLA_GATE_TAU)
    la_hi = la.astype(BF16)
    la_lo = (la - la_hi.astype(F32)).astype(BF16)
    tri = tri_ref[...]
    cum = (jnp.dot(tri, la_hi, preferred_element_type=F32)
           + jnp.dot(tri, la_lo, preferred_element_type=F32))

    for c in range(tb // CHUNK):
        r0 = c * CHUNK
        cum_c = cum[r0:r0 + CHUNK]
        last = cum_c[CHUNK - 1:CHUNK]
        kdec = (hp[r0:r0 + CHUNK, GLA_HK:2 * GLA_HK] * jnp.exp(last - cum_c)).astype(BF16)
        dec = jnp.exp(last)
        qc = (hp[r0:r0 + CHUNK, :GLA_HK] * (GLA_DK ** -0.5)).astype(BF16)
        vc = hp[r0:r0 + CHUNK, 2 * GLA_HK:2 * GLA_HK + GLA_HV].astype(BF16)
        for h in range(GLA_HEADS):
            ks = slice(h * GLA_DK, (h + 1) * GLA_DK)
            vs = slice(h * GLA_DV, (h + 1) * GLA_DV)
            upd = lax.dot_general(vc[:, vs], kdec[:, ks], (((0,), (0,)), ((), ())),
                                  preferred_element_type=F32)
            st = st_ref[h] * dec[:, ks] + upd
            st_ref[h] = st
            o = lax.dot_general(qc[:, ks], st.astype(BF16), (((1,), (1,)), ((), ())),
                                preferred_element_type=F32)
            o = o * lax.rsqrt(jnp.mean(o * o, -1, keepdims=True) + LN_EPS) * gn_ref[:, vs]
            mix_ref[r0:r0 + CHUNK, vs] = o

    r = hp[:, 2 * GLA_HK + GLA_HV:]
    gated = mix_ref[...] * (r * _sigmoid(r))
    mix = jnp.dot(gated.astype(BF16), w_out_ref[...], preferred_element_type=F32)
    o_ref[0] = _layer_norm(DEEPNORM_ALPHA * x + mix, lng_ref[...], lnb_ref[...])


def _gla_layer(h, w_in, w_gate2, b_gate, g_norm, w_out, ln_g, ln_b, *, tb=256):
    B, S, D = h.shape
    n_main = 2 * GLA_HK + 2 * GLA_HV
    w_main = w_in[:, :n_main].astype(BF16)
    w_lr = jnp.zeros((D, LANES), F32).at[:, :GLA_GATE_RANK].set(w_in[:, n_main:]).astype(BF16)
    w_g2 = jnp.zeros((LANES, GLA_HK), F32).at[:GLA_GATE_RANK].set(w_gate2).astype(BF16)
    idx = jnp.arange(tb)
    tri = ((idx[:, None] >= idx[None, :]) & (idx[:, None] // CHUNK == idx[None, :] // CHUNK)).astype(BF16)
    return pl.pallas_call(
        _gla_kernel,
        grid=(B, S // tb),
        in_specs=[
            pl.BlockSpec((1, tb, D), lambda b, j: (b, j, 0)),
            _full((D, n_main)), _full((D, LANES)), _full((LANES, GLA_HK)), _full((1, GLA_HK)),
            _full((1, GLA_HV)), _full((GLA_HV, D)), _full((1, D)), _full((1, D)), _full((tb, tb)),
        ],
        out_specs=pl.BlockSpec((1, tb, D), lambda b, j: (b, j, 0)),
        out_shape=jax.ShapeDtypeStruct((B, S, D), F32),
        scratch_shapes=[pltpu.VMEM((GLA_HEADS, GLA_DV, GLA_DK), F32), pltpu.VMEM((tb, GLA_HV), F32)],
        compiler_params=pltpu.CompilerParams(dimension_semantics=("parallel", "arbitrary"),
                                             vmem_limit_bytes=VMEM_LIMIT),
        name="gla_layer",
    )(h, w_main, w_lr, w_g2, b_gate.reshape(1, -1), g_norm.reshape(1, -1), w_out.astype(BF16),
      ln_g.reshape(1, -1), ln_b.reshape(1, -1), tri)


def _route(logits):
    lane = lax.broadcasted_iota(jnp.int32, logits.shape, 1)
    big = jnp.int32(1 << 20)
    gl = jnp.where(lane < MOE_GROUPS, logits, -jnp.inf)
    gmax = jnp.max(gl, -1, keepdims=True)
    gidx = jnp.min(jnp.where(gl == gmax, lane, big), -1, keepdims=True)
    gw = 1.0 / jnp.sum(jnp.exp(gl - gmax), -1, keepdims=True)
    lo = MOE_GROUPS + gidx * MOE_EPG
    emask = (lane >= lo) & (lane < lo + MOE_EPG)
    el = jnp.where(emask, logits, -jnp.inf)
    pe = jnp.exp(el - jnp.max(el, -1, keepdims=True))
    prob = pe / jnp.sum(pe, -1, keepdims=True)
    pm = jnp.where(emask, prob, -1.0)
    p1 = jnp.max(pm, -1, keepdims=True)
    i1 = jnp.min(jnp.where(pm == p1, lane, big), -1, keepdims=True)
    pm2 = jnp.where(lane == i1, -1.0, pm)
    p2 = jnp.max(pm2, -1, keepdims=True)
    i2 = jnp.min(jnp.where(pm2 == p2, lane, big), -1, keepdims=True)
    den = p1 + p2
    comb = jnp.where(lane == i1, p1 / den, 0.0) + jnp.where(lane == i2, p2 / den, 0.0)
    return comb * gw, gidx


def _split3(v):
    hi = v.astype(BF16).astype(F32)
    mid = (v - hi).astype(BF16).astype(F32)
    lo = (v - hi - mid).astype(BF16).astype(F32)
    return hi, mid, lo


def _moe_kernel(x_ref, wr_ref, br_ref, wg_ref, wu_ref, wd_ref, lng_ref, lnb_ref, o_ref,
                xs_ref, cw_ref, pt_ref, y_ref, meta_ref):
    step = pl.program_id(1)
    steps_per_group = MOE_EPG // MOE_EPS
    grp = step // steps_per_group
    j0 = (step % steps_per_group) * MOE_EPS
    tb = x_ref.shape[0]
    tbs = xs_ref.shape[0]
    d = x_ref.shape[1]

    @pl.when(step == 0)
    def _():
        x = x_ref[...]
        xh = x.astype(BF16)
        xl = (x - xh.astype(F32)).astype(BF16)
        both = jnp.dot(jnp.concatenate([xh, xl], axis=1), wr_ref[...], preferred_element_type=F32)
        logits = both[:, :LANES] + both[:, LANES:] + br_ref[...]
        comb, gidx = _route(logits)
        lane = lax.broadcasted_iota(jnp.int32, (tb, LANES), 1)
        lane1 = lax.broadcasted_iota(jnp.int32, (1, LANES), 1)
        onehot = jnp.where(lane == gidx, 1.0, 0.0)
        earlier = jnp.where(lax.broadcasted_iota(jnp.int32, (tb, tb), 0)
                            > lax.broadcasted_iota(jnp.int32, (tb, tb), 1), 1.0, 0.0).astype(BF16)
        rank = jnp.sum(jnp.dot(earlier, onehot.astype(BF16), preferred_element_type=F32) * onehot,
                       -1, keepdims=True)
        cnt = jnp.sum(onehot, 0, keepdims=True)
        ntile = jnp.floor((cnt + (MOE_SUB - 1.0)) * (1.0 / MOE_SUB))
        off = jnp.zeros((1, LANES), F32)
        for gg in range(MOE_GROUPS - 1):
            rows_g = jnp.sum(jnp.where(lane1 == gg, ntile, 0.0), -1, keepdims=True) * MOE_SUB
            off = off + jnp.where(lane1 > gg, rows_g, 0.0)
        for gg in range(MOE_GROUPS):
            meta_ref[gg] = jnp.sum(jnp.where(lane1 == gg, ntile, 0.0)).astype(jnp.int32)
            meta_ref[MOE_GROUPS + gg] = jnp.sum(jnp.where(lane1 == gg, off, 0.0)).astype(jnp.int32)
        pos = jnp.sum(onehot * off, -1, keepdims=True) + rank
        pt_ref[...] = jnp.where(lax.broadcasted_iota(jnp.int32, (tb, tbs), 1) == pos.astype(jnp.int32),
                                1.0, 0.0).astype(BF16)
        pos_row = jnp.broadcast_to(pos, (tb, LANES)).T[0:1, :].astype(jnp.int32)
        perm = jnp.where(lax.broadcasted_iota(jnp.int32, (tbs, tb), 0) == pos_row, 1.0, 0.0).astype(BF16)
        lo = MOE_GROUPS + gidx * MOE_EPG
        cw = jnp.zeros((tb, LANES), F32)
        for jj in range(MOE_EPG):
            parts = _split3(jnp.sum(jnp.where(lane == lo + jj, comb, 0.0), -1, keepdims=True))
            for k, part in enumerate(parts):
                cw = cw + jnp.where(lane == jj + k * MOE_EPG, part, 0.0)
        sorted_all = jnp.dot(perm, jnp.concatenate([xh, cw.astype(BF16)], axis=1),
                             preferred_element_type=F32)
        xs_ref[...] = sorted_all[:, :d].astype(BF16)
        cw_ref[...] = sorted_all[:, d:]
        y_ref[...] = jnp.zeros_like(y_ref)

    base = meta_ref[MOE_GROUPS + grp]

    def tile(i, carry):
        rows = pl.ds(pl.multiple_of(base + i * MOE_SUB, MOE_SUB), MOE_SUB)
        xt = xs_ref[rows, :]
        cwt = cw_ref[rows, :]
        lane = lax.broadcasted_iota(jnp.int32, (MOE_SUB, LANES), 1)
        total = None
        for k in range(MOE_EPS):
            gt = jnp.dot(xt, wg_ref[k], preferred_element_type=F32)
            ut = jnp.dot(xt, wu_ref[k], preferred_element_type=F32)
            hdn = (gt * _sigmoid(gt)) * ut
            y = jnp.dot(hdn.astype(BF16), wd_ref[k], preferred_element_type=F32)
            mine = (lane % MOE_EPG == j0 + k) & (lane < 3 * MOE_EPG)
            c = jnp.sum(jnp.where(mine, cwt, 0.0), -1, keepdims=True)
            total = c * y if total is None else total + c * y
        y_ref[rows, :] += total
        return carry

    lax.fori_loop(0, meta_ref[grp], tile, 0)

    @pl.when(step == MOE_EXPERTS // MOE_EPS - 1)
    def _():
        ffn = jnp.dot(pt_ref[...], y_ref[...].astype(BF16), preferred_element_type=F32)
        o_ref[...] = _layer_norm(DEEPNORM_ALPHA * x_ref[...] + ffn, lng_ref[...], lnb_ref[...])


def _moe_layer(h, w_group, b_group, w_router, b_router, w_gate, w_up, w_down, ln_g, ln_b, *, tb=1024):
    T, D = h.shape
    wr = jnp.zeros((D, LANES), F32).at[:, :MOE_GROUPS].set(w_group)
    wr = wr.at[:, MOE_GROUPS:MOE_GROUPS + MOE_EXPERTS].set(w_router)
    wrh = wr.astype(BF16)
    wrl = (wr - wrh.astype(F32)).astype(BF16)
    wr2 = jnp.concatenate([jnp.concatenate([wrh, wrl], axis=1),
                           jnp.concatenate([wrh, jnp.zeros_like(wrl)], axis=1)], axis=0)
    br = jnp.zeros((1, LANES), F32).at[0, :MOE_GROUPS].set(b_group)
    br = br.at[0, MOE_GROUPS:MOE_GROUPS + MOE_EXPERTS].set(b_router.reshape(-1))
    tbs = tb + MOE_GROUPS * MOE_SUB
    return pl.pallas_call(
        _moe_kernel,
        grid=(T // tb, MOE_EXPERTS // MOE_EPS),
        in_specs=[
            pl.BlockSpec((tb, D), lambda i, e: (i, 0)),
            _full((2 * D, 2 * LANES)), _full((1, LANES)),
            pl.BlockSpec((MOE_EPS, D, MOE_FF), lambda i, e: (e, 0, 0)),
            pl.BlockSpec((MOE_EPS, D, MOE_FF), lambda i, e: (e, 0, 0)),
            pl.BlockSpec((MOE_EPS, MOE_FF, D), lambda i, e: (e, 0, 0)),
            _full((1, D)), _full((1, D)),
        ],
        out_specs=pl.BlockSpec((tb, D), lambda i, e: (i, 0)),
        out_shape=jax.ShapeDtypeStruct((T, D), F32),
        scratch_shapes=[pltpu.VMEM((tbs, D), BF16), pltpu.VMEM((tbs, LANES), F32), pltpu.VMEM((tb, tbs), BF16),
                        pltpu.VMEM((tbs, D), F32), pltpu.SMEM((2 * MOE_GROUPS,), jnp.int32)],
        compiler_params=pltpu.CompilerParams(dimension_semantics=("parallel", "arbitrary"),
                                             vmem_limit_bytes=MOE_VMEM_LIMIT),
        name="moe_layer",
    )(h, wr2, br, w_gate.astype(BF16), w_up.astype(BF16), w_down.astype(BF16),
      ln_g.reshape(1, -1), ln_b.reshape(1, -1))


def _kv_kernel(x_ref, wk_ref, wvt_ref, k_ref, vt_ref):
    xb = x_ref[0].astype(BF16)
    k_ref[0] = jnp.dot(xb, wk_ref[...], preferred_element_type=F32).astype(BF16)
    vt = lax.dot_general(wvt_ref[...], xb, (((1,), (1,)), ((), ())), preferred_element_type=F32)
    for j in range(vt_ref.shape[1]):
        vt_ref[0, j] = vt[:, j * ATT_BLK:(j + 1) * ATT_BLK].astype(BF16)


def _kv_proj(h, kv_w, *, ts=512):
    B, S, D = h.shape
    k1 = kv_w[:, :DIFF_QK].reshape(D, DIFF_HEADS, DIFF_DH)
    k2 = kv_w[:, DIFF_QK:2 * DIFF_QK].reshape(D, DIFF_HEADS, DIFF_DH)
    wk = jnp.stack([k1, k2], axis=2).reshape(D, 2 * DIFF_QK).astype(BF16)
    wvt = kv_w[:, 2 * DIFF_QK:].T.astype(BF16)
    nv = DIFF_HEADS * DIFF_DV
    nj = ts // ATT_BLK
    return pl.pallas_call(
        _kv_kernel,
        grid=(B, S // ts),
        in_specs=[pl.BlockSpec((1, ts, D), lambda b, j: (b, j, 0)), _full((D, 2 * DIFF_QK)), _full((nv, D))],
        out_specs=[pl.BlockSpec((1, ts, 2 * DIFF_QK), lambda b, j: (b, j, 0)),
                   pl.BlockSpec((1, nj, nv, ATT_BLK), lambda b, j: (b, j, 0, 0))],
        out_shape=[jax.ShapeDtypeStruct((B, S, 2 * DIFF_QK), BF16),
                   jax.ShapeDtypeStruct((B, S // ATT_BLK, nv, ATT_BLK), BF16)],
        compiler_params=pltpu.CompilerParams(dimension_semantics=("parallel", "parallel"),
                                             vmem_limit_bytes=VMEM_LIMIT),
        name="kv_proj",
    )(h, wk, wvt)


def _q_kernel(x_ref, wqt_ref, qt_ref):
    xb = x_ref[0].astype(BF16)
    qt = lax.dot_general(wqt_ref[...], xb, (((1,), (1,)), ((), ())), preferred_element_type=F32)
    qt = qt * (DIFF_DH ** -0.5 * LOG2E)
    for j in range(qt_ref.shape[1]):
        qt_ref[0, j] = qt[:, j * ATT_BLK:(j + 1) * ATT_BLK].astype(BF16)


def _q_proj(h, w_q, *, ts=512):
    B, S, D = h.shape
    q1 = w_q[:, :DIFF_QK].reshape(D, DIFF_HEADS, DIFF_DH)
    q2 = w_q[:, DIFF_QK:].reshape(D, DIFF_HEADS, DIFF_DH)
    wqt = jnp.stack([q1, q2], axis=2).reshape(D, 2 * DIFF_QK).T.astype(BF16)
    nq = 2 * DIFF_QK
    nj = ts // ATT_BLK
    return pl.pallas_call(
        _q_kernel,
        grid=(B, S // ts),
        in_specs=[pl.BlockSpec((1, ts, D), lambda b, j: (b, j, 0)), _full((nq, D))],
        out_specs=pl.BlockSpec((1, nj, nq, ATT_BLK), lambda b, j: (b, j, 0, 0)),
        out_shape=jax.ShapeDtypeStruct((B, S // ATT_BLK, nq, ATT_BLK), BF16),
        compiler_params=pltpu.CompilerParams(dimension_semantics=("parallel", "parallel"),
                                             vmem_limit_bytes=VMEM_LIMIT),
        name="q_proj",
    )(h, wqt)


def _bias_kernel(tab_ref, o_ref):
    h = pl.program_id(0)
    kk = lax.broadcasted_iota(jnp.int32, (ATT_BLK, ATT_BLK), 0)
    qq = lax.broadcasted_iota(jnp.int32, (ATT_BLK, ATT_BLK), 1)
    nb = REL_BUCKETS // 2
    far = tab_ref[nb - 1, h]
    for t in range(2):
        rel = kk - qq - t * ATT_BLK
        n = jnp.abs(rel)
        large = nb // 2
        for thr in (12, 16, 23, 32, 46, 64, 91):
            large = large + (n >= thr).astype(jnp.int32)
        bucket = jnp.where(rel > 0, nb, 0) + jnp.where(n < nb // 2, n, large)
        bias = jnp.zeros((ATT_BLK, ATT_BLK), F32)
        for b in range(REL_BUCKETS):
            bias = jnp.where(bucket == b, tab_ref[b, h], bias)
        bias = (bias - far) * LOG2E
        if t == 0:
            visible = (kk // CHUNK) <= (qq // CHUNK)
            bias = jnp.where(visible, bias, NEG_INF)
        o_ref[0, t] = bias


def _bias_tiles(rel_table):
    return pl.pallas_call(
        _bias_kernel,
        grid=(DIFF_HEADS,),
        in_specs=[pl.BlockSpec(memory_space=pltpu.SMEM)],
        out_specs=pl.BlockSpec((1, 2, ATT_BLK, ATT_BLK), lambda h: (h, 0, 0, 0)),
        out_shape=jax.ShapeDtypeStruct((DIFF_HEADS, 2, ATT_BLK, ATT_BLK), F32),
        name="bias_tiles",
    )(rel_table)


def _attn_kernel(lam_ref, qt_ref, k_ref, vt_ref, bias_ref, g_ref, o_ref, w_ref, s0_ref, s1_ref, t0_ref, t1_ref,
                 acc_ref, ml_ref, *, lambda_init):
    blk = ATT_BLK
    qi = pl.program_id(2)
    n_far = jnp.maximum(qi - 1, 0)
    qt = qt_ref[0, 0]
    z = jnp.zeros((DIFF_DH, blk), BF16)
    w_ref[...] = jnp.concatenate(
        [jnp.concatenate([qt[:DIFF_DH], z], 0), jnp.concatenate([z, qt[DIFF_DH:]], 0)], 1)
    ml_ref[0:1, :] = jnp.full((1, 2 * blk), NEG_INF, F32)
    ml_ref[1:2, :] = jnp.zeros((1, 2 * blk), F32)
    acc_ref[...] = jnp.zeros_like(acc_ref)

    prev_block = jnp.maximum(qi - 1, 0)
    slots = ((s0_ref, t0_ref), (s1_ref, t1_ref))

    def scores(kb, slot, bias=None, extra=None):
        s_ref, t_ref = slots[slot]
        kt = k_ref[0, pl.ds(pl.multiple_of(kb * blk, blk), blk), :]
        s = jnp.dot(kt, w_ref[...], preferred_element_type=F32)
        if bias is not None:
            s = s + jnp.concatenate([bias, bias], axis=1)
        if extra is not None:
            s = s + extra
        s_ref[...] = s
        t_ref[0:1, :] = jnp.max(s, 0, keepdims=True)

    def accumulate(kb, slot):
        s_ref, t_ref = slots[slot]
        m_old = ml_ref[0:1, :]
        m_new = jnp.maximum(m_old, t_ref[0:1, :])
        a = jnp.exp2(m_old - m_new)
        p = jnp.exp2(s_ref[...] - m_new)
        ml_ref[1:2, :] = a * ml_ref[1:2, :] + jnp.sum(p, 0, keepdims=True)
        ml_ref[0:1, :] = m_new
        pb = p.astype(BF16)
        vt = vt_ref[0, kb]
        for i in range(2):
            cols = slice(i * blk, (i + 1) * blk)
            acc_ref[i] = a[:, cols] * acc_ref[i] + jnp.dot(vt, pb[:, cols], preferred_element_type=F32)

    scores(qi, 0, bias=bias_ref[0, 0])
    scores(prev_block, 1, bias=bias_ref[0, 1], extra=jnp.where(qi >= 1, 0.0, -jnp.inf).astype(F32))
    accumulate(qi, 0)

    def pair(j, carry):
        kb = 2 * j
        scores(kb, 0)
        accumulate(jnp.where(j == 0, prev_block, kb - 1), 1)
        scores(kb + 1, 1)
        accumulate(kb, 0)
        return carry

    n_pairs = n_far // 2
    lax.fori_loop(0, n_pairs, pair, 0)
    last_slot1 = jnp.where(n_pairs == 0, prev_block, 2 * n_pairs - 1)

    @pl.when(n_far % 2 == 1)
    def _():
        scores(n_far - 1, 0)
        accumulate(last_slot1, 1)
        accumulate(n_far - 1, 0)

    @pl.when(n_far % 2 == 0)
    def _():
        accumulate(last_slot1, 1)


    lam = (jnp.exp(jnp.sum(lam_ref[0:1, :] * lam_ref[1:2, :], keepdims=True))
           - jnp.exp(jnp.sum(lam_ref[2:3, :] * lam_ref[3:4, :], keepdims=True)) + lambda_init)
    l = ml_ref[1:2, :]
    ot = acc_ref[0] / l[:, :blk] - lam * (acc_ref[1] / l[:, blk:])
    ot = ot * lax.rsqrt(jnp.mean(ot * ot, 0, keepdims=True) + LN_EPS) * g_ref[...] * (1.0 - lambda_init)
    o_ref[0] = ot.T.astype(BF16)


def _diff_attention(qt, k12, vt, bias, lam4, g_sub, lambda_init):
    B, nqb, _, _ = qt.shape
    S = k12.shape[1]
    kernel = functools.partial(_attn_kernel, lambda_init=lambda_init)
    return pl.pallas_call(
        kernel,
        grid=(B, DIFF_HEADS, nqb),
        in_specs=[
            _full((4, DIFF_DH)),
            pl.BlockSpec((1, 1, 2 * DIFF_DH, ATT_BLK), lambda b, h, i: (b, i, h, 0)),
            pl.BlockSpec((1, S, 2 * DIFF_DH), lambda b, h, i: (b, 0, h)),
            pl.BlockSpec((1, nqb, DIFF_DV, ATT_BLK), lambda b, h, i: (b, 0, h, 0)),
            pl.BlockSpec((1, 2, ATT_BLK, ATT_BLK), lambda b, h, i: (h, 0, 0, 0)),
            _full((DIFF_DV, 1)),
        ],
        out_specs=pl.BlockSpec((1, ATT_BLK, DIFF_DV), lambda b, h, i: (b, i, h)),
        out_shape=jax.ShapeDtypeStruct((B, S, DIFF_HEADS * DIFF_DV), BF16),
        scratch_shapes=[pltpu.VMEM((2 * DIFF_DH, 2 * ATT_BLK), BF16),
                        pltpu.VMEM((ATT_BLK, 2 * ATT_BLK), F32),
                        pltpu.VMEM((ATT_BLK, 2 * ATT_BLK), F32),
                        pltpu.VMEM((8, 2 * ATT_BLK), F32),
                        pltpu.VMEM((8, 2 * ATT_BLK), F32),
                        pltpu.VMEM((2, DIFF_DV, ATT_BLK), F32),
                        pltpu.VMEM((8, 2 * ATT_BLK), F32)],
        compiler_params=pltpu.CompilerParams(dimension_semantics=("parallel", "parallel", "arbitrary"),
                                             vmem_limit_bytes=VMEM_LIMIT),
        name="diff_attn",
    )(lam4, qt, k12, vt, bias, g_sub.reshape(-1, 1))


def _proj_ln_kernel(h_ref, a_ref, w_ref, lng_ref, lnb_ref, o_ref):
    mix = jnp.dot(a_ref[...], w_ref[...], preferred_element_type=F32)
    o_ref[...] = _layer_norm(DEEPNORM_ALPHA * h_ref[...] + mix, lng_ref[...], lnb_ref[...])


def _proj_ln(h, a, w_out, ln_g, ln_b, *, tb=512):
    T, D = h.shape
    return pl.pallas_call(
        _proj_ln_kernel,
        grid=(T // tb,),
        in_specs=[pl.BlockSpec((tb, D), lambda i: (i, 0)), pl.BlockSpec((tb, a.shape[1]), lambda i: (i, 0)),
                  _full(w_out.shape), _full((1, D)), _full((1, D))],
        out_specs=pl.BlockSpec((tb, D), lambda i: (i, 0)),
        out_shape=jax.ShapeDtypeStruct((T, D), F32),
        compiler_params=pltpu.CompilerParams(dimension_semantics=("parallel",), vmem_limit_bytes=VMEM_LIMIT),
        name="attn_out_ln",
    )(h, a, w_out.astype(BF16), ln_g.reshape(1, -1), ln_b.reshape(1, -1))


def kernel(x, a_w_in, a_w_gate2, a_b_gate, a_g_norm, a_w_out, kv_w, b_w_q, b_lam_q1, b_lam_k1, b_lam_q2, b_lam_k2, b_g_sub, b_w_out, rel_table, moe_w_group, moe_b_group, moe_w_router, moe_b_router, moe_w_gate, moe_w_up, moe_w_down, ln_g, ln_b):
    B, S, D = x.shape
    h = x
    bias = None
    k12 = vt = None
    for layer in range(DEPTH):
        if layer < N_A_LAYERS:
            h = _gla_layer(h, a_w_in[layer], a_w_gate2[layer], a_b_gate[layer], a_g_norm[layer],
                           a_w_out[layer], ln_g[layer, 0], ln_b[layer, 0])
        else:
            j = layer - N_A_LAYERS
            lambda_init = 0.8 - 0.6 * math.exp(-0.3 * layer)
            if bias is None:
                bias = _bias_tiles(rel_table)
            qt = _q_proj(h, b_w_q[j])
            lam4 = jnp.stack([b_lam_q1[j], b_lam_k1[j], b_lam_q2[j], b_lam_k2[j]])
            a = _diff_attention(qt, k12, vt, bias, lam4, b_g_sub[j], lambda_init)
            h = _proj_ln(h.reshape(B * S, D), a.reshape(B * S, -1), b_w_out[j], ln_g[layer, 0],
                         ln_b[layer, 0]).reshape(B, S, D)
        h = _moe_layer(h.reshape(B * S, D), moe_w_group[layer], moe_b_group[layer], moe_w_router[layer],
                       moe_b_router[layer], moe_w_gate[layer], moe_w_up[layer], moe_w_down[layer],
                       ln_g[layer, 1], ln_b[layer, 1]).reshape(B, S, D)
        if layer == N_A_LAYERS - 1:
            k12, vt = _kv_proj(h, kv_w)
    return h
```

```python
import functools
import math

import jax
import jax.numpy as jnp
from jax import lax
from jax.experimental import pallas as pl
from jax.experimental.pallas import tpu as pltpu

F32 = jnp.float32
BF16 = jnp.bfloat16

D_MODEL = 1024
DEPTH = 4
CHUNK = 64
N_A_LAYERS = 2

GLA_HEADS = 4
GLA_DK = 128
GLA_DV = 256
GLA_GATE_RANK = 16
GLA_GATE_TAU = 16.0
GLA_HK = GLA_HEADS * GLA_DK
GLA_HV = GLA_HEADS * GLA_DV

DIFF_HEADS = 8
DIFF_DH = 64
DIFF_DV = 128
DIFF_QK = DIFF_HEADS * DIFF_DH

REL_BUCKETS = 32

MOE_GROUPS = 4
MOE_EPG = 4
MOE_EXPERTS = 16
MOE_FF = 512

DEEPNORM_ALPHA = (2.0 * DEPTH) ** 0.25
LN_EPS = 1e-5
NEG_INF = -1e30

LANES = 128
ATT_BLK = 512
ATT_HPS = 2
LOG2E = math.log2(math.e)
VMEM_LIMIT = 48 * 1024 * 1024
MOE_VMEM_LIMIT = 58 * 1024 * 1024
MOE_SUB = 128
MOE_EPS = 2

_HIGHEST = lax.Precision.HIGHEST


def _layer_norm(y, g, b):
    mu = jnp.mean(y, -1, keepdims=True)
    d = y - mu
    var = jnp.mean(d * d, -1, keepdims=True)
    return d * lax.rsqrt(var + LN_EPS) * g + b


def _sigmoid(x):
    return 1.0 / (1.0 + jnp.exp(-x))


def _full(shape):
    return pl.BlockSpec(shape, lambda *_: (0,) * len(shape))


def _gla_kernel(x_ref, w_main_ref, w_kt_ref, w_lr_ref, w_g2_ref, b_g_ref, gn_ref, w_out_ref, lng_ref, lnb_ref,
                tri_ref, o_ref, st_ref, mix_ref):
    nb, tb, d = x_ref.shape
    nt = (((1,), (1,)), ((), ()))

    @pl.when(pl.program_id(0) == 0)
    def _():
        st_ref[...] = jnp.zeros_like(st_ref)

    x = x_ref[...].reshape(nb * tb, d)
    xb = x.astype(BF16)
    lrt = lax.dot_general(w_lr_ref[...], xb, nt, preferred_element_type=F32)
    kt = lax.dot_general(w_kt_ref[...], xb, nt, preferred_element_type=F32)
    hq = jnp.dot(xb, w_main_ref[:, :GLA_HK], preferred_element_type=F32)
    glog = jnp.dot(w_g2_ref[...], lrt.astype(BF16), preferred_element_type=F32) + b_g_ref[...]
    hv = jnp.dot(xb, w_main_ref[:, GLA_HK:GLA_HK + GLA_HV], preferred_element_type=F32)
    la = (jnp.minimum(glog, 0.0) - jnp.log(1.0 + jnp.exp(-jnp.abs(glog)))) * (1.0 / GLA_GATE_TAU)
    la_hi = la.astype(BF16)
    la_lo = (la - la_hi.astype(F32)).astype(BF16)
    tri = tri_ref[...]
    cum = [jnp.dot(la_hi[:, b * tb:(b + 1) * tb], tri, preferred_element_type=F32)
           + jnp.dot(la_lo[:, b * tb:(b + 1) * tb], tri, preferred_element_type=F32) for b in range(nb)]
    r = jnp.dot(xb, w_main_ref[:, GLA_HK + GLA_HV:], preferred_element_type=F32)

    lane = lax.broadcasted_iota(jnp.int32, (GLA_HK, 2 * CHUNK), 1)
    halves = (lane < CHUNK, lane >= CHUNK)
    heads = [(slice(h * GLA_DK, (h + 1) * GLA_DK), slice(h * GLA_DV, (h + 1) * GLA_DV)) for h in range(GLA_HEADS)]
    for pair in range(tb // (2 * CHUNK)):
        c0 = pair * 2 * CHUNK
        dec, upd = {}, {}
        for b in range(nb):
            t0 = b * tb + c0
            cum_p = cum[b][:, c0:c0 + 2 * CHUNK]
            last = (cum_p[:, CHUNK - 1:CHUNK], cum_p[:, 2 * CHUNK - 1:2 * CHUNK])
            kdec = kt[:, t0:t0 + 2 * CHUNK] * jnp.exp(jnp.where(halves[0], last[0], last[1]) - cum_p)
            v_pair = hv[t0:t0 + 2 * CHUNK].astype(BF16)
            for half in range(2):
                kd = jnp.where(halves[half], kdec, 0.0).astype(BF16)
                dec[b, half] = jnp.exp(last[half])
                for h, (ks, vs) in enumerate(heads):
                    upd[b, half, h] = jnp.dot(kd[ks, :], v_pair[:, vs], preferred_element_type=F32)
        for half in range(2):
            for b in range(nb):
                r0 = b * tb + c0 + half * CHUNK
                qc = (hq[r0:r0 + CHUNK] * (GLA_DK ** -0.5)).astype(BF16)
                for h, (ks, vs) in enumerate(heads):
                    st = st_ref[b, h] * dec[b, half][ks, :] + upd[b, half, h]
                    st_ref[b, h] = st
                    o = jnp.dot(qc[:, ks], st.astype(BF16), preferred_element_type=F32)
                    o = o * lax.rsqrt(jnp.mean(o * o, -1, keepdims=True) + LN_EPS) * gn_ref[:, vs]
                    mix_ref[r0:r0 + CHUNK, vs] = o

    gated = mix_ref[...] * (r * _sigmoid(r))
    mix = jnp.dot(gated.astype(BF16), w_out_ref[...], preferred_element_type=F32)
    o_ref[...] = _layer_norm(DEEPNORM_ALPHA * x + mix, lng_ref[...], lnb_ref[...]).reshape(nb, tb, d)


def _gla_layer(h, w_in, w_gate2, b_gate, g_norm, w_out, ln_g, ln_b, *, tb=256):
    B, S, D = h.shape
    n_k = 2 * GLA_HK
    n_main = GLA_HK + 2 * GLA_HV
    n_all = 2 * GLA_HK + 2 * GLA_HV
    w_main = jnp.concatenate([w_in[:, :GLA_HK], w_in[:, n_k:n_all]], axis=1).astype(BF16)
    w_kt = w_in[:, GLA_HK:n_k].T.astype(BF16)
    w_lr = jnp.zeros((LANES, D), F32).at[:GLA_GATE_RANK].set(w_in[:, n_all:].T).astype(BF16)
    w_g2 = jnp.zeros((GLA_HK, LANES), F32).at[:, :GLA_GATE_RANK].set(w_gate2.T).astype(BF16)
    idx = jnp.arange(tb)
    tri = ((idx[:, None] <= idx[None, :]) & (idx[:, None] // CHUNK == idx[None, :] // CHUNK)).astype(BF16)
    return pl.pallas_call(
        _gla_kernel,
        grid=(S // tb,),
        in_specs=[
            pl.BlockSpec((B, tb, D), lambda j: (0, j, 0)),
            _full((D, n_main)), _full((GLA_HK, D)), _full((LANES, D)), _full((GLA_HK, LANES)),
            _full((GLA_HK, 1)), _full((1, GLA_HV)), _full((GLA_HV, D)), _full((1, D)), _full((1, D)),
            _full((tb, tb)),
        ],
        out_specs=pl.BlockSpec((B, tb, D), lambda j: (0, j, 0)),
        out_shape=jax.ShapeDtypeStruct((B, S, D), F32),
        scratch_shapes=[pltpu.VMEM((B, GLA_HEADS, GLA_DK, GLA_DV), F32), pltpu.VMEM((B * tb, GLA_HV), F32)],
        compiler_params=pltpu.CompilerParams(dimension_semantics=("arbitrary",),
                                             vmem_limit_bytes=VMEM_LIMIT),
        name="gla_layer",
    )(h, w_main, w_kt, w_lr, w_g2, b_gate.reshape(-1, 1), g_norm.reshape(1, -1), w_out.astype(BF16),
      ln_g.reshape(1, -1), ln_b.reshape(1, -1), tri)


def _route(logits):
    lane = lax.broadcasted_iota(jnp.int32, logits.shape, 1)
    big = jnp.int32(1 << 20)
    gl = jnp.where(lane < MOE_GROUPS, logits, -jnp.inf)
    gmax = jnp.max(gl, -1, keepdims=True)
    gidx = jnp.min(jnp.where(gl == gmax, lane, big), -1, keepdims=True)
    gw = 1.0 / jnp.sum(jnp.exp(gl - gmax), -1, keepdims=True)
    lo = MOE_GROUPS + gidx * MOE_EPG
    emask = (lane >= lo) & (lane < lo + MOE_EPG)
    el = jnp.where(emask, logits, -jnp.inf)
    pe = jnp.exp(el - jnp.max(el, -1, keepdims=True))
    prob = pe / jnp.sum(pe, -1, keepdims=True)
    pm = jnp.where(emask, prob, -1.0)
    p1 = jnp.max(pm, -1, keepdims=True)
    i1 = jnp.min(jnp.where(pm == p1, lane, big), -1, keepdims=True)
    pm2 = jnp.where(lane == i1, -1.0, pm)
    p2 = jnp.max(pm2, -1, keepdims=True)
    i2 = jnp.min(jnp.where(pm2 == p2, lane, big), -1, keepdims=True)
    den = p1 + p2
    comb = jnp.where(lane == i1, p1 / den, 0.0) + jnp.where(lane == i2, p2 / den, 0.0)
    return comb * gw, gidx


def _split3(v):
    hi = v.astype(BF16).astype(F32)
    mid = (v - hi).astype(BF16).astype(F32)
    lo = (v - hi - mid).astype(BF16).astype(F32)
    return hi, mid, lo


def _moe_kernel(x_ref, wr_ref, br_ref, wg_ref, wu_ref, wd_ref, lng_ref, lnb_ref, o_ref,
                xs_ref, cw_ref, pt_ref, y_ref, meta_ref):
    step = pl.program_id(1)
    steps_per_group = MOE_EPG // MOE_EPS
    grp = step // steps_per_group
    j0 = (step % steps_per_group) * MOE_EPS
    tb = x_ref.shape[0]
    tbs = xs_ref.shape[0]
    d = x_ref.shape[1]

    @pl.when(step == 0)
    def _():
        x = x_ref[...]
        xh = x.astype(BF16)
        xl = (x - xh.astype(F32)).astype(BF16)
        both = jnp.dot(jnp.concatenate([xh, xl], axis=1), wr_ref[...], preferred_element_type=F32)
        logits = both[:, :LANES] + both[:, LANES:] + br_ref[...]
        comb, gidx = _route(logits)
        lane = lax.broadcasted_iota(jnp.int32, (tb, LANES), 1)
        lane1 = lax.broadcasted_iota(jnp.int32, (1, LANES), 1)
        onehot = jnp.where(lane == gidx, 1.0, 0.0)
        earlier = jnp.where(lax.broadcasted_iota(jnp.int32, (tb, tb), 0)
                            > lax.broadcasted_iota(jnp.int32, (tb, tb), 1), 1.0, 0.0).astype(BF16)
        rank = jnp.sum(jnp.dot(earlier, onehot.astype(BF16), preferred_element_type=F32) * onehot,
                       -1, keepdims=True)
        cnt = jnp.sum(onehot, 0, keepdims=True)
        ntile = jnp.floor((cnt + (MOE_SUB - 1.0)) * (1.0 / MOE_SUB))
        off = jnp.zeros((1, LANES), F32)
        for gg in range(MOE_GROUPS - 1):
            rows_g = jnp.sum(jnp.where(lane1 == gg, ntile, 0.0), -1, keepdims=True) * MOE_SUB
            off = off + jnp.where(lane1 > gg, rows_g, 0.0)
        for gg in range(MOE_GROUPS):
            meta_ref[gg] = jnp.sum(jnp.where(lane1 == gg, ntile, 0.0)).astype(jnp.int32)
            meta_ref[MOE_GROUPS + gg] = jnp.sum(jnp.where(lane1 == gg, off, 0.0)).astype(jnp.int32)
        pos = jnp.sum(onehot * off, -1, keepdims=True) + rank
        pt_ref[...] = jnp.where(lax.broadcasted_iota(jnp.int32, (tb, tbs), 1) == pos.astype(jnp.int32),
                                1.0, 0.0).astype(BF16)
        pos_row = jnp.broadcast_to(pos, (tb, LANES)).T[0:1, :].astype(jnp.int32)
        perm = jnp.where(lax.broadcasted_iota(jnp.int32, (tbs, tb), 0) == pos_row, 1.0, 0.0).astype(BF16)
        lo = MOE_GROUPS + gidx * MOE_EPG
        cw = jnp.zeros((tb, LANES), F32)
        for jj in range(MOE_EPG):
            parts = _split3(jnp.sum(jnp.where(lane == lo + jj, comb, 0.0), -1, keepdims=True))
            for k, part in enumerate(parts):
                cw = cw + jnp.where(lane == jj + k * MOE_EPG, part, 0.0)
        sorted_all = jnp.dot(perm, jnp.concatenate([xh, cw.astype(BF16)], axis=1),
                             preferred_element_type=F32)
        xs_ref[...] = sorted_all[:, :d].astype(BF16)
        cw_ref[...] = sorted_all[:, d:]
        y_ref[...] = jnp.zeros_like(y_ref)

    base = meta_ref[MOE_GROUPS + grp]

    def tile(i, carry):
        rows = pl.ds(pl.multiple_of(base + i * MOE_SUB, MOE_SUB), MOE_SUB)
        xt = xs_ref[rows, :]
        cwt = cw_ref[rows, :]
        lane = lax.broadcasted_iota(jnp.int32, (MOE_SUB, LANES), 1)
        hidden = []
        for k in range(MOE_EPS):
            gt = jnp.dot(xt, wg_ref[k], preferred_element_type=F32)
            ut = jnp.dot(xt, wu_ref[k], preferred_element_type=F32)
            hidden.append(((gt * _sigmoid(gt)) * ut).astype(BF16))
        total = None
        for k in range(MOE_EPS):
            y = jnp.dot(hidden[k], wd_ref[k], preferred_element_type=F32)
            mine = (lane % MOE_EPG == j0 + k) & (lane < 3 * MOE_EPG)
            c = jnp.sum(jnp.where(mine, cwt, 0.0), -1, keepdims=True)
            total = c * y if total is None else total + c * y
        y_ref[rows, :] += total
        return carry

    lax.fori_loop(0, meta_ref[grp], tile, 0)

    @pl.when(step == MOE_EXPERTS // MOE_EPS - 1)
    def _():
        ffn = jnp.dot(pt_ref[...], y_ref[...].astype(BF16), preferred_element_type=F32)
        o_ref[...] = _layer_norm(DEEPNORM_ALPHA * x_ref[...] + ffn, lng_ref[...], lnb_ref[...])


def _moe_layer(h, w_group, b_group, w_router, b_router, w_gate, w_up, w_down, ln_g, ln_b, *, tb=1024):
    T, D = h.shape
    wr = jnp.zeros((D, LANES), F32).at[:, :MOE_GROUPS].set(w_group)
    wr = wr.at[:, MOE_GROUPS:MOE_GROUPS + MOE_EXPERTS].set(w_router)
    wrh = wr.astype(BF16)
    wrl = (wr - wrh.astype(F32)).astype(BF16)
    wr2 = jnp.concatenate([jnp.concatenate([wrh, wrl], axis=1),
                           jnp.concatenate([wrh, jnp.zeros_like(wrl)], axis=1)], axis=0)
    br = jnp.zeros((1, LANES), F32).at[0, :MOE_GROUPS].set(b_group)
    br = br.at[0, MOE_GROUPS:MOE_GROUPS + MOE_EXPERTS].set(b_router.reshape(-1))
    tbs = tb + MOE_GROUPS * MOE_SUB
    return pl.pallas_call(
        _moe_kernel,
        grid=(T // tb, MOE_EXPERTS // MOE_EPS),
        in_specs=[
            pl.BlockSpec((tb, D), lambda i, e: (i, 0)),
            _full((2 * D, 2 * LANES)), _full((1, LANES)),
            pl.BlockSpec((MOE_EPS, D, MOE_FF), lambda i, e: (e, 0, 0)),
            pl.BlockSpec((MOE_EPS, D, MOE_FF), lambda i, e: (e, 0, 0)),
            pl.BlockSpec((MOE_EPS, MOE_FF, D), lambda i, e: (e, 0, 0)),
            _full((1, D)), _full((1, D)),
        ],
        out_specs=pl.BlockSpec((tb, D), lambda i, e: (i, 0)),
        out_shape=jax.ShapeDtypeStruct((T, D), F32),
        scratch_shapes=[pltpu.VMEM((tbs, D), BF16), pltpu.VMEM((tbs, LANES), F32), pltpu.VMEM((tb, tbs), BF16),
                        pltpu.VMEM((tbs, D), F32), pltpu.SMEM((2 * MOE_GROUPS,), jnp.int32)],
        compiler_params=pltpu.CompilerParams(dimension_semantics=("parallel", "arbitrary"),
                                             vmem_limit_bytes=MOE_VMEM_LIMIT),
        name="moe_layer",
    )(h, wr2, br, w_gate.astype(BF16), w_up.astype(BF16), w_down.astype(BF16),
      ln_g.reshape(1, -1), ln_b.reshape(1, -1))


def _kv_kernel(x_ref, wk_ref, wvt_ref, k_ref, vt_ref):
    xb = x_ref[0].astype(BF16)
    k_ref[0] = jnp.dot(xb, wk_ref[...], preferred_element_type=F32).astype(BF16)
    vt = lax.dot_general(wvt_ref[...], xb, (((1,), (1,)), ((), ())), preferred_element_type=F32)
    for j in range(vt_ref.shape[1]):
        vt_ref[0, j] = vt[:, j * ATT_BLK:(j + 1) * ATT_BLK].astype(BF16)


def _kv_proj(h, kv_w, *, ts=512):
    B, S, D = h.shape
    k1 = kv_w[:, :DIFF_QK].reshape(D, DIFF_HEADS, DIFF_DH)
    k2 = kv_w[:, DIFF_QK:2 * DIFF_QK].reshape(D, DIFF_HEADS, DIFF_DH)
    wk = jnp.stack([k1, k2], axis=2).reshape(D, 2 * DIFF_QK).astype(BF16)
    wvt = kv_w[:, 2 * DIFF_QK:].T.astype(BF16)
    nv = DIFF_HEADS * DIFF_DV
    nj = ts // ATT_BLK
    return pl.pallas_call(
        _kv_kernel,
        grid=(B, S // ts),
        in_specs=[pl.BlockSpec((1, ts, D), lambda b, j: (b, j, 0)), _full((D, 2 * DIFF_QK)), _full((nv, D))],
        out_specs=[pl.BlockSpec((1, ts, 2 * DIFF_QK), lambda b, j: (b, j, 0)),
                   pl.BlockSpec((1, nj, nv, ATT_BLK), lambda b, j: (b, j, 0, 0))],
        out_shape=[jax.ShapeDtypeStruct((B, S, 2 * DIFF_QK), BF16),
                   jax.ShapeDtypeStruct((B, S // ATT_BLK, nv, ATT_BLK), BF16)],
        compiler_params=pltpu.CompilerParams(dimension_semantics=("parallel", "parallel"),
                                             vmem_limit_bytes=VMEM_LIMIT),
        name="kv_proj",
    )(h, wk, wvt)


def _q_kernel(x_ref, wqt_ref, qt_ref):
    xb = x_ref[0].astype(BF16)
    qt = lax.dot_general(wqt_ref[...], xb, (((1,), (1,)), ((), ())), preferred_element_type=F32)
    qt = qt * (DIFF_DH ** -0.5 * LOG2E)
    for j in range(qt_ref.shape[1]):
        qt_ref[0, j] = qt[:, j * ATT_BLK:(j + 1) * ATT_BLK].astype(BF16)


def _q_proj(h, w_q, *, ts=512):
    B, S, D = h.shape
    q1 = w_q[:, :DIFF_QK].reshape(D, DIFF_HEADS, DIFF_DH)
    q2 = w_q[:, DIFF_QK:].reshape(D, DIFF_HEADS, DIFF_DH)
    wqt = jnp.stack([q1, q2], axis=2).reshape(D, 2 * DIFF_QK).T.astype(BF16)
    nq = 2 * DIFF_QK
    nj = ts // ATT_BLK
    return pl.pallas_call(
        _q_kernel,
        grid=(B, S // ts),
        in_specs=[pl.BlockSpec((1, ts, D), lambda b, j: (b, j, 0)), _full((nq, D))],
        out_specs=pl.BlockSpec((1, nj, nq, ATT_BLK), lambda b, j: (b, j, 0, 0)),
        out_shape=jax.ShapeDtypeStruct((B, S // ATT_BLK, nq, ATT_BLK), BF16),
        compiler_params=pltpu.CompilerParams(dimension_semantics=("parallel", "parallel"),
                                             vmem_limit_bytes=VMEM_LIMIT),
        name="q_proj",
    )(h, wqt)


def _bias_kernel(tab_ref, o_ref):
    h = pl.program_id(0)
    kk = lax.broadcasted_iota(jnp.int32, (ATT_BLK, ATT_BLK), 0)
    qq = lax.broadcasted_iota(jnp.int32, (ATT_BLK, ATT_BLK), 1)
    nb = REL_BUCKETS // 2
    far = tab_ref[nb - 1, h]
    for t in range(2):
        rel = kk - qq - t * ATT_BLK
        n = jnp.abs(rel)
        large = nb // 2
        for thr in (12, 16, 23, 32, 46, 64, 91):
            large = large + (n >= thr).astype(jnp.int32)
        bucket = jnp.where(rel > 0, nb, 0) + jnp.where(n < nb // 2, n, large)
        bias = jnp.zeros((ATT_BLK, ATT_BLK), F32)
        for b in range(REL_BUCKETS):
            bias = jnp.where(bucket == b, tab_ref[b, h], bias)
        bias = (bias - far) * LOG2E
        if t == 0:
            visible = (kk // CHUNK) <= (qq // CHUNK)
            bias = jnp.where(visible, bias, NEG_INF)
        o_ref[0, t] = bias


def _bias_tiles(rel_table):
    return pl.pallas_call(
        _bias_kernel,
        grid=(DIFF_HEADS,),
        in_specs=[pl.BlockSpec(memory_space=pltpu.SMEM)],
        out_specs=pl.BlockSpec((1, 2, ATT_BLK, ATT_BLK), lambda h: (h, 0, 0, 0)),
        out_shape=jax.ShapeDtypeStruct((DIFF_HEADS, 2, ATT_BLK, ATT_BLK), F32),
        name="bias_tiles",
    )(rel_table)


def _attn_kernel(lam_ref, qt_ref, k_ref, vt_ref, bias_ref, g_ref, o_ref, *scratch, lambda_init):
    blk = ATT_BLK
    qi = pl.program_id(2)
    n_far = jnp.maximum(qi - 1, 0)
    prev_block = jnp.maximum(qi - 1, 0)
    heads = [scratch[hh * 7:(hh + 1) * 7] for hh in range(ATT_HPS)]
    z = jnp.zeros((DIFF_DH, blk), BF16)
    for hh, (w_ref, _, _, _, _, acc_ref, ml_ref) in enumerate(heads):
        qt = qt_ref[0, 0, hh * 2 * DIFF_DH:(hh + 1) * 2 * DIFF_DH, :]
        w_ref[...] = jnp.concatenate(
            [jnp.concatenate([qt[:DIFF_DH], z], 0), jnp.concatenate([z, qt[DIFF_DH:]], 0)], 1)
        ml_ref[0:1, :] = jnp.full((1, 2 * blk), NEG_INF, F32)
        ml_ref[1:2, :] = jnp.zeros((1, 2 * blk), F32)
        acc_ref[...] = jnp.zeros_like(acc_ref)

    def scores(kb, slot, bias_idx=None, extra=None):
        for hh, (w_ref, s0_ref, s1_ref, t0_ref, t1_ref, _, _) in enumerate(heads):
            s_ref, t_ref = ((s0_ref, t0_ref), (s1_ref, t1_ref))[slot]
            kt = k_ref[0, pl.ds(pl.multiple_of(kb * blk, blk), blk), hh * LANES:(hh + 1) * LANES]
            s = jnp.dot(kt, w_ref[...], preferred_element_type=F32)
            if bias_idx is not None:
                bias = bias_ref[hh, bias_idx]
                s = s + jnp.concatenate([bias, bias], axis=1)
            if extra is not None:
                s = s + extra
            s_ref[...] = s
            t_ref[0:1, :] = jnp.max(s, 0, keepdims=True)

    def accumulate(kb, slot):
        for hh, (_, s0_ref, s1_ref, t0_ref, t1_ref, acc_ref, ml_ref) in enumerate(heads):
            s_ref, t_ref = ((s0_ref, t0_ref), (s1_ref, t1_ref))[slot]
            m_old = ml_ref[0:1, :]
            m_new = jnp.maximum(m_old, t_ref[0:1, :])
            a = jnp.exp2(m_old - m_new)
            p = jnp.exp2(s_ref[...] - m_new)
            ml_ref[1:2, :] = a * ml_ref[1:2, :] + jnp.sum(p, 0, keepdims=True)
            ml_ref[0:1, :] = m_new
            pb = p.astype(BF16)
            vt = vt_ref[0, kb, hh * DIFF_DV:(hh + 1) * DIFF_DV, :]
            for i in range(2):
                cols = slice(i * blk, (i + 1) * blk)
                acc_ref[i] = a[:, cols] * acc_ref[i] + jnp.dot(vt, pb[:, cols], preferred_element_type=F32)

    scores(qi, 0, bias_idx=0)
    scores(prev_block, 1, bias_idx=1, extra=jnp.where(qi >= 1, 0.0, -jnp.inf).astype(F32))
    accumulate(qi, 0)

    def pair(j, carry):
        kb = 2 * j
        scores(kb, 0)
        accumulate(jnp.where(j == 0, prev_block, kb - 1), 1)
        scores(kb + 1, 1)
        accumulate(kb, 0)
        return carry

    n_pairs = n_far // 2
    lax.fori_loop(0, n_pairs, pair, 0)
    last_slot1 = jnp.where(n_pairs == 0, prev_block, 2 * n_pairs - 1)

    @pl.when(n_far % 2 == 1)
    def _():
        scores(n_far - 1, 0)
        accumulate(last_slot1, 1)
        accumulate(n_far - 1, 0)

    @pl.when(n_far % 2 == 0)
    def _():
        accumulate(last_slot1, 1)

    lam = (jnp.exp(jnp.sum(lam_ref[0:1, :] * lam_ref[1:2, :], keepdims=True))
           - jnp.exp(jnp.sum(lam_ref[2:3, :] * lam_ref[3:4, :], keepdims=True)) + lambda_init)
    for hh, (_, _, _, _, _, acc_ref, ml_ref) in enumerate(heads):
        l = ml_ref[1:2, :]
        ot = acc_ref[0] / l[:, :blk] - lam * (acc_ref[1] / l[:, blk:])
        ot = ot * lax.rsqrt(jnp.mean(ot * ot, 0, keepdims=True) + LN_EPS) * g_ref[...] * (1.0 - lambda_init)
        o_ref[0, :, hh * DIFF_DV:(hh + 1) * DIFF_DV] = ot.T.astype(BF16)


def _diff_attention(qt, k12, vt, bias, lam4, g_sub, lambda_init):
    B, nqb, _, _ = qt.shape
    S = k12.shape[1]
    kernel = functools.partial(_attn_kernel, lambda_init=lambda_init)
    once = pl.Buffered(1)
    head_scratch = [pltpu.VMEM((2 * DIFF_DH, 2 * ATT_BLK), BF16),
                    pltpu.VMEM((ATT_BLK, 2 * ATT_BLK), F32), pltpu.VMEM((ATT_BLK, 2 * ATT_BLK), F32),
                    pltpu.VMEM((8, 2 * ATT_BLK), F32), pltpu.VMEM((8, 2 * ATT_BLK), F32),
                    pltpu.VMEM((2, DIFF_DV, ATT_BLK), F32), pltpu.VMEM((8, 2 * ATT_BLK), F32)]
    return pl.pallas_call(
        kernel,
        grid=(B, DIFF_HEADS // ATT_HPS, nqb),
        in_specs=[
            _full((4, DIFF_DH)),
            pl.BlockSpec((1, 1, ATT_HPS * 2 * DIFF_DH, ATT_BLK), lambda b, h, i: (b, i, h, 0)),
            pl.BlockSpec((1, S, ATT_HPS * 2 * DIFF_DH), lambda b, h, i: (b, 0, h), pipeline_mode=once),
            pl.BlockSpec((1, nqb, ATT_HPS * DIFF_DV, ATT_BLK), lambda b, h, i: (b, 0, h, 0),
                         pipeline_mode=once),
            pl.BlockSpec((ATT_HPS, 2, ATT_BLK, ATT_BLK), lambda b, h, i: (h, 0, 0, 0), pipeline_mode=once),
            _full((DIFF_DV, 1)),
        ],
        out_specs=pl.BlockSpec((1, ATT_BLK, ATT_HPS * DIFF_DV), lambda b, h, i: (b, i, h)),
        out_shape=jax.ShapeDtypeStruct((B, S, DIFF_HEADS * DIFF_DV), BF16),
        scratch_shapes=head_scratch * ATT_HPS,
        compiler_params=pltpu.CompilerParams(dimension_semantics=("parallel", "parallel", "arbitrary"),
                                             vmem_limit_bytes=VMEM_LIMIT),
        name="diff_attn",
    )(lam4, qt, k12, vt, bias, g_sub.reshape(-1, 1))


def _proj_ln_kernel(h_ref, a_ref, w_ref, lng_ref, lnb_ref, o_ref):
    mix = jnp.dot(a_ref[...], w_ref[...], preferred_element_type=F32)
    o_ref[...] = _layer_norm(DEEPNORM_ALPHA * h_ref[...] + mix, lng_ref[...], lnb_ref[...])


def _proj_ln(h, a, w_out, ln_g, ln_b, *, tb=512):
    T, D = h.shape
    return pl.pallas_call(
        _proj_ln_kernel,
        grid=(T // tb,),
        in_specs=[pl.BlockSpec((tb, D), lambda i: (i, 0)), pl.BlockSpec((tb, a.shape[1]), lambda i: (i, 0)),
                  _full(w_out.shape), _full((1, D)), _full((1, D))],
        out_specs=pl.BlockSpec((tb, D), lambda i: (i, 0)),
        out_shape=jax.ShapeDtypeStruct((T, D), F32),
        compiler_params=pltpu.CompilerParams(dimension_semantics=("parallel",), vmem_limit_bytes=VMEM_LIMIT),
        name="attn_out_ln",
    )(h, a, w_out.astype(BF16), ln_g.reshape(1, -1), ln_b.reshape(1, -1))


def kernel(x, a_w_in, a_w_gate2, a_b_gate, a_g_norm, a_w_out, kv_w, b_w_q, b_lam_q1, b_lam_k1, b_lam_q2, b_lam_k2, b_g_sub, b_w_out, rel_table, moe_w_group, moe_b_group, moe_w_router, moe_b_router, moe_w_gate, moe_w_up, moe_w_down, ln_g, ln_b):
    B, S, D = x.shape
    h = x
    bias = None
    k12 = vt = None
    for layer in range(DEPTH):
        if layer < N_A_LAYERS:
            h = _gla_layer(h, a_w_in[layer], a_w_gate2[layer], a_b_gate[layer], a_g_norm[layer],
                           a_w_out[layer], ln_g[layer, 0], ln_b[layer, 0])
        else:
            j = layer - N_A_LAYERS
            lambda_init = 0.8 - 0.6 * math.exp(-0.3 * layer)
            if bias is None:
                bias = _bias_tiles(rel_table)
            qt = _q_proj(h, b_w_q[j])
            lam4 = jnp.stack([b_lam_q1[j], b_lam_k1[j], b_lam_q2[j], b_lam_k2[j]])
            a = _diff_attention(qt, k12, vt, bias, lam4, b_g_sub[j], lambda_init)
            h = _proj_ln(h.reshape(B * S, D), a.reshape(B * S, -1), b_w_out[j], ln_g[layer, 0],
                         ln_b[layer, 0]).reshape(B, S, D)
        h = _moe_layer(h.reshape(B * S, D), moe_w_group[layer], moe_b_group[layer], moe_w_router[layer],
                       moe_b_router[layer], moe_w_gate[layer], moe_w_up[layer], moe_w_down[layer],
                       ln_g[layer, 1], ln_b[layer, 1]).reshape(B, S, D)
        if layer == N_A_LAYERS - 1:
            k12, vt = _kv_proj(h, kv_w)
    return h
```

```python
import functools
import math

import jax
import jax.numpy as jnp
from jax import lax
from jax.experimental import pallas as pl
from jax.experimental.pallas import tpu as pltpu

F32 = jnp.float32
BF16 = jnp.bfloat16

D_MODEL = 1024
DEPTH = 4
CHUNK = 64
N_A_LAYERS = 2

GLA_HEADS = 4
GLA_DK = 128
GLA_DV = 256
GLA_GATE_RANK = 16
GLA_GATE_TAU = 16.0
GLA_HK = GLA_HEADS * GLA_DK
GLA_HV = GLA_HEADS * GLA_DV

DIFF_HEADS = 8
DIFF_DH = 64
DIFF_DV = 128
DIFF_QK = DIFF_HEADS * DIFF_DH

REL_BUCKETS = 32

MOE_GROUPS = 4
MOE_EPG = 4
MOE_EXPERTS = 16
MOE_FF = 512

DEEPNORM_ALPHA = (2.0 * DEPTH) ** 0.25
LN_EPS = 1e-5
NEG_INF = -1e30

LANES = 128
ATT_BLK = 512
ATT_HPS = 2
LOG2E = math.log2(math.e)
VMEM_LIMIT = 48 * 1024 * 1024
MOE_VMEM_LIMIT = 58 * 1024 * 1024
MOE_SUB = 128
MOE_EPS = 2

_HIGHEST = lax.Precision.HIGHEST


def _layer_norm(y, g, b):
    mu = jnp.mean(y, -1, keepdims=True)
    d = y - mu
    var = jnp.mean(d * d, -1, keepdims=True)
    return d * lax.rsqrt(var + LN_EPS) * g + b


def _sigmoid(x):
    return 1.0 / (1.0 + jnp.exp(-x))


def _full(shape):
    return pl.BlockSpec(shape, lambda *_: (0,) * len(shape))


def _gla_kernel(x_ref, w_main_ref, w_kt_ref, w_lr_ref, w_g2_ref, b_g_ref, gn_ref, w_out_ref, lng_ref, lnb_ref,
                tri_ref, o_ref, st_ref, mix_ref):
    nb, tb, d = x_ref.shape
    nt = (((1,), (1,)), ((), ()))

    @pl.when(pl.program_id(0) == 0)
    def _():
        st_ref[...] = jnp.zeros_like(st_ref)

    x = x_ref[...].reshape(nb * tb, d)
    xb = x.astype(BF16)
    lrt = lax.dot_general(w_lr_ref[...], xb, nt, preferred_element_type=F32)
    kt = lax.dot_general(w_kt_ref[...], xb, nt, preferred_element_type=F32)
    hq = jnp.dot(xb, w_main_ref[:, :GLA_HK], preferred_element_type=F32)
    glog = jnp.dot(w_g2_ref[...], lrt.astype(BF16), preferred_element_type=F32) + b_g_ref[...]
    hv = jnp.dot(xb, w_main_ref[:, GLA_HK:GLA_HK + GLA_HV], preferred_element_type=F32)
    la = (jnp.minimum(glog, 0.0) - jnp.log(1.0 + jnp.exp(-jnp.abs(glog)))) * (1.0 / GLA_GATE_TAU)
    la_hi = la.astype(BF16)
    la_lo = (la - la_hi.astype(F32)).astype(BF16)
    tri = tri_ref[...]
    cum = [jnp.dot(la_hi[:, b * tb:(b + 1) * tb], tri, preferred_element_type=F32)
           + jnp.dot(la_lo[:, b * tb:(b + 1) * tb], tri, preferred_element_type=F32) for b in range(nb)]
    r = jnp.dot(xb, w_main_ref[:, GLA_HK + GLA_HV:], preferred_element_type=F32)

    lane = lax.broadcasted_iota(jnp.int32, (GLA_HK, 2 * CHUNK), 1)
    halves = (lane < CHUNK, lane >= CHUNK)
    heads = [(slice(h * GLA_DK, (h + 1) * GLA_DK), slice(h * GLA_DV, (h + 1) * GLA_DV)) for h in range(GLA_HEADS)]
    for pair in range(tb // (2 * CHUNK)):
        c0 = pair * 2 * CHUNK
        dec, upd = {}, {}
        for b in range(nb):
            t0 = b * tb + c0
            cum_p = cum[b][:, c0:c0 + 2 * CHUNK]
            last = (cum_p[:, CHUNK - 1:CHUNK], cum_p[:, 2 * CHUNK - 1:2 * CHUNK])
            kdec = kt[:, t0:t0 + 2 * CHUNK] * jnp.exp(jnp.where(halves[0], last[0], last[1]) - cum_p)
            v_pair = hv[t0:t0 + 2 * CHUNK].astype(BF16)
            for half in range(2):
                kd = jnp.where(halves[half], kdec, 0.0).astype(BF16)
                dec[b, half] = jnp.exp(last[half])
                for h, (ks, vs) in enumerate(heads):
                    upd[b, half, h] = jnp.dot(kd[ks, :], v_pair[:, vs], preferred_element_type=F32)
        for half in range(2):
            for b in range(nb):
                r0 = b * tb + c0 + half * CHUNK
                qc = (hq[r0:r0 + CHUNK] * (GLA_DK ** -0.5)).astype(BF16)
                for h, (ks, vs) in enumerate(heads):
                    st = st_ref[b, h] * dec[b, half][ks, :] + upd[b, half, h]
                    st_ref[b, h] = st
                    o = jnp.dot(qc[:, ks], st.astype(BF16), preferred_element_type=F32)
                    o = o * lax.rsqrt(jnp.mean(o * o, -1, keepdims=True) + LN_EPS) * gn_ref[:, vs]
                    mix_ref[r0:r0 + CHUNK, vs] = o

    gated = mix_ref[...] * (r * _sigmoid(r))
    mix = jnp.dot(gated.astype(BF16), w_out_ref[...], preferred_element_type=F32)
    o_ref[...] = _layer_norm(DEEPNORM_ALPHA * x + mix, lng_ref[...], lnb_ref[...]).reshape(nb, tb, d)


def _gla_layer(h, w_in, w_gate2, b_gate, g_norm, w_out, ln_g, ln_b, *, tb=256):
    B, S, D = h.shape
    n_k = 2 * GLA_HK
    n_main = GLA_HK + 2 * GLA_HV
    n_all = 2 * GLA_HK + 2 * GLA_HV
    w_main = jnp.concatenate([w_in[:, :GLA_HK], w_in[:, n_k:n_all]], axis=1).astype(BF16)
    w_kt = w_in[:, GLA_HK:n_k].T.astype(BF16)
    w_lr = jnp.zeros((LANES, D), F32).at[:GLA_GATE_RANK].set(w_in[:, n_all:].T).astype(BF16)
    w_g2 = jnp.zeros((GLA_HK, LANES), F32).at[:, :GLA_GATE_RANK].set(w_gate2.T).astype(BF16)
    idx = jnp.arange(tb)
    tri = ((idx[:, None] <= idx[None, :]) & (idx[:, None] // CHUNK == idx[None, :] // CHUNK)).astype(BF16)
    return pl.pallas_call(
        _gla_kernel,
        grid=(S // tb,),
        in_specs=[
            pl.BlockSpec((B, tb, D), lambda j: (0, j, 0)),
            _full((D, n_main)), _full((GLA_HK, D)), _full((LANES, D)), _full((GLA_HK, LANES)),
            _full((GLA_HK, 1)), _full((1, GLA_HV)), _full((GLA_HV, D)), _full((1, D)), _full((1, D)),
            _full((tb, tb)),
        ],
        out_specs=pl.BlockSpec((B, tb, D), lambda j: (0, j, 0)),
        out_shape=jax.ShapeDtypeStruct((B, S, D), F32),
        scratch_shapes=[pltpu.VMEM((B, GLA_HEADS, GLA_DK, GLA_DV), F32), pltpu.VMEM((B * tb, GLA_HV), F32)],
        compiler_params=pltpu.CompilerParams(dimension_semantics=("arbitrary",),
                                             vmem_limit_bytes=VMEM_LIMIT),
        name="gla_layer",
    )(h, w_main, w_kt, w_lr, w_g2, b_gate.reshape(-1, 1), g_norm.reshape(1, -1), w_out.astype(BF16),
      ln_g.reshape(1, -1), ln_b.reshape(1, -1), tri)


def _route(logits):
    lane = lax.broadcasted_iota(jnp.int32, logits.shape, 1).astype(F32)
    big = float(1 << 20)
    gl = jnp.where(lane < MOE_GROUPS, logits, -jnp.inf)
    gmax = jnp.max(gl, -1, keepdims=True)
    gidx = jnp.min(jnp.where(gl == gmax, lane, big), -1, keepdims=True)
    gw = 1.0 / jnp.sum(jnp.exp(gl - gmax), -1, keepdims=True)
    lo = MOE_GROUPS + gidx * MOE_EPG
    emask = (lane >= lo) & (lane < lo + MOE_EPG)
    el = jnp.where(emask, logits, -jnp.inf)
    pe = jnp.exp(el - jnp.max(el, -1, keepdims=True))
    prob = pe / jnp.sum(pe, -1, keepdims=True)
    pm = jnp.where(emask, prob, -1.0)
    p1 = jnp.max(pm, -1, keepdims=True)
    i1 = jnp.min(jnp.where(pm == p1, lane, big), -1, keepdims=True)
    pm2 = jnp.where(lane == i1, -1.0, pm)
    p2 = jnp.max(pm2, -1, keepdims=True)
    i2 = jnp.min(jnp.where(pm2 == p2, lane, big), -1, keepdims=True)
    den = p1 + p2
    comb = jnp.where(lane == i1, p1 / den, 0.0) + jnp.where(lane == i2, p2 / den, 0.0)
    return comb * gw, gidx.astype(jnp.int32)


def _split3(v):
    hi = v.astype(BF16).astype(F32)
    mid = (v - hi).astype(BF16).astype(F32)
    lo = (v - hi - mid).astype(BF16).astype(F32)
    return hi, mid, lo


def _moe_kernel(x_ref, wr_ref, br_ref, wg_ref, wu_ref, wd_ref, lng_ref, lnb_ref, o_ref,
                xs_ref, cw_ref, pt_ref, y_ref, meta_ref):
    step = pl.program_id(1)
    steps_per_group = MOE_EPG // MOE_EPS
    grp = step // steps_per_group
    j0 = (step % steps_per_group) * MOE_EPS
    tb = x_ref.shape[0]
    tbs = xs_ref.shape[0]
    d = x_ref.shape[1]

    @pl.when(step == 0)
    def _():
        x = x_ref[...]
        xh = x.astype(BF16)
        xl = (x - xh.astype(F32)).astype(BF16)
        x2 = jnp.concatenate([xh, xl], axis=1)
        half = tb // 2
        both = jnp.concatenate([jnp.dot(x2[:half], wr_ref[...], preferred_element_type=F32),
                                jnp.dot(x2[half:], wr_ref[...], preferred_element_type=F32)], axis=0)
        logits = both[:, :LANES] + both[:, LANES:] + br_ref[...]
        comb, gidx = _route(logits)
        lane = lax.broadcasted_iota(jnp.int32, (tb, LANES), 1)
        lane1 = lax.broadcasted_iota(jnp.int32, (1, LANES), 1)
        onehot = jnp.where(lane == gidx, 1.0, 0.0)
        earlier = jnp.where(lax.broadcasted_iota(jnp.int32, (LANES, LANES), 0)
                            > lax.broadcasted_iota(jnp.int32, (LANES, LANES), 1), 1.0, 0.0).astype(BF16)
        onehot_b = onehot.astype(BF16)
        before, carry = [], jnp.zeros((1, LANES), F32)
        for piece in range(tb // LANES):
            oh = onehot_b[piece * LANES:(piece + 1) * LANES]
            before.append(jnp.dot(earlier, oh, preferred_element_type=F32) + carry)
            carry = carry + jnp.sum(onehot[piece * LANES:(piece + 1) * LANES], 0, keepdims=True)
        rank = jnp.sum(jnp.concatenate(before, axis=0) * onehot, -1, keepdims=True)
        cnt = carry
        ntile = jnp.floor((cnt + (MOE_SUB - 1.0)) * (1.0 / MOE_SUB))
        off = jnp.zeros((1, LANES), F32)
        for gg in range(MOE_GROUPS - 1):
            rows_g = jnp.sum(jnp.where(lane1 == gg, ntile, 0.0), -1, keepdims=True) * MOE_SUB
            off = off + jnp.where(lane1 > gg, rows_g, 0.0)
        for gg in range(MOE_GROUPS):
            meta_ref[gg] = jnp.sum(jnp.where(lane1 == gg, ntile, 0.0)).astype(jnp.int32)
            meta_ref[MOE_GROUPS + gg] = jnp.sum(jnp.where(lane1 == gg, off, 0.0)).astype(jnp.int32)
        pos = jnp.sum(onehot * off, -1, keepdims=True) + rank
        pt_ref[...] = jnp.where(lax.broadcasted_iota(jnp.int32, (tb, tbs), 1) == pos.astype(jnp.int32),
                                1.0, 0.0).astype(BF16)
        pos_row = jnp.broadcast_to(pos, (tb, LANES)).T[0:1, :].astype(jnp.int32)
        perm = jnp.where(lax.broadcasted_iota(jnp.int32, (tbs, tb), 0) == pos_row, 1.0, 0.0).astype(BF16)
        lo = MOE_GROUPS + gidx * MOE_EPG
        cw = jnp.zeros((tb, LANES), F32)
        for jj in range(MOE_EPG):
            parts = _split3(jnp.sum(jnp.where(lane == lo + jj, comb, 0.0), -1, keepdims=True))
            for k, part in enumerate(parts):
                cw = cw + jnp.where(lane == jj + k * MOE_EPG, part, 0.0)
        sorted_all = jnp.dot(perm, jnp.concatenate([xh, cw.astype(BF16)], axis=1),
                             preferred_element_type=F32)
        xs_ref[...] = sorted_all[:, :d].astype(BF16)
        cw_ref[...] = sorted_all[:, d:]
        y_ref[...] = jnp.zeros_like(y_ref)

    base = meta_ref[MOE_GROUPS + grp]

    def experts(start, nrows):
        rows = pl.ds(pl.multiple_of(start, MOE_SUB), nrows)
        xt = xs_ref[rows, :]
        cwt = cw_ref[rows, :]
        lane = lax.broadcasted_iota(jnp.int32, (nrows, LANES), 1)
        hidden = []
        for k in range(MOE_EPS):
            gt = jnp.dot(xt, wg_ref[k], preferred_element_type=F32)
            ut = jnp.dot(xt, wu_ref[k], preferred_element_type=F32)
            hidden.append(((gt * _sigmoid(gt)) * ut).astype(BF16))
        total = None
        for k in range(MOE_EPS):
            y = jnp.dot(hidden[k], wd_ref[k], preferred_element_type=F32)
            mine = (lane % MOE_EPG == j0 + k) & (lane < 3 * MOE_EPG)
            c = jnp.sum(jnp.where(mine, cwt, 0.0), -1, keepdims=True)
            total = c * y if total is None else total + c * y
        y_ref[rows, :] += total

    n_tiles = meta_ref[grp]

    def wide(i, carry):
        experts(base + i * (2 * MOE_SUB), 2 * MOE_SUB)
        return carry

    lax.fori_loop(0, n_tiles // 2, wide, 0)

    @pl.when(n_tiles % 2 == 1)
    def _():
        experts(base + (n_tiles - 1) * MOE_SUB, MOE_SUB)

    @pl.when(step == MOE_EXPERTS // MOE_EPS - 1)
    def _():
        yb = y_ref[...].astype(BF16)
        piece = tb // 4
        for c in range(4):
            rows = slice(c * piece, (c + 1) * piece)
            ffn = jnp.dot(pt_ref[rows, :], yb, preferred_element_type=F32)
            o_ref[rows, :] = _layer_norm(DEEPNORM_ALPHA * x_ref[rows, :] + ffn, lng_ref[...], lnb_ref[...])


def _moe_layer(h, w_group, b_group, w_router, b_router, w_gate, w_up, w_down, ln_g, ln_b, *, tb=1024):
    T, D = h.shape
    wr = jnp.zeros((D, LANES), F32).at[:, :MOE_GROUPS].set(w_group)
    wr = wr.at[:, MOE_GROUPS:MOE_GROUPS + MOE_EXPERTS].set(w_router)
    wrh = wr.astype(BF16)
    wrl = (wr - wrh.astype(F32)).astype(BF16)
    wr2 = jnp.concatenate([jnp.concatenate([wrh, wrl], axis=1),
                           jnp.concatenate([wrh, jnp.zeros_like(wrl)], axis=1)], axis=0)
    br = jnp.zeros((1, LANES), F32).at[0, :MOE_GROUPS].set(b_group)
    br = br.at[0, MOE_GROUPS:MOE_GROUPS + MOE_EXPERTS].set(b_router.reshape(-1))
    tbs = tb + MOE_GROUPS * MOE_SUB
    return pl.pallas_call(
        _moe_kernel,
        grid=(T // tb, MOE_EXPERTS // MOE_EPS),
        in_specs=[
            pl.BlockSpec((tb, D), lambda i, e: (i, 0)),
            _full((2 * D, 2 * LANES)), _full((1, LANES)),
            pl.BlockSpec((MOE_EPS, D, MOE_FF), lambda i, e: (e, 0, 0)),
            pl.BlockSpec((MOE_EPS, D, MOE_FF), lambda i, e: (e, 0, 0)),
            pl.BlockSpec((MOE_EPS, MOE_FF, D), lambda i, e: (e, 0, 0)),
            _full((1, D)), _full((1, D)),
        ],
        out_specs=pl.BlockSpec((tb, D), lambda i, e: (i, 0)),
        out_shape=jax.ShapeDtypeStruct((T, D), F32),
        scratch_shapes=[pltpu.VMEM((tbs, D), BF16), pltpu.VMEM((tbs, LANES), F32), pltpu.VMEM((tb, tbs), BF16),
                        pltpu.VMEM((tbs, D), F32), pltpu.SMEM((2 * MOE_GROUPS,), jnp.int32)],
        compiler_params=pltpu.CompilerParams(dimension_semantics=("parallel", "arbitrary"),
                                             vmem_limit_bytes=MOE_VMEM_LIMIT),
        name="moe_layer",
    )(h, wr2, br, w_gate.astype(BF16), w_up.astype(BF16), w_down.astype(BF16),
      ln_g.reshape(1, -1), ln_b.reshape(1, -1))


def _kv_kernel(x_ref, wk_ref, wvt_ref, k_ref, vt_ref):
    xb = x_ref[0].astype(BF16)
    k_ref[0] = jnp.dot(xb, wk_ref[...], preferred_element_type=F32).astype(BF16)
    vt = lax.dot_general(wvt_ref[...], xb, (((1,), (1,)), ((), ())), preferred_element_type=F32)
    for j in range(vt_ref.shape[1]):
        vt_ref[0, j] = vt[:, j * ATT_BLK:(j + 1) * ATT_BLK].astype(BF16)


def _kv_proj(h, kv_w, *, ts=512):
    B, S, D = h.shape
    k1 = kv_w[:, :DIFF_QK].reshape(D, DIFF_HEADS, DIFF_DH)
    k2 = kv_w[:, DIFF_QK:2 * DIFF_QK].reshape(D, DIFF_HEADS, DIFF_DH)
    wk = jnp.stack([k1, k2], axis=2).reshape(D, 2 * DIFF_QK).astype(BF16)
    wvt = kv_w[:, 2 * DIFF_QK:].T.astype(BF16)
    nv = DIFF_HEADS * DIFF_DV
    nj = ts // ATT_BLK
    return pl.pallas_call(
        _kv_kernel,
        grid=(B, S // ts),
        in_specs=[pl.BlockSpec((1, ts, D), lambda b, j: (b, j, 0)), _full((D, 2 * DIFF_QK)), _full((nv, D))],
        out_specs=[pl.BlockSpec((1, ts, 2 * DIFF_QK), lambda b, j: (b, j, 0)),
                   pl.BlockSpec((1, nj, nv, ATT_BLK), lambda b, j: (b, j, 0, 0))],
        out_shape=[jax.ShapeDtypeStruct((B, S, 2 * DIFF_QK), BF16),
                   jax.ShapeDtypeStruct((B, S // ATT_BLK, nv, ATT_BLK), BF16)],
        compiler_params=pltpu.CompilerParams(dimension_semantics=("parallel", "parallel"),
                                             vmem_limit_bytes=VMEM_LIMIT),
        name="kv_proj",
    )(h, wk, wvt)


def _q_kernel(x_ref, wqt_ref, qt_ref):
    xb = x_ref[0].astype(BF16)
    qt = lax.dot_general(wqt_ref[...], xb, (((1,), (1,)), ((), ())), preferred_element_type=F32)
    qt = qt * (DIFF_DH ** -0.5 * LOG2E)
    for j in range(qt_ref.shape[1]):
        qt_ref[0, j] = qt[:, j * ATT_BLK:(j + 1) * ATT_BLK].astype(BF16)


def _q_proj(h, w_q, *, ts=512):
    B, S, D = h.shape
    q1 = w_q[:, :DIFF_QK].reshape(D, DIFF_HEADS, DIFF_DH)
    q2 = w_q[:, DIFF_QK:].reshape(D, DIFF_HEADS, DIFF_DH)
    wqt = jnp.stack([q1, q2], axis=2).reshape(D, 2 * DIFF_QK).T.astype(BF16)
    nq = 2 * DIFF_QK
    nj = ts // ATT_BLK
    return pl.pallas_call(
        _q_kernel,
        grid=(B, S // ts),
        in_specs=[pl.BlockSpec((1, ts, D), lambda b, j: (b, j, 0)), _full((nq, D))],
        out_specs=pl.BlockSpec((1, nj, nq, ATT_BLK), lambda b, j: (b, j, 0, 0)),
        out_shape=jax.ShapeDtypeStruct((B, S // ATT_BLK, nq, ATT_BLK), BF16),
        compiler_params=pltpu.CompilerParams(dimension_semantics=("parallel", "parallel"),
                                             vmem_limit_bytes=VMEM_LIMIT),
        name="q_proj",
    )(h, wqt)


def _bias_kernel(tab_ref, o_ref):
    h = pl.program_id(0)
    kk = lax.broadcasted_iota(jnp.int32, (ATT_BLK, ATT_BLK), 0)
    qq = lax.broadcasted_iota(jnp.int32, (ATT_BLK, ATT_BLK), 1)
    nb = REL_BUCKETS // 2
    far = tab_ref[nb - 1, h]
    for t in range(2):
        rel = kk - qq - t * ATT_BLK
        n = jnp.abs(rel)
        large = nb // 2
        for thr in (12, 16, 23, 32, 46, 64, 91):
            large = large + (n >= thr).astype(jnp.int32)
        bucket = jnp.where(rel > 0, nb, 0) + jnp.where(n < nb // 2, n, large)
        bias = jnp.zeros((ATT_BLK, ATT_BLK), F32)
        for b in range(REL_BUCKETS):
            bias = jnp.where(bucket == b, tab_ref[b, h], bias)
        bias = (bias - far) * LOG2E
        if t == 0:
            visible = (kk // CHUNK) <= (qq // CHUNK)
            bias = jnp.where(visible, bias, NEG_INF)
        o_ref[0, t] = bias


def _bias_tiles(rel_table):
    return pl.pallas_call(
        _bias_kernel,
        grid=(DIFF_HEADS,),
        in_specs=[pl.BlockSpec(memory_space=pltpu.SMEM)],
        out_specs=pl.BlockSpec((1, 2, ATT_BLK, ATT_BLK), lambda h: (h, 0, 0, 0)),
        out_shape=jax.ShapeDtypeStruct((DIFF_HEADS, 2, ATT_BLK, ATT_BLK), F32),
        name="bias_tiles",
    )(rel_table)


def _attn_kernel(lam_ref, qt_ref, k_ref, vt_ref, bias_ref, g_ref, o_ref, *scratch, lambda_init):
    blk = ATT_BLK
    qi = pl.program_id(2)
    n_far = jnp.maximum(qi - 1, 0)
    prev_block = jnp.maximum(qi - 1, 0)
    heads = [scratch[hh * 7:(hh + 1) * 7] for hh in range(ATT_HPS)]
    z = jnp.zeros((DIFF_DH, blk), BF16)
    for hh, (w_ref, _, _, _, _, acc_ref, ml_ref) in enumerate(heads):
        qt = qt_ref[0, 0, hh * 2 * DIFF_DH:(hh + 1) * 2 * DIFF_DH, :]
        w_ref[...] = jnp.concatenate(
            [jnp.concatenate([qt[:DIFF_DH], z], 0), jnp.concatenate([z, qt[DIFF_DH:]], 0)], 1)
        ml_ref[0:1, :] = jnp.full((1, 2 * blk), NEG_INF, F32)
        ml_ref[1:2, :] = jnp.zeros((1, 2 * blk), F32)
        acc_ref[...] = jnp.zeros_like(acc_ref)

    def scores(kb, slot, bias_idx=None, extra=None):
        for hh, (w_ref, s0_ref, s1_ref, t0_ref, t1_ref, _, _) in enumerate(heads):
            s_ref, t_ref = ((s0_ref, t0_ref), (s1_ref, t1_ref))[slot]
            kt = k_ref[0, pl.ds(pl.multiple_of(kb * blk, blk), blk), hh * LANES:(hh + 1) * LANES]
            s = jnp.dot(kt, w_ref[...], preferred_element_type=F32)
            if bias_idx is not None:
                bias = bias_ref[hh, bias_idx]
                s = s + jnp.concatenate([bias, bias], axis=1)
            if extra is not None:
                s = s + extra
            s_ref[...] = s
            t_ref[0:1, :] = jnp.max(s, 0, keepdims=True)

    def accumulate(kb, slot):
        for hh, (_, s0_ref, s1_ref, t0_ref, t1_ref, acc_ref, ml_ref) in enumerate(heads):
            s_ref, t_ref = ((s0_ref, t0_ref), (s1_ref, t1_ref))[slot]
            m_old = ml_ref[0:1, :]
            m_new = jnp.maximum(m_old, t_ref[0:1, :])
            a = jnp.exp2(m_old - m_new)
            p = jnp.exp2(s_ref[...] - m_new)
            ml_ref[1:2, :] = a * ml_ref[1:2, :] + jnp.sum(p, 0, keepdims=True)
            ml_ref[0:1, :] = m_new
            pb = p.astype(BF16)
            vt = vt_ref[0, kb, hh * DIFF_DV:(hh + 1) * DIFF_DV, :]
            for i in range(2):
                cols = slice(i * blk, (i + 1) * blk)
                acc_ref[i] = a[:, cols] * acc_ref[i] + jnp.dot(vt, pb[:, cols], preferred_element_type=F32)

    scores(qi, 0, bias_idx=0)
    scores(prev_block, 1, bias_idx=1, extra=jnp.where(qi >= 1, 0.0, -jnp.inf).astype(F32))
    accumulate(qi, 0)

    def pair(j, carry):
        kb = 2 * j
        scores(kb, 0)
        accumulate(jnp.where(j == 0, prev_block, kb - 1), 1)
        scores(kb + 1, 1)
        accumulate(kb, 0)
        return carry

    n_pairs = n_far // 2
    lax.fori_loop(0, n_pairs, pair, 0)
    last_slot1 = jnp.where(n_pairs == 0, prev_block, 2 * n_pairs - 1)

    @pl.when(n_far % 2 == 1)
    def _():
        scores(n_far - 1, 0)
        accumulate(last_slot1, 1)
        accumulate(n_far - 1, 0)

    @pl.when(n_far % 2 == 0)
    def _():
        accumulate(last_slot1, 1)

    lam = (jnp.exp(jnp.sum(lam_ref[0:1, :] * lam_ref[1:2, :], keepdims=True))
           - jnp.exp(jnp.sum(lam_ref[2:3, :] * lam_ref[3:4, :], keepdims=True)) + lambda_init)
    for hh, (_, _, _, _, _, acc_ref, ml_ref) in enumerate(heads):
        l = ml_ref[1:2, :]
        ot = acc_ref[0] / l[:, :blk] - lam * (acc_ref[1] / l[:, blk:])
        ot = ot * lax.rsqrt(jnp.mean(ot * ot, 0, keepdims=True) + LN_EPS) * g_ref[...] * (1.0 - lambda_init)
        o_ref[0, :, hh * DIFF_DV:(hh + 1) * DIFF_DV] = ot.T.astype(BF16)


def _diff_attention(qt, k12, vt, bias, lam4, g_sub, lambda_init):
    B, nqb, _, _ = qt.shape
    S = k12.shape[1]
    kernel = functools.partial(_attn_kernel, lambda_init=lambda_init)
    once = pl.Buffered(1)
    head_scratch = [pltpu.VMEM((2 * DIFF_DH, 2 * ATT_BLK), BF16),
                    pltpu.VMEM((ATT_BLK, 2 * ATT_BLK), F32), pltpu.VMEM((ATT_BLK, 2 * ATT_BLK), F32),
                    pltpu.VMEM((8, 2 * ATT_BLK), F32), pltpu.VMEM((8, 2 * ATT_BLK), F32),
                    pltpu.VMEM((2, DIFF_DV, ATT_BLK), F32), pltpu.VMEM((8, 2 * ATT_BLK), F32)]
    return pl.pallas_call(
        kernel,
        grid=(B, DIFF_HEADS // ATT_HPS, nqb),
        in_specs=[
            _full((4, DIFF_DH)),
            pl.BlockSpec((1, 1, ATT_HPS * 2 * DIFF_DH, ATT_BLK), lambda b, h, i: (b, i, h, 0)),
            pl.BlockSpec((1, S, ATT_HPS * 2 * DIFF_DH), lambda b, h, i: (b, 0, h), pipeline_mode=once),
            pl.BlockSpec((1, nqb, ATT_HPS * DIFF_DV, ATT_BLK), lambda b, h, i: (b, 0, h, 0),
                         pipeline_mode=once),
            pl.BlockSpec((ATT_HPS, 2, ATT_BLK, ATT_BLK), lambda b, h, i: (h, 0, 0, 0), pipeline_mode=once),
            _full((DIFF_DV, 1)),
        ],
        out_specs=pl.BlockSpec((1, ATT_BLK, ATT_HPS * DIFF_DV), lambda b, h, i: (b, i, h)),
        out_shape=jax.ShapeDtypeStruct((B, S, DIFF_HEADS * DIFF_DV), BF16),
        scratch_shapes=head_scratch * ATT_HPS,
        compiler_params=pltpu.CompilerParams(dimension_semantics=("parallel", "parallel", "arbitrary"),
                                             vmem_limit_bytes=VMEM_LIMIT),
        name="diff_attn",
    )(lam4, qt, k12, vt, bias, g_sub.reshape(-1, 1))


def _proj_ln_kernel(h_ref, a_ref, w_ref, lng_ref, lnb_ref, o_ref):
    mix = jnp.dot(a_ref[...], w_ref[...], preferred_element_type=F32)
    o_ref[...] = _layer_norm(DEEPNORM_ALPHA * h_ref[...] + mix, lng_ref[...], lnb_ref[...])


def _proj_ln(h, a, w_out, ln_g, ln_b, *, tb=512):
    T, D = h.shape
    return pl.pallas_call(
        _proj_ln_kernel,
        grid=(T // tb,),
        in_specs=[pl.BlockSpec((tb, D), lambda i: (i, 0)), pl.BlockSpec((tb, a.shape[1]), lambda i: (i, 0)),
                  _full(w_out.shape), _full((1, D)), _full((1, D))],
        out_specs=pl.BlockSpec((tb, D), lambda i: (i, 0)),
        out_shape=jax.ShapeDtypeStruct((T, D), F32),
        compiler_params=pltpu.CompilerParams(dimension_semantics=("parallel",), vmem_limit_bytes=VMEM_LIMIT),
        name="attn_out_ln",
    )(h, a, w_out.astype(BF16), ln_g.reshape(1, -1), ln_b.reshape(1, -1))


def kernel(x, a_w_in, a_w_gate2, a_b_gate, a_g_norm, a_w_out, kv_w, b_w_q, b_lam_q1, b_lam_k1, b_lam_q2, b_lam_k2, b_g_sub, b_w_out, rel_table, moe_w_group, moe_b_group, moe_w_router, moe_b_router, moe_w_gate, moe_w_up, moe_w_down, ln_g, ln_b):
    B, S, D = x.shape
    h = x
    bias = None
    k12 = vt = None
    for layer in range(DEPTH):
        if layer < N_A_LAYERS:
            h = _gla_layer(h, a_w_in[layer], a_w_gate2[layer], a_b_gate[layer], a_g_norm[layer],
                           a_w_out[layer], ln_g[layer, 0], ln_b[layer, 0])
        else:
            j = layer - N_A_LAYERS
            lambda_init = 0.8 - 0.6 * math.exp(-0.3 * layer)
            if bias is None:
                bias = _bias_tiles(rel_table)
            qt = _q_proj(h, b_w_q[j])
            lam4 = jnp.stack([b_lam_q1[j], b_lam_k1[j], b_lam_q2[j], b_lam_k2[j]])
            a = _diff_attention(qt, k12, vt, bias, lam4, b_g_sub[j], lambda_init)
            h = _proj_ln(h.reshape(B * S, D), a.reshape(B * S, -1), b_w_out[j], ln_g[layer, 0],
                         ln_b[layer, 0]).reshape(B, S, D)
        h = _moe_layer(h.reshape(B * S, D), moe_w_group[layer], moe_b_group[layer], moe_w_router[layer],
                       moe_b_router[layer], moe_w_gate[layer], moe_w_up[layer], moe_w_down[layer],
                       ln_g[layer, 1], ln_b[layer, 1]).reshape(B, S, D)
        if layer == N_A_LAYERS - 1:
            k12, vt = _kv_proj(h, kv_w)
    return h
```

```python
import functools
import math

import jax
import jax.numpy as jnp
from jax import lax
from jax.experimental import pallas as pl
from jax.experimental.pallas import tpu as pltpu

F32 = jnp.float32
BF16 = jnp.bfloat16

D_MODEL = 1024
DEPTH = 4
CHUNK = 64
N_A_LAYERS = 2

GLA_HEADS = 4
GLA_DK = 128
GLA_DV = 256
GLA_GATE_RANK = 16
GLA_GATE_TAU = 16.0
GLA_HK = GLA_HEADS * GLA_DK
GLA_HV = GLA_HEADS * GLA_DV

DIFF_HEADS = 8
DIFF_DH = 64
DIFF_DV = 128
DIFF_QK = DIFF_HEADS * DIFF_DH

REL_BUCKETS = 32

MOE_GROUPS = 4
MOE_EPG = 4
MOE_EXPERTS = 16
MOE_FF = 512

DEEPNORM_ALPHA = (2.0 * DEPTH) ** 0.25
LN_EPS = 1e-5
NEG_INF = -1e30

LANES = 128
ATT_BLK = 512
ATT_HPS = 2
ATT_SUM_ROWS = 16
LOG2E = math.log2(math.e)
VMEM_LIMIT = 48 * 1024 * 1024
MOE_VMEM_LIMIT = 58 * 1024 * 1024
MOE_SUB = 128
MOE_EPS = 2

_HIGHEST = lax.Precision.HIGHEST


def _layer_norm(y, g, b):
    mu = jnp.mean(y, -1, keepdims=True)
    d = y - mu
    var = jnp.mean(d * d, -1, keepdims=True)
    return d * lax.rsqrt(var + LN_EPS) * g + b


def _sigmoid(x):
    return 1.0 / (1.0 + jnp.exp(-x))


def _full(shape):
    return pl.BlockSpec(shape, lambda *_: (0,) * len(shape))


def _gla_kernel(x_ref, w_main_ref, w_kt_ref, w_lr_ref, w_g2_ref, b_g_ref, gn_ref, w_out_ref, lng_ref, lnb_ref,
                tri_ref, o_ref, st_ref, mix_ref):
    nb, tb, d = x_ref.shape
    nt = (((1,), (1,)), ((), ()))

    @pl.when(pl.program_id(0) == 0)
    def _():
        st_ref[...] = jnp.zeros_like(st_ref)

    x = x_ref[...].reshape(nb * tb, d)
    xb = x.astype(BF16)
    lrt = lax.dot_general(w_lr_ref[...], xb, nt, preferred_element_type=F32)
    kt = lax.dot_general(w_kt_ref[...], xb, nt, preferred_element_type=F32)
    hq = jnp.dot(xb, w_main_ref[:, :GLA_HK], preferred_element_type=F32)
    glog = jnp.dot(w_g2_ref[...], lrt.astype(BF16), preferred_element_type=F32) + b_g_ref[...]
    hv = jnp.dot(xb, w_main_ref[:, GLA_HK:GLA_HK + GLA_HV], preferred_element_type=F32)
    la = (jnp.minimum(glog, 0.0) - jnp.log(1.0 + jnp.exp(-jnp.abs(glog)))) * (1.0 / GLA_GATE_TAU)
    la_hi = la.astype(BF16)
    la_lo = (la - la_hi.astype(F32)).astype(BF16)
    tri = tri_ref[...]
    cum = [jnp.dot(la_hi[:, b * tb:(b + 1) * tb], tri, preferred_element_type=F32)
           + jnp.dot(la_lo[:, b * tb:(b + 1) * tb], tri, preferred_element_type=F32) for b in range(nb)]
    r = jnp.dot(xb, w_main_ref[:, GLA_HK + GLA_HV:], preferred_element_type=F32)

    lane = lax.broadcasted_iota(jnp.int32, (GLA_HK, 2 * CHUNK), 1)
    halves = (lane < CHUNK, lane >= CHUNK)
    heads = [(slice(h * GLA_DK, (h + 1) * GLA_DK), slice(h * GLA_DV, (h + 1) * GLA_DV)) for h in range(GLA_HEADS)]
    for pair in range(tb // (2 * CHUNK)):
        c0 = pair * 2 * CHUNK
        dec, upd = {}, {}
        for b in range(nb):
            t0 = b * tb + c0
            cum_p = cum[b][:, c0:c0 + 2 * CHUNK]
            last = (cum_p[:, CHUNK - 1:CHUNK], cum_p[:, 2 * CHUNK - 1:2 * CHUNK])
            kdec = kt[:, t0:t0 + 2 * CHUNK] * jnp.exp(jnp.where(halves[0], last[0], last[1]) - cum_p)
            v_pair = hv[t0:t0 + 2 * CHUNK].astype(BF16)
            for half in range(2):
                kd = jnp.where(halves[half], kdec, 0.0).astype(BF16)
                dec[b, half] = jnp.exp(last[half])
                for h, (ks, vs) in enumerate(heads):
                    upd[b, half, h] = jnp.dot(kd[ks, :], v_pair[:, vs], preferred_element_type=F32)
        for half in range(2):
            for b in range(nb):
                r0 = b * tb + c0 + half * CHUNK
                qc = (hq[r0:r0 + CHUNK] * (GLA_DK ** -0.5)).astype(BF16)
                for h, (ks, vs) in enumerate(heads):
                    st = st_ref[b, h] * dec[b, half][ks, :] + upd[b, half, h]
                    st_ref[b, h] = st
                    o = jnp.dot(qc[:, ks], st.astype(BF16), preferred_element_type=F32)
                    o = o * lax.rsqrt(jnp.mean(o * o, -1, keepdims=True) + LN_EPS) * gn_ref[:, vs]
                    mix_ref[r0:r0 + CHUNK, vs] = o

    gated = mix_ref[...] * (r * _sigmoid(r))
    mix = jnp.dot(gated.astype(BF16), w_out_ref[...], preferred_element_type=F32)
    o_ref[...] = _layer_norm(DEEPNORM_ALPHA * x + mix, lng_ref[...], lnb_ref[...]).reshape(nb, tb, d)


def _gla_layer(h, w_in, w_gate2, b_gate, g_norm, w_out, ln_g, ln_b, *, tb=256):
    B, S, D = h.shape
    n_k = 2 * GLA_HK
    n_main = GLA_HK + 2 * GLA_HV
    n_all = 2 * GLA_HK + 2 * GLA_HV
    w_main = jnp.concatenate([w_in[:, :GLA_HK], w_in[:, n_k:n_all]], axis=1).astype(BF16)
    w_kt = w_in[:, GLA_HK:n_k].T.astype(BF16)
    w_lr = jnp.zeros((LANES, D), F32).at[:GLA_GATE_RANK].set(w_in[:, n_all:].T).astype(BF16)
    w_g2 = jnp.zeros((GLA_HK, LANES), F32).at[:, :GLA_GATE_RANK].set(w_gate2.T).astype(BF16)
    idx = jnp.arange(tb)
    tri = ((idx[:, None] <= idx[None, :]) & (idx[:, None] // CHUNK == idx[None, :] // CHUNK)).astype(BF16)
    return pl.pallas_call(
        _gla_kernel,
        grid=(S // tb,),
        in_specs=[
            pl.BlockSpec((B, tb, D), lambda j: (0, j, 0)),
            _full((D, n_main)), _full((GLA_HK, D)), _full((LANES, D)), _full((GLA_HK, LANES)),
            _full((GLA_HK, 1)), _full((1, GLA_HV)), _full((GLA_HV, D)), _full((1, D)), _full((1, D)),
            _full((tb, tb)),
        ],
        out_specs=pl.BlockSpec((B, tb, D), lambda j: (0, j, 0)),
        out_shape=jax.ShapeDtypeStruct((B, S, D), F32),
        scratch_shapes=[pltpu.VMEM((B, GLA_HEADS, GLA_DK, GLA_DV), F32), pltpu.VMEM((B * tb, GLA_HV), F32)],
        compiler_params=pltpu.CompilerParams(dimension_semantics=("arbitrary",),
                                             vmem_limit_bytes=VMEM_LIMIT),
        name="gla_layer",
    )(h, w_main, w_kt, w_lr, w_g2, b_gate.reshape(-1, 1), g_norm.reshape(1, -1), w_out.astype(BF16),
      ln_g.reshape(1, -1), ln_b.reshape(1, -1), tri)


def _route(logits):
    lane = lax.broadcasted_iota(jnp.int32, logits.shape, 1).astype(F32)
    big = float(1 << 20)
    gl = jnp.where(lane < MOE_GROUPS, logits, -jnp.inf)
    gmax = jnp.max(gl, -1, keepdims=True)
    gidx = jnp.min(jnp.where(gl == gmax, lane, big), -1, keepdims=True)
    gw = 1.0 / jnp.sum(jnp.exp(gl - gmax), -1, keepdims=True)
    lo = MOE_GROUPS + gidx * MOE_EPG
    emask = (lane >= lo) & (lane < lo + MOE_EPG)
    el = jnp.where(emask, logits, -jnp.inf)
    pe = jnp.exp(el - jnp.max(el, -1, keepdims=True))
    prob = pe / jnp.sum(pe, -1, keepdims=True)
    pm = jnp.where(emask, prob, -1.0)
    p1 = jnp.max(pm, -1, keepdims=True)
    i1 = jnp.min(jnp.where(pm == p1, lane, big), -1, keepdims=True)
    pm2 = jnp.where(lane == i1, -1.0, pm)
    p2 = jnp.max(pm2, -1, keepdims=True)
    i2 = jnp.min(jnp.where(pm2 == p2, lane, big), -1, keepdims=True)
    den = p1 + p2
    comb = jnp.where(lane == i1, p1 / den, 0.0) + jnp.where(lane == i2, p2 / den, 0.0)
    return comb * gw, gidx.astype(jnp.int32)


def _split3(v):
    hi = v.astype(BF16).astype(F32)
    mid = (v - hi).astype(BF16).astype(F32)
    lo = (v - hi - mid).astype(BF16).astype(F32)
    return hi, mid, lo


def _moe_kernel(x_ref, wr_ref, br_ref, wg_ref, wu_ref, wd_ref, lng_ref, lnb_ref, o_ref,
                xs_ref, cw_ref, pt_ref, y_ref, meta_ref):
    step = pl.program_id(1)
    steps_per_group = MOE_EPG // MOE_EPS
    grp = step // steps_per_group
    j0 = (step % steps_per_group) * MOE_EPS
    tb = x_ref.shape[0]
    tbs = xs_ref.shape[0]
    d = x_ref.shape[1]

    @pl.when(step == 0)
    def _():
        x = x_ref[...]
        xh = x.astype(BF16)
        xl = (x - xh.astype(F32)).astype(BF16)
        x2 = jnp.concatenate([xh, xl], axis=1)
        half = tb // 2
        both = jnp.concatenate([jnp.dot(x2[:half], wr_ref[...], preferred_element_type=F32),
                                jnp.dot(x2[half:], wr_ref[...], preferred_element_type=F32)], axis=0)
        logits = both[:, :LANES] + both[:, LANES:] + br_ref[...]
        comb, gidx = _route(logits)
        lane = lax.broadcasted_iota(jnp.int32, (tb, LANES), 1)
        lane1 = lax.broadcasted_iota(jnp.int32, (1, LANES), 1)
        onehot = jnp.where(lane == gidx, 1.0, 0.0)
        earlier = jnp.where(lax.broadcasted_iota(jnp.int32, (LANES, LANES), 0)
                            > lax.broadcasted_iota(jnp.int32, (LANES, LANES), 1), 1.0, 0.0).astype(BF16)
        onehot_b = onehot.astype(BF16)
        before, carry = [], jnp.zeros((1, LANES), F32)
        for piece in range(tb // LANES):
            oh = onehot_b[piece * LANES:(piece + 1) * LANES]
            before.append(jnp.dot(earlier, oh, preferred_element_type=F32) + carry)
            carry = carry + jnp.sum(onehot[piece * LANES:(piece + 1) * LANES], 0, keepdims=True)
        rank = jnp.sum(jnp.concatenate(before, axis=0) * onehot, -1, keepdims=True)
        cnt = carry
        ntile = jnp.floor((cnt + (MOE_SUB - 1.0)) * (1.0 / MOE_SUB))
        off = jnp.zeros((1, LANES), F32)
        for gg in range(MOE_GROUPS - 1):
            rows_g = jnp.sum(jnp.where(lane1 == gg, ntile, 0.0), -1, keepdims=True) * MOE_SUB
            off = off + jnp.where(lane1 > gg, rows_g, 0.0)
        for gg in range(MOE_GROUPS):
            meta_ref[gg] = jnp.sum(jnp.where(lane1 == gg, ntile, 0.0)).astype(jnp.int32)
            meta_ref[MOE_GROUPS + gg] = jnp.sum(jnp.where(lane1 == gg, off, 0.0)).astype(jnp.int32)
        pos = jnp.sum(onehot * off, -1, keepdims=True) + rank
        pt_ref[...] = jnp.where(lax.broadcasted_iota(jnp.int32, (tb, tbs), 1) == pos.astype(jnp.int32),
                                1.0, 0.0).astype(BF16)
        pos_row = jnp.broadcast_to(pos, (tb, LANES)).T[0:1, :].astype(jnp.int32)
        perm = jnp.where(lax.broadcasted_iota(jnp.int32, (tbs, tb), 0) == pos_row, 1.0, 0.0).astype(BF16)
        lo = MOE_GROUPS + gidx * MOE_EPG
        cw = jnp.zeros((tb, LANES), F32)
        for jj in range(MOE_EPG):
            parts = _split3(jnp.sum(jnp.where(lane == lo + jj, comb, 0.0), -1, keepdims=True))
            for k, part in enumerate(parts):
                cw = cw + jnp.where(lane == jj + k * MOE_EPG, part, 0.0)
        sorted_all = jnp.dot(perm, jnp.concatenate([xh, cw.astype(BF16)], axis=1),
                             preferred_element_type=F32)
        xs_ref[...] = sorted_all[:, :d].astype(BF16)
        cw_ref[...] = sorted_all[:, d:]
        y_ref[...] = jnp.zeros_like(y_ref)

    base = meta_ref[MOE_GROUPS + grp]

    def experts(start, nrows):
        rows = pl.ds(pl.multiple_of(start, MOE_SUB), nrows)
        xt = xs_ref[rows, :]
        cwt = cw_ref[rows, :]
        lane = lax.broadcasted_iota(jnp.int32, (nrows, LANES), 1)
        hidden = []
        for k in range(MOE_EPS):
            gt = jnp.dot(xt, wg_ref[k], preferred_element_type=F32)
            ut = jnp.dot(xt, wu_ref[k], preferred_element_type=F32)
            hidden.append(((gt * _sigmoid(gt)) * ut).astype(BF16))
        total = None
        for k in range(MOE_EPS):
            y = jnp.dot(hidden[k], wd_ref[k], preferred_element_type=F32)
            mine = (lane % MOE_EPG == j0 + k) & (lane < 3 * MOE_EPG)
            c = jnp.sum(jnp.where(mine, cwt, 0.0), -1, keepdims=True)
            total = c * y if total is None else total + c * y
        y_ref[rows, :] += total

    n_tiles = meta_ref[grp]

    def wide(i, carry):
        experts(base + i * (2 * MOE_SUB), 2 * MOE_SUB)
        return carry

    lax.fori_loop(0, n_tiles // 2, wide, 0)

    @pl.when(n_tiles % 2 == 1)
    def _():
        experts(base + (n_tiles - 1) * MOE_SUB, MOE_SUB)

    @pl.when(step == MOE_EXPERTS // MOE_EPS - 1)
    def _():
        yb = y_ref[...].astype(BF16)
        piece = tb // 4
        for c in range(4):
            rows = slice(c * piece, (c + 1) * piece)
            ffn = jnp.dot(pt_ref[rows, :], yb, preferred_element_type=F32)
            o_ref[rows, :] = _layer_norm(DEEPNORM_ALPHA * x_ref[rows, :] + ffn, lng_ref[...], lnb_ref[...])


def _moe_layer(h, w_group, b_group, w_router, b_router, w_gate, w_up, w_down, ln_g, ln_b, *, tb=1024):
    T, D = h.shape
    wr = jnp.zeros((D, LANES), F32).at[:, :MOE_GROUPS].set(w_group)
    wr = wr.at[:, MOE_GROUPS:MOE_GROUPS + MOE_EXPERTS].set(w_router)
    wrh = wr.astype(BF16)
    wrl = (wr - wrh.astype(F32)).astype(BF16)
    wr2 = jnp.concatenate([jnp.concatenate([wrh, wrl], axis=1),
                           jnp.concatenate([wrh, jnp.zeros_like(wrl)], axis=1)], axis=0)
    br = jnp.zeros((1, LANES), F32).at[0, :MOE_GROUPS].set(b_group)
    br = br.at[0, MOE_GROUPS:MOE_GROUPS + MOE_EXPERTS].set(b_router.reshape(-1))
    tbs = tb + MOE_GROUPS * MOE_SUB
    return pl.pallas_call(
        _moe_kernel,
        grid=(T // tb, MOE_EXPERTS // MOE_EPS),
        in_specs=[
            pl.BlockSpec((tb, D), lambda i, e: (i, 0)),
            _full((2 * D, 2 * LANES)), _full((1, LANES)),
            pl.BlockSpec((MOE_EPS, D, MOE_FF), lambda i, e: (e, 0, 0)),
            pl.BlockSpec((MOE_EPS, D, MOE_FF), lambda i, e: (e, 0, 0)),
            pl.BlockSpec((MOE_EPS, MOE_FF, D), lambda i, e: (e, 0, 0)),
            _full((1, D)), _full((1, D)),
        ],
        out_specs=pl.BlockSpec((tb, D), lambda i, e: (i, 0)),
        out_shape=jax.ShapeDtypeStruct((T, D), F32),
        scratch_shapes=[pltpu.VMEM((tbs, D), BF16), pltpu.VMEM((tbs, LANES), F32), pltpu.VMEM((tb, tbs), BF16),
                        pltpu.VMEM((tbs, D), F32), pltpu.SMEM((2 * MOE_GROUPS,), jnp.int32)],
        compiler_params=pltpu.CompilerParams(dimension_semantics=("parallel", "arbitrary"),
                                             vmem_limit_bytes=MOE_VMEM_LIMIT),
        name="moe_layer",
    )(h, wr2, br, w_gate.astype(BF16), w_up.astype(BF16), w_down.astype(BF16),
      ln_g.reshape(1, -1), ln_b.reshape(1, -1))


def _kv_kernel(x_ref, wk_ref, wvt_ref, k_ref, vt_ref):
    xb = x_ref[0].astype(BF16)
    k_ref[0] = jnp.dot(xb, wk_ref[...], preferred_element_type=F32).astype(BF16)
    vt = lax.dot_general(wvt_ref[...], xb, (((1,), (1,)), ((), ())), preferred_element_type=F32)
    for j in range(vt_ref.shape[1]):
        vt_ref[0, j] = vt[:, j * ATT_BLK:(j + 1) * ATT_BLK].astype(BF16)


def _kv_proj(h, kv_w, *, ts=512):
    B, S, D = h.shape
    k1 = kv_w[:, :DIFF_QK].reshape(D, DIFF_HEADS, DIFF_DH)
    k2 = kv_w[:, DIFF_QK:2 * DIFF_QK].reshape(D, DIFF_HEADS, DIFF_DH)
    wk = jnp.stack([k1, k2], axis=2).reshape(D, 2 * DIFF_QK).astype(BF16)
    wvt = kv_w[:, 2 * DIFF_QK:].T.astype(BF16)
    nv = DIFF_HEADS * DIFF_DV
    nj = ts // ATT_BLK
    return pl.pallas_call(
        _kv_kernel,
        grid=(B, S // ts),
        in_specs=[pl.BlockSpec((1, ts, D), lambda b, j: (b, j, 0)), _full((D, 2 * DIFF_QK)), _full((nv, D))],
        out_specs=[pl.BlockSpec((1, ts, 2 * DIFF_QK), lambda b, j: (b, j, 0)),
                   pl.BlockSpec((1, nj, nv, ATT_BLK), lambda b, j: (b, j, 0, 0))],
        out_shape=[jax.ShapeDtypeStruct((B, S, 2 * DIFF_QK), BF16),
                   jax.ShapeDtypeStruct((B, S // ATT_BLK, nv, ATT_BLK), BF16)],
        compiler_params=pltpu.CompilerParams(dimension_semantics=("parallel", "parallel"),
                                             vmem_limit_bytes=VMEM_LIMIT),
        name="kv_proj",
    )(h, wk, wvt)


def _q_kernel(x_ref, wqt_ref, qt_ref):
    xb = x_ref[0].astype(BF16)
    qt = lax.dot_general(wqt_ref[...], xb, (((1,), (1,)), ((), ())), preferred_element_type=F32)
    qt = qt * (DIFF_DH ** -0.5 * LOG2E)
    for j in range(qt_ref.shape[1]):
        qt_ref[0, j] = qt[:, j * ATT_BLK:(j + 1) * ATT_BLK].astype(BF16)


def _q_proj(h, w_q, *, ts=512):
    B, S, D = h.shape
    q1 = w_q[:, :DIFF_QK].reshape(D, DIFF_HEADS, DIFF_DH)
    q2 = w_q[:, DIFF_QK:].reshape(D, DIFF_HEADS, DIFF_DH)
    wqt = jnp.stack([q1, q2], axis=2).reshape(D, 2 * DIFF_QK).T.astype(BF16)
    nq = 2 * DIFF_QK
    nj = ts // ATT_BLK
    return pl.pallas_call(
        _q_kernel,
        grid=(B, S // ts),
        in_specs=[pl.BlockSpec((1, ts, D), lambda b, j: (b, j, 0)), _full((nq, D))],
        out_specs=pl.BlockSpec((1, nj, nq, ATT_BLK), lambda b, j: (b, j, 0, 0)),
        out_shape=jax.ShapeDtypeStruct((B, S // ATT_BLK, nq, ATT_BLK), BF16),
        compiler_params=pltpu.CompilerParams(dimension_semantics=("parallel", "parallel"),
                                             vmem_limit_bytes=VMEM_LIMIT),
        name="q_proj",
    )(h, wqt)


def _bias_kernel(tab_ref, o_ref):
    h = pl.program_id(0)
    kk = lax.broadcasted_iota(jnp.int32, (ATT_BLK, ATT_BLK), 0)
    qq = lax.broadcasted_iota(jnp.int32, (ATT_BLK, ATT_BLK), 1)
    nb = REL_BUCKETS // 2
    far = tab_ref[nb - 1, h]
    for t in range(2):
        rel = kk - qq - t * ATT_BLK
        n = jnp.abs(rel)
        large = nb // 2
        for thr in (12, 16, 23, 32, 46, 64, 91):
            large = large + (n >= thr).astype(jnp.int32)
        bucket = jnp.where(rel > 0, nb, 0) + jnp.where(n < nb // 2, n, large)
        bias = jnp.zeros((ATT_BLK, ATT_BLK), F32)
        for b in range(REL_BUCKETS):
            bias = jnp.where(bucket == b, tab_ref[b, h], bias)
        bias = (bias - far) * LOG2E
        if t == 0:
            visible = (kk // CHUNK) <= (qq // CHUNK)
            bias = jnp.where(visible, bias, NEG_INF)
        o_ref[0, t] = bias


def _bias_tiles(rel_table):
    return pl.pallas_call(
        _bias_kernel,
        grid=(DIFF_HEADS,),
        in_specs=[pl.BlockSpec(memory_space=pltpu.SMEM)],
        out_specs=pl.BlockSpec((1, 2, ATT_BLK, ATT_BLK), lambda h: (h, 0, 0, 0)),
        out_shape=jax.ShapeDtypeStruct((DIFF_HEADS, 2, ATT_BLK, ATT_BLK), F32),
        name="bias_tiles",
    )(rel_table)


def _attn_kernel(lam_ref, qt_ref, k_ref, vt_ref, bias_ref, g_ref, o_ref, *scratch, lambda_init):
    blk = ATT_BLK
    qi = pl.program_id(2)
    n_far = jnp.maximum(qi - 1, 0)
    prev_block = jnp.maximum(qi - 1, 0)
    heads = [scratch[hh * 7:(hh + 1) * 7] for hh in range(ATT_HPS)]
    z = jnp.zeros((DIFF_DH, blk), BF16)
    ones_rows = jnp.ones((ATT_SUM_ROWS, blk), BF16)
    for hh, (w_ref, _, _, _, _, acc_ref, ml_ref) in enumerate(heads):
        qt = qt_ref[0, 0, hh * 2 * DIFF_DH:(hh + 1) * 2 * DIFF_DH, :]
        w_ref[...] = jnp.concatenate(
            [jnp.concatenate([qt[:DIFF_DH], z], 0), jnp.concatenate([z, qt[DIFF_DH:]], 0)], 1)
        ml_ref[0:1, :] = jnp.full((1, 2 * blk), NEG_INF, F32)
        acc_ref[...] = jnp.zeros_like(acc_ref)

    def scores(kb, slot, bias_idx=None, extra=None):
        for hh, (w_ref, s0_ref, s1_ref, t0_ref, t1_ref, _, _) in enumerate(heads):
            s_ref, t_ref = ((s0_ref, t0_ref), (s1_ref, t1_ref))[slot]
            kt = k_ref[0, pl.ds(pl.multiple_of(kb * blk, blk), blk), hh * LANES:(hh + 1) * LANES]
            s = jnp.dot(kt, w_ref[...], preferred_element_type=F32)
            if bias_idx is not None:
                bias = bias_ref[hh, bias_idx]
                s = s + jnp.concatenate([bias, bias], axis=1)
            if extra is not None:
                s = s + extra
            s_ref[...] = s
            t_ref[0:1, :] = jnp.max(s, 0, keepdims=True)

    def accumulate(kb, slot):
        for hh, (_, s0_ref, s1_ref, t0_ref, t1_ref, acc_ref, ml_ref) in enumerate(heads):
            s_ref, t_ref = ((s0_ref, t0_ref), (s1_ref, t1_ref))[slot]
            m_old = ml_ref[0:1, :]
            m_new = jnp.maximum(m_old, t_ref[0:1, :])
            a = jnp.exp2(m_old - m_new)
            p = jnp.exp2(s_ref[...] - m_new)
            ml_ref[0:1, :] = m_new
            pb = p.astype(BF16)
            vt = jnp.concatenate([vt_ref[0, kb, hh * DIFF_DV:(hh + 1) * DIFF_DV, :], ones_rows], axis=0)
            for i in range(2):
                cols = slice(i * blk, (i + 1) * blk)
                acc_ref[i] = a[:, cols] * acc_ref[i] + jnp.dot(vt, pb[:, cols], preferred_element_type=F32)

    def fused(kb_a, slot_a, kb_b, slot_b, bias_idx=None, extra=None):
        nslab = 4
        rows = blk // nslab
        width = 2 * blk // nslab
        for hh, (w_ref, s0_ref, s1_ref, t0_ref, t1_ref, acc_ref, ml_ref) in enumerate(heads):
            sa_ref, ta_ref = ((s0_ref, t0_ref), (s1_ref, t1_ref))[slot_a]
            sb_ref, tb_ref = ((s0_ref, t0_ref), (s1_ref, t1_ref))[slot_b]
            m_old = ml_ref[0:1, :]
            m_new = jnp.maximum(m_old, tb_ref[0:1, :])
            a = jnp.exp2(m_old - m_new)
            ml_ref[0:1, :] = m_new
            vt = jnp.concatenate([vt_ref[0, kb_b, hh * DIFF_DV:(hh + 1) * DIFF_DV, :], ones_rows], axis=0)
            tmax = None
            for j in range(nslab):
                cols = slice(j * width, (j + 1) * width)
                p = jnp.exp2(sb_ref[:, cols] - m_new[:, cols])
                i, c0 = divmod(j * width, blk)
                acc_ref[i, :, c0:c0 + width] = (a[:, cols] * acc_ref[i, :, c0:c0 + width]
                                                + jnp.dot(vt, p.astype(BF16), preferred_element_type=F32))
                kt = k_ref[0, pl.ds(pl.multiple_of(kb_a * blk + j * rows, rows), rows),
                           hh * LANES:(hh + 1) * LANES]
                s = jnp.dot(kt, w_ref[...], preferred_element_type=F32)
                if bias_idx is not None:
                    bias = bias_ref[hh, bias_idx, j * rows:(j + 1) * rows, :]
                    s = s + jnp.concatenate([bias, bias], axis=1)
                if extra is not None:
                    s = s + extra
                sa_ref[j * rows:(j + 1) * rows, :] = s
                smax = jnp.max(s, 0, keepdims=True)
                tmax = smax if tmax is None else jnp.maximum(tmax, smax)
            ta_ref[0:1, :] = tmax

    scores(qi, 0, bias_idx=0)
    fused(prev_block, 1, qi, 0, bias_idx=1, extra=jnp.where(qi >= 1, 0.0, -jnp.inf).astype(F32))

    def pair(j, carry):
        kb = 2 * j
        fused(kb, 0, jnp.where(j == 0, prev_block, kb - 1), 1)
        fused(kb + 1, 1, kb, 0)
        return carry

    n_pairs = n_far // 2
    lax.fori_loop(0, n_pairs, pair, 0)
    last_slot1 = jnp.where(n_pairs == 0, prev_block, 2 * n_pairs - 1)

    @pl.when(n_far % 2 == 1)
    def _():
        fused(n_far - 1, 0, last_slot1, 1)
        accumulate(n_far - 1, 0)

    @pl.when(n_far % 2 == 0)
    def _():
        accumulate(last_slot1, 1)

    lam = (jnp.exp(jnp.sum(lam_ref[0:1, :] * lam_ref[1:2, :], keepdims=True))
           - jnp.exp(jnp.sum(lam_ref[2:3, :] * lam_ref[3:4, :], keepdims=True)) + lambda_init)
    for hh, (_, _, _, _, _, acc_ref, _) in enumerate(heads):
        o1, o2 = acc_ref[0], acc_ref[1]
        ot = (o1[:DIFF_DV] / o1[DIFF_DV:DIFF_DV + 1] - lam * (o2[:DIFF_DV] / o2[DIFF_DV:DIFF_DV + 1]))
        ot = ot * lax.rsqrt(jnp.mean(ot * ot, 0, keepdims=True) + LN_EPS) * g_ref[...] * (1.0 - lambda_init)
        o_ref[0, :, hh * DIFF_DV:(hh + 1) * DIFF_DV] = ot.T.astype(BF16)


def _diff_attention(qt, k12, vt, bias, lam4, g_sub, lambda_init):
    B, nqb, _, _ = qt.shape
    S = k12.shape[1]
    kernel = functools.partial(_attn_kernel, lambda_init=lambda_init)
    once = pl.Buffered(1)
    head_scratch = [pltpu.VMEM((2 * DIFF_DH, 2 * ATT_BLK), BF16),
                    pltpu.VMEM((ATT_BLK, 2 * ATT_BLK), F32), pltpu.VMEM((ATT_BLK, 2 * ATT_BLK), F32),
                    pltpu.VMEM((8, 2 * ATT_BLK), F32), pltpu.VMEM((8, 2 * ATT_BLK), F32),
                    pltpu.VMEM((2, DIFF_DV + ATT_SUM_ROWS, ATT_BLK), F32), pltpu.VMEM((8, 2 * ATT_BLK), F32)]
    return pl.pallas_call(
        kernel,
        grid=(B, DIFF_HEADS // ATT_HPS, nqb),
        in_specs=[
            _full((4, DIFF_DH)),
            pl.BlockSpec((1, 1, ATT_HPS * 2 * DIFF_DH, ATT_BLK), lambda b, h, i: (b, i, h, 0)),
            pl.BlockSpec((1, S, ATT_HPS * 2 * DIFF_DH), lambda b, h, i: (b, 0, h), pipeline_mode=once),
            pl.BlockSpec((1, nqb, ATT_HPS * DIFF_DV, ATT_BLK), lambda b, h, i: (b, 0, h, 0),
                         pipeline_mode=once),
            pl.BlockSpec((ATT_HPS, 2, ATT_BLK, ATT_BLK), lambda b, h, i: (h, 0, 0, 0), pipeline_mode=once),
            _full((DIFF_DV, 1)),
        ],
        out_specs=pl.BlockSpec((1, ATT_BLK, ATT_HPS * DIFF_DV), lambda b, h, i: (b, i, h)),
        out_shape=jax.ShapeDtypeStruct((B, S, DIFF_HEADS * DIFF_DV), BF16),
        scratch_shapes=head_scratch * ATT_HPS,
        compiler_params=pltpu.CompilerParams(dimension_semantics=("parallel", "parallel", "arbitrary"),
                                             vmem_limit_bytes=VMEM_LIMIT),
        name="diff_attn",
    )(lam4, qt, k12, vt, bias, g_sub.reshape(-1, 1))


def _proj_ln_kernel(h_ref, a_ref, w_ref, lng_ref, lnb_ref, o_ref):
    mix = jnp.dot(a_ref[...], w_ref[...], preferred_element_type=F32)
    o_ref[...] = _layer_norm(DEEPNORM_ALPHA * h_ref[...] + mix, lng_ref[...], lnb_ref[...])


def _proj_ln(h, a, w_out, ln_g, ln_b, *, tb=512):
    T, D = h.shape
    return pl.pallas_call(
        _proj_ln_kernel,
        grid=(T // tb,),
        in_specs=[pl.BlockSpec((tb, D), lambda i: (i, 0)), pl.BlockSpec((tb, a.shape[1]), lambda i: (i, 0)),
                  _full(w_out.shape), _full((1, D)), _full((1, D))],
        out_specs=pl.BlockSpec((tb, D), lambda i: (i, 0)),
        out_shape=jax.ShapeDtypeStruct((T, D), F32),
        compiler_params=pltpu.CompilerParams(dimension_semantics=("parallel",), vmem_limit_bytes=VMEM_LIMIT),
        name="attn_out_ln",
    )(h, a, w_out.astype(BF16), ln_g.reshape(1, -1), ln_b.reshape(1, -1))


def kernel(x, a_w_in, a_w_gate2, a_b_gate, a_g_norm, a_w_out, kv_w, b_w_q, b_lam_q1, b_lam_k1, b_lam_q2, b_lam_k2, b_g_sub, b_w_out, rel_table, moe_w_group, moe_b_group, moe_w_router, moe_b_router, moe_w_gate, moe_w_up, moe_w_down, ln_g, ln_b):
    B, S, D = x.shape
    h = x
    bias = None
    k12 = vt = None
    for layer in range(DEPTH):
        if layer < N_A_LAYERS:
            h = _gla_layer(h, a_w_in[layer], a_w_gate2[layer], a_b_gate[layer], a_g_norm[layer],
                           a_w_out[layer], ln_g[layer, 0], ln_b[layer, 0])
        else:
            j = layer - N_A_LAYERS
            lambda_init = 0.8 - 0.6 * math.exp(-0.3 * layer)
            if bias is None:
                bias = _bias_tiles(rel_table)
            qt = _q_proj(h, b_w_q[j])
            lam4 = jnp.stack([b_lam_q1[j], b_lam_k1[j], b_lam_q2[j], b_lam_k2[j]])
            a = _diff_attention(qt, k12, vt, bias, lam4, b_g_sub[j], lambda_init)
            h = _proj_ln(h.reshape(B * S, D), a.reshape(B * S, -1), b_w_out[j], ln_g[layer, 0],
                         ln_b[layer, 0]).reshape(B, S, D)
        h = _moe_layer(h.reshape(B * S, D), moe_w_group[layer], moe_b_group[layer], moe_w_router[layer],
                       moe_b_router[layer], moe_w_gate[layer], moe_w_up[layer], moe_w_down[layer],
                       ln_g[layer, 1], ln_b[layer, 1]).reshape(B, S, D)
        if layer == N_A_LAYERS - 1:
            k12, vt = _kv_proj(h, kv_w)
    return h
```

```python
import functools
import math

import jax
import jax.numpy as jnp
from jax import lax
from jax.experimental import pallas as pl
from jax.experimental.pallas import tpu as pltpu

F32 = jnp.float32
BF16 = jnp.bfloat16

D_MODEL = 1024
DEPTH = 4
CHUNK = 64
N_A_LAYERS = 2

GLA_HEADS = 4
GLA_DK = 128
GLA_DV = 256
GLA_GATE_RANK = 16
GLA_GATE_TAU = 16.0
GLA_HK = GLA_HEADS * GLA_DK
GLA_HV = GLA_HEADS * GLA_DV

DIFF_HEADS = 8
DIFF_DH = 64
DIFF_DV = 128
DIFF_QK = DIFF_HEADS * DIFF_DH

REL_BUCKETS = 32

MOE_GROUPS = 4
MOE_EPG = 4
MOE_EXPERTS = 16
MOE_FF = 512

DEEPNORM_ALPHA = (2.0 * DEPTH) ** 0.25
LN_EPS = 1e-5
NEG_INF = -1e30

LANES = 128
ATT_BLK = 512
ATT_HPS = 2
ATT_SUM_ROWS = 16
LOG2E = math.log2(math.e)
VMEM_LIMIT = 48 * 1024 * 1024
MOE_VMEM_LIMIT = 58 * 1024 * 1024
MOE_SUB = 128
MOE_EPS = 2
MOE_POS_LANE = 12
MOE_TR_ROWS = 16

_HIGHEST = lax.Precision.HIGHEST


def _layer_norm(y, g, b):
    mu = jnp.mean(y, -1, keepdims=True)
    d = y - mu
    var = jnp.mean(d * d, -1, keepdims=True)
    return d * lax.rsqrt(var + LN_EPS) * g + b


def _sigmoid(x):
    return 1.0 / (1.0 + jnp.exp(-x))


def _full(shape):
    return pl.BlockSpec(shape, lambda *_: (0,) * len(shape))


def _gla_kernel(x_ref, w_main_ref, w_kt_ref, w_lr_ref, w_g2_ref, b_g_ref, gn_ref, w_out_ref, lng_ref, lnb_ref,
                tri_ref, o_ref, st_ref, mix_ref):
    nb, tb, d = x_ref.shape
    nt = (((1,), (1,)), ((), ()))

    @pl.when(pl.program_id(0) == 0)
    def _():
        st_ref[...] = jnp.zeros_like(st_ref)

    x = x_ref[...].reshape(nb * tb, d)
    xb = x.astype(BF16)
    lrt = lax.dot_general(w_lr_ref[...], xb, nt, preferred_element_type=F32)
    kt = lax.dot_general(w_kt_ref[...], xb, nt, preferred_element_type=F32)
    hq = jnp.dot(xb, w_main_ref[:, :GLA_HK], preferred_element_type=F32)
    glog = jnp.dot(w_g2_ref[...], lrt.astype(BF16), preferred_element_type=F32) + b_g_ref[...]
    hv = jnp.dot(xb, w_main_ref[:, GLA_HK:GLA_HK + GLA_HV], preferred_element_type=F32)
    la = (jnp.minimum(glog, 0.0) - jnp.log(1.0 + jnp.exp(-jnp.abs(glog)))) * (1.0 / GLA_GATE_TAU)
    la_hi = la.astype(BF16)
    la_lo = (la - la_hi.astype(F32)).astype(BF16)
    tri = tri_ref[...]
    cum = [jnp.dot(la_hi[:, b * tb:(b + 1) * tb], tri, preferred_element_type=F32)
           + jnp.dot(la_lo[:, b * tb:(b + 1) * tb], tri, preferred_element_type=F32) for b in range(nb)]
    r = jnp.dot(xb, w_main_ref[:, GLA_HK + GLA_HV:], preferred_element_type=F32)

    lane = lax.broadcasted_iota(jnp.int32, (GLA_HK, 2 * CHUNK), 1)
    halves = (lane < CHUNK, lane >= CHUNK)
    heads = [(slice(h * GLA_DK, (h + 1) * GLA_DK), slice(h * GLA_DV, (h + 1) * GLA_DV)) for h in range(GLA_HEADS)]
    for pair in range(tb // (2 * CHUNK)):
        c0 = pair * 2 * CHUNK
        dec, upd = {}, {}
        for b in range(nb):
            t0 = b * tb + c0
            cum_p = cum[b][:, c0:c0 + 2 * CHUNK]
            last = (cum_p[:, CHUNK - 1:CHUNK], cum_p[:, 2 * CHUNK - 1:2 * CHUNK])
            kdec = kt[:, t0:t0 + 2 * CHUNK] * jnp.exp(jnp.where(halves[0], last[0], last[1]) - cum_p)
            v_pair = hv[t0:t0 + 2 * CHUNK].astype(BF16)
            for half in range(2):
                kd = jnp.where(halves[half], kdec, 0.0).astype(BF16)
                dec[b, half] = jnp.exp(last[half])
                for h, (ks, vs) in enumerate(heads):
                    upd[b, half, h] = jnp.dot(kd[ks, :], v_pair[:, vs], preferred_element_type=F32)
        for half in range(2):
            for b in range(nb):
                r0 = b * tb + c0 + half * CHUNK
                qc = (hq[r0:r0 + CHUNK] * (GLA_DK ** -0.5)).astype(BF16)
                for h, (ks, vs) in enumerate(heads):
                    st = st_ref[b, h] * dec[b, half][ks, :] + upd[b, half, h]
                    st_ref[b, h] = st
                    o = jnp.dot(qc[:, ks], st.astype(BF16), preferred_element_type=F32)
                    o = o * lax.rsqrt(jnp.mean(o * o, -1, keepdims=True) + LN_EPS) * gn_ref[:, vs]
                    mix_ref[r0:r0 + CHUNK, vs] = o

    gated = mix_ref[...] * (r * _sigmoid(r))
    mix = jnp.dot(gated.astype(BF16), w_out_ref[...], preferred_element_type=F32)
    o_ref[...] = _layer_norm(DEEPNORM_ALPHA * x + mix, lng_ref[...], lnb_ref[...]).reshape(nb, tb, d)


def _gla_layer(h, w_in, w_gate2, b_gate, g_norm, w_out, ln_g, ln_b, *, tb=256):
    B, S, D = h.shape
    n_k = 2 * GLA_HK
    n_main = GLA_HK + 2 * GLA_HV
    n_all = 2 * GLA_HK + 2 * GLA_HV
    w_main = jnp.concatenate([w_in[:, :GLA_HK], w_in[:, n_k:n_all]], axis=1).astype(BF16)
    w_kt = w_in[:, GLA_HK:n_k].T.astype(BF16)
    w_lr = jnp.zeros((LANES, D), F32).at[:GLA_GATE_RANK].set(w_in[:, n_all:].T).astype(BF16)
    w_g2 = jnp.zeros((GLA_HK, LANES), F32).at[:, :GLA_GATE_RANK].set(w_gate2.T).astype(BF16)
    idx = jnp.arange(tb)
    tri = ((idx[:, None] <= idx[None, :]) & (idx[:, None] // CHUNK == idx[None, :] // CHUNK)).astype(BF16)
    return pl.pallas_call(
        _gla_kernel,
        grid=(S // tb,),
        in_specs=[
            pl.BlockSpec((B, tb, D), lambda j: (0, j, 0)),
            _full((D, n_main)), _full((GLA_HK, D)), _full((LANES, D)), _full((GLA_HK, LANES)),
            _full((GLA_HK, 1)), _full((1, GLA_HV)), _full((GLA_HV, D)), _full((1, D)), _full((1, D)),
            _full((tb, tb)),
        ],
        out_specs=pl.BlockSpec((B, tb, D), lambda j: (0, j, 0)),
        out_shape=jax.ShapeDtypeStruct((B, S, D), F32),
        scratch_shapes=[pltpu.VMEM((B, GLA_HEADS, GLA_DK, GLA_DV), F32), pltpu.VMEM((B * tb, GLA_HV), F32)],
        compiler_params=pltpu.CompilerParams(dimension_semantics=("arbitrary",),
                                             vmem_limit_bytes=VMEM_LIMIT),
        name="gla_layer",
    )(h, w_main, w_kt, w_lr, w_g2, b_gate.reshape(-1, 1), g_norm.reshape(1, -1), w_out.astype(BF16),
      ln_g.reshape(1, -1), ln_b.reshape(1, -1), tri)


def _first_max(vals):
    top = functools.reduce(jnp.maximum, vals)
    idx = jnp.full_like(top, float(len(vals) - 1))
    for j in range(len(vals) - 2, -1, -1):
        idx = jnp.where(vals[j] == top, float(j), idx)
    return top, idx


def _route(lt):
    gl = [lt[g:g + 1, :] for g in range(MOE_GROUPS)]
    gmax, gidx = _first_max(gl)
    gw = 1.0 / functools.reduce(jnp.add, [jnp.exp(v - gmax) for v in gl])
    el = []
    for j in range(MOE_EPG):
        rows = [lt[MOE_GROUPS + g * MOE_EPG + j:MOE_GROUPS + g * MOE_EPG + j + 1, :] for g in range(MOE_GROUPS)]
        sel = rows[MOE_GROUPS - 1]
        for g in range(MOE_GROUPS - 2, -1, -1):
            sel = jnp.where(gidx == float(g), rows[g], sel)
        el.append(sel)
    emax = functools.reduce(jnp.maximum, el)
    pe = [jnp.exp(v - emax) for v in el]
    psum = functools.reduce(jnp.add, pe)
    prob = [v / psum for v in pe]
    p1, i1 = _first_max(prob)
    p2, i2 = _first_max([jnp.where(i1 == float(j), -1.0, prob[j]) for j in range(MOE_EPG)])
    den = p1 + p2
    weights = [(jnp.where(i1 == float(j), p1 / den, 0.0) + jnp.where(i2 == float(j), p2 / den, 0.0)) * gw
               for j in range(MOE_EPG)]
    return gidx, weights


def _split3(v):
    hi = v.astype(BF16).astype(F32)
    mid = (v - hi).astype(BF16).astype(F32)
    lo = (v - hi - mid).astype(BF16).astype(F32)
    return hi, mid, lo


def _router_logits(x, wr_ref, br_ref):
    tb = x.shape[0]
    xh = x.astype(BF16)
    xl = (x - xh.astype(F32)).astype(BF16)
    x2 = jnp.concatenate([xh, xl], axis=1)
    half = tb // 2
    both = jnp.concatenate([jnp.dot(x2[:half], wr_ref[...], preferred_element_type=F32),
                            jnp.dot(x2[half:], wr_ref[...], preferred_element_type=F32)], axis=0)
    return both[:, :LANES] + both[:, LANES:] + br_ref[...]


def _moe_route(logits, tr_ref, posr_ref, cwn_ref, meta_ref, meta_base):
    tb = logits.shape[0]
    gidx, weights = _route(logits.T)
    onehot = [jnp.where(gidx == float(g), 1.0, 0.0) for g in range(MOE_GROUPS)]
    oht = jnp.concatenate(onehot + [jnp.zeros((MOE_TR_ROWS - MOE_GROUPS, tb), F32)], axis=0)
    ohb = oht.astype(BF16)
    earlier = jnp.where(lax.broadcasted_iota(jnp.int32, (LANES, LANES), 0)
                        < lax.broadcasted_iota(jnp.int32, (LANES, LANES), 1), 1.0, 0.0).astype(BF16)
    before, carry = [], jnp.zeros((MOE_TR_ROWS, 1), F32)
    for piece in range(tb // LANES):
        cols = slice(piece * LANES, (piece + 1) * LANES)
        before.append(jnp.dot(ohb[:, cols], earlier, preferred_element_type=F32) + carry)
        carry = carry + jnp.sum(oht[:, cols], -1, keepdims=True)
    rank = jnp.sum(jnp.concatenate(before, axis=1) * oht, 0, keepdims=True)
    ntile = jnp.floor((carry + (MOE_SUB - 1.0)) * (1.0 / MOE_SUB))
    pos, first_row = rank, jnp.zeros((1, 1), F32)
    for g in range(MOE_GROUPS):
        meta_ref[meta_base + g] = jnp.sum(ntile[g:g + 1, :]).astype(jnp.int32)
        meta_ref[meta_base + MOE_GROUPS + g] = jnp.sum(first_row).astype(jnp.int32)
        pos = pos + onehot[g] * first_row
        first_row = first_row + ntile[g:g + 1, :] * MOE_SUB
    posr_ref[0:1, :] = pos
    for j in range(MOE_EPG):
        for k, part in enumerate(_split3(weights[j])):
            tr_ref[j + k * MOE_EPG:j + k * MOE_EPG + 1, :] = part
    tr_ref[MOE_POS_LANE:MOE_POS_LANE + 1, :] = pos
    tr_ref[MOE_POS_LANE + 1:MOE_TR_ROWS, :] = jnp.zeros((MOE_TR_ROWS - MOE_POS_LANE - 1, tb), F32)
    cwn_ref[...] = tr_ref[...].T


def _moe_kernel(x_ref, wr_ref, br_ref, wg_ref, wu_ref, wd_ref, lng_ref, lnb_ref, o_ref,
                xs_ref, cw_ref, pt_ref, y_ref, res_ref, tr_ref, posr_ref, cwn_ref, meta_ref):
    blk_i = pl.program_id(0)
    step = pl.program_id(1)
    steps_per_group = MOE_EPG // MOE_EPS
    grp = step // steps_per_group
    j0 = (step % steps_per_group) * MOE_EPS
    tb = x_ref.shape[0]
    tbs = xs_ref.shape[0]
    d = x_ref.shape[1]
    meta_cur = (blk_i % 2) * (2 * MOE_GROUPS)
    meta_next = 2 * MOE_GROUPS - meta_cur

    @pl.when(step == 0)
    def _():
        @pl.when(blk_i == 0)
        def _():
            tr_ref[...] = jnp.zeros_like(tr_ref)
            _moe_route(_router_logits(x_ref[...], wr_ref, br_ref), tr_ref, posr_ref, cwn_ref, meta_ref, meta_cur)

        x = x_ref[...]
        res_ref[...] = x
        cwn = cwn_ref[...]
        pos_col = cwn[:, MOE_POS_LANE:MOE_POS_LANE + 1].astype(jnp.int32)
        pt_ref[...] = jnp.where(lax.broadcasted_iota(jnp.int32, (tb, tbs), 1) == pos_col, 1.0, 0.0).astype(BF16)
        pos_row = posr_ref[0:1, :].astype(jnp.int32)
        perm = jnp.where(lax.broadcasted_iota(jnp.int32, (tbs, tb), 0) == pos_row, 1.0, 0.0).astype(BF16)
        sorted_all = jnp.dot(perm, jnp.concatenate([x.astype(BF16), cwn.astype(BF16)], axis=1),
                             preferred_element_type=F32)
        xs_ref[...] = sorted_all[:, :d].astype(BF16)
        cw_ref[...] = sorted_all[:, d:]
        y_ref[...] = jnp.zeros_like(y_ref)

    base = meta_ref[meta_cur + MOE_GROUPS + grp]

    def experts(start, nrows):
        rows = pl.ds(pl.multiple_of(start, MOE_SUB), nrows)
        xt = xs_ref[rows, :]
        cwt = cw_ref[rows, :]
        lane = lax.broadcasted_iota(jnp.int32, (nrows, LANES), 1)
        hidden = []
        for k in range(MOE_EPS):
            gt = jnp.dot(xt, wg_ref[k], preferred_element_type=F32)
            ut = jnp.dot(xt, wu_ref[k], preferred_element_type=F32)
            hidden.append(((gt * _sigmoid(gt)) * ut).astype(BF16))
        total = None
        for k in range(MOE_EPS):
            y = jnp.dot(hidden[k], wd_ref[k], preferred_element_type=F32)
            mine = (lane % MOE_EPG == j0 + k) & (lane < 3 * MOE_EPG)
            c = jnp.sum(jnp.where(mine, cwt, 0.0), -1, keepdims=True)
            total = c * y if total is None else total + c * y
        y_ref[rows, :] += total

    n_tiles = meta_ref[meta_cur + grp]

    def wide(i, carry):
        experts(base + i * (2 * MOE_SUB), 2 * MOE_SUB)
        return carry

    lax.fori_loop(0, n_tiles // 2, wide, 0)

    @pl.when(n_tiles % 2 == 1)
    def _():
        experts(base + (n_tiles - 1) * MOE_SUB, MOE_SUB)

    @pl.when(step == MOE_EXPERTS // MOE_EPS - 1)
    def _():
        logits = _router_logits(x_ref[...], wr_ref, br_ref)
        yb = y_ref[...].astype(BF16)
        npiece = 4
        piece = tb // npiece
        for c in range(npiece):
            if c == npiece - 1:
                _moe_route(logits, tr_ref, posr_ref, cwn_ref, meta_ref, meta_next)
            rows = slice(c * piece, (c + 1) * piece)
            ffn = jnp.dot(pt_ref[rows, :], yb, preferred_element_type=F32)
            o_ref[rows, :] = _layer_norm(DEEPNORM_ALPHA * res_ref[rows, :] + ffn, lng_ref[...], lnb_ref[...])


def _moe_layer(h, w_group, b_group, w_router, b_router, w_gate, w_up, w_down, ln_g, ln_b, *, tb=1024):
    T, D = h.shape
    wr = jnp.zeros((D, LANES), F32).at[:, :MOE_GROUPS].set(w_group)
    wr = wr.at[:, MOE_GROUPS:MOE_GROUPS + MOE_EXPERTS].set(w_router)
    wrh = wr.astype(BF16)
    wrl = (wr - wrh.astype(F32)).astype(BF16)
    wr2 = jnp.concatenate([jnp.concatenate([wrh, wrl], axis=1),
                           jnp.concatenate([wrh, jnp.zeros_like(wrl)], axis=1)], axis=0)
    br = jnp.zeros((1, LANES), F32).at[0, :MOE_GROUPS].set(b_group)
    br = br.at[0, MOE_GROUPS:MOE_GROUPS + MOE_EXPERTS].set(b_router.reshape(-1))
    tbs = tb + MOE_GROUPS * MOE_SUB
    nblk = T // tb
    return pl.pallas_call(
        _moe_kernel,
        grid=(nblk, MOE_EXPERTS // MOE_EPS),
        in_specs=[
            pl.BlockSpec((tb, D), lambda i, e: (jnp.where(e == 0, i, jnp.minimum(i + 1, nblk - 1)), 0)),
            _full((2 * D, 2 * LANES)), _full((1, LANES)),
            pl.BlockSpec((MOE_EPS, D, MOE_FF), lambda i, e: (e, 0, 0)),
            pl.BlockSpec((MOE_EPS, D, MOE_FF), lambda i, e: (e, 0, 0)),
            pl.BlockSpec((MOE_EPS, MOE_FF, D), lambda i, e: (e, 0, 0)),
            _full((1, D)), _full((1, D)),
        ],
        out_specs=pl.BlockSpec((tb, D), lambda i, e: (i, 0)),
        out_shape=jax.ShapeDtypeStruct((T, D), F32),
        scratch_shapes=[pltpu.VMEM((tbs, D), BF16), pltpu.VMEM((tbs, LANES), F32), pltpu.VMEM((tb, tbs), BF16),
                        pltpu.VMEM((tbs, D), F32), pltpu.VMEM((tb, D), F32), pltpu.VMEM((LANES, tb), F32),
                        pltpu.VMEM((8, tb), F32), pltpu.VMEM((tb, LANES), F32),
                        pltpu.SMEM((4 * MOE_GROUPS,), jnp.int32)],
        compiler_params=pltpu.CompilerParams(dimension_semantics=("arbitrary", "arbitrary"),
                                             vmem_limit_bytes=MOE_VMEM_LIMIT),
        name="moe_layer",
    )(h, wr2, br, w_gate.astype(BF16), w_up.astype(BF16), w_down.astype(BF16),
      ln_g.reshape(1, -1), ln_b.reshape(1, -1))


def _kv_kernel(x_ref, wk_ref, wvt_ref, k_ref, vt_ref):
    xb = x_ref[0].astype(BF16)
    k_ref[0] = jnp.dot(xb, wk_ref[...], preferred_element_type=F32).astype(BF16)
    vt = lax.dot_general(wvt_ref[...], xb, (((1,), (1,)), ((), ())), preferred_element_type=F32)
    for j in range(vt_ref.shape[1]):
        vt_ref[0, j] = vt[:, j * ATT_BLK:(j + 1) * ATT_BLK].astype(BF16)


def _kv_proj(h, kv_w, *, ts=512):
    B, S, D = h.shape
    k1 = kv_w[:, :DIFF_QK].reshape(D, DIFF_HEADS, DIFF_DH)
    k2 = kv_w[:, DIFF_QK:2 * DIFF_QK].reshape(D, DIFF_HEADS, DIFF_DH)
    wk = jnp.stack([k1, k2], axis=2).reshape(D, 2 * DIFF_QK).astype(BF16)
    wvt = kv_w[:, 2 * DIFF_QK:].T.astype(BF16)
    nv = DIFF_HEADS * DIFF_DV
    nj = ts // ATT_BLK
    return pl.pallas_call(
        _kv_kernel,
        grid=(B, S // ts),
        in_specs=[pl.BlockSpec((1, ts, D), lambda b, j: (b, j, 0)), _full((D, 2 * DIFF_QK)), _full((nv, D))],
        out_specs=[pl.BlockSpec((1, ts, 2 * DIFF_QK), lambda b, j: (b, j, 0)),
                   pl.BlockSpec((1, nj, nv, ATT_BLK), lambda b, j: (b, j, 0, 0))],
        out_shape=[jax.ShapeDtypeStruct((B, S, 2 * DIFF_QK), BF16),
                   jax.ShapeDtypeStruct((B, S // ATT_BLK, nv, ATT_BLK), BF16)],
        compiler_params=pltpu.CompilerParams(dimension_semantics=("parallel", "parallel"),
                                             vmem_limit_bytes=VMEM_LIMIT),
        name="kv_proj",
    )(h, wk, wvt)


def _q_kernel(x_ref, wqt_ref, qt_ref):
    xb = x_ref[0].astype(BF16)
    qt = lax.dot_general(wqt_ref[...], xb, (((1,), (1,)), ((), ())), preferred_element_type=F32)
    qt = qt * (DIFF_DH ** -0.5 * LOG2E)
    for j in range(qt_ref.shape[1]):
        qt_ref[0, j] = qt[:, j * ATT_BLK:(j + 1) * ATT_BLK].astype(BF16)


def _q_proj(h, w_q, *, ts=512):
    B, S, D = h.shape
    q1 = w_q[:, :DIFF_QK].reshape(D, DIFF_HEADS, DIFF_DH)
    q2 = w_q[:, DIFF_QK:].reshape(D, DIFF_HEADS, DIFF_DH)
    wqt = jnp.stack([q1, q2], axis=2).reshape(D, 2 * DIFF_QK).T.astype(BF16)
    nq = 2 * DIFF_QK
    nj = ts // ATT_BLK
    return pl.pallas_call(
        _q_kernel,
        grid=(B, S // ts),
        in_specs=[pl.BlockSpec((1, ts, D), lambda b, j: (b, j, 0)), _full((nq, D))],
        out_specs=pl.BlockSpec((1, nj, nq, ATT_BLK), lambda b, j: (b, j, 0, 0)),
        out_shape=jax.ShapeDtypeStruct((B, S // ATT_BLK, nq, ATT_BLK), BF16),
        compiler_params=pltpu.CompilerParams(dimension_semantics=("parallel", "parallel"),
                                             vmem_limit_bytes=VMEM_LIMIT),
        name="q_proj",
    )(h, wqt)


def _bias_kernel(tab_ref, o_ref):
    h = pl.program_id(0)
    kk = lax.broadcasted_iota(jnp.int32, (ATT_BLK, ATT_BLK), 0)
    qq = lax.broadcasted_iota(jnp.int32, (ATT_BLK, ATT_BLK), 1)
    nb = REL_BUCKETS // 2
    far = tab_ref[nb - 1, h]
    for t in range(2):
        rel = kk - qq - t * ATT_BLK
        n = jnp.abs(rel)
        large = nb // 2
        for thr in (12, 16, 23, 32, 46, 64, 91):
            large = large + (n >= thr).astype(jnp.int32)
        bucket = jnp.where(rel > 0, nb, 0) + jnp.where(n < nb // 2, n, large)
        bias = jnp.zeros((ATT_BLK, ATT_BLK), F32)
        for b in range(REL_BUCKETS):
            bias = jnp.where(bucket == b, tab_ref[b, h], bias)
        bias = (bias - far) * LOG2E
        if t == 0:
            visible = (kk // CHUNK) <= (qq // CHUNK)
            bias = jnp.where(visible, bias, NEG_INF)
        o_ref[0, t] = bias


def _bias_tiles(rel_table):
    return pl.pallas_call(
        _bias_kernel,
        grid=(DIFF_HEADS,),
        in_specs=[pl.BlockSpec(memory_space=pltpu.SMEM)],
        out_specs=pl.BlockSpec((1, 2, ATT_BLK, ATT_BLK), lambda h: (h, 0, 0, 0)),
        out_shape=jax.ShapeDtypeStruct((DIFF_HEADS, 2, ATT_BLK, ATT_BLK), F32),
        name="bias_tiles",
    )(rel_table)


def _attn_kernel(lam_ref, qt_ref, k_ref, vt_ref, bias_ref, g_ref, o_ref, *scratch, lambda_init):
    blk = ATT_BLK
    qi = pl.program_id(2)
    n_far = jnp.maximum(qi - 1, 0)
    prev_block = jnp.maximum(qi - 1, 0)
    heads = [scratch[hh * 7:(hh + 1) * 7] for hh in range(ATT_HPS)]
    z = jnp.zeros((DIFF_DH, blk), BF16)
    ones_rows = jnp.ones((ATT_SUM_ROWS, blk), BF16)
    for hh, (w_ref, _, _, _, _, acc_ref, ml_ref) in enumerate(heads):
        qt = qt_ref[0, 0, hh * 2 * DIFF_DH:(hh + 1) * 2 * DIFF_DH, :]
        w_ref[...] = jnp.concatenate(
            [jnp.concatenate([qt[:DIFF_DH], z], 0), jnp.concatenate([z, qt[DIFF_DH:]], 0)], 1)
        ml_ref[0:1, :] = jnp.full((1, 2 * blk), NEG_INF, F32)
        acc_ref[...] = jnp.zeros_like(acc_ref)

    def scores(kb, slot, bias_idx=None, extra=None):
        for hh, (w_ref, s0_ref, s1_ref, t0_ref, t1_ref, _, _) in enumerate(heads):
            s_ref, t_ref = ((s0_ref, t0_ref), (s1_ref, t1_ref))[slot]
            kt = k_ref[0, pl.ds(pl.multiple_of(kb * blk, blk), blk), hh * LANES:(hh + 1) * LANES]
            s = jnp.dot(kt, w_ref[...], preferred_element_type=F32)
            if bias_idx is not None:
                bias = bias_ref[hh, bias_idx]
                s = s + jnp.concatenate([bias, bias], axis=1)
            if extra is not None:
                s = s + extra
            s_ref[...] = s
            t_ref[0:1, :] = jnp.max(s, 0, keepdims=True)

    def accumulate(kb, slot, then=None):
        for hh, (_, s0_ref, s1_ref, t0_ref, t1_ref, acc_ref, ml_ref) in enumerate(heads):
            s_ref, t_ref = ((s0_ref, t0_ref), (s1_ref, t1_ref))[slot]
            m_old = ml_ref[0:1, :]
            m_new = jnp.maximum(m_old, t_ref[0:1, :])
            a = jnp.exp2(m_old - m_new)
            p = jnp.exp2(s_ref[...] - m_new)
            ml_ref[0:1, :] = m_new
            pb = p.astype(BF16)
            vt = jnp.concatenate([vt_ref[0, kb, hh * DIFF_DV:(hh + 1) * DIFF_DV, :], ones_rows], axis=0)
            for i in range(2):
                cols = slice(i * blk, (i + 1) * blk)
                acc_ref[i] = a[:, cols] * acc_ref[i] + jnp.dot(vt, pb[:, cols], preferred_element_type=F32)
            if then is not None:
                then(hh)

    def fused(kb_a, slot_a, kb_b, slot_b, bias_idx=None, extra=None):
        nslab = 4
        rows = blk // nslab
        width = 2 * blk // nslab
        for hh, (w_ref, s0_ref, s1_ref, t0_ref, t1_ref, acc_ref, ml_ref) in enumerate(heads):
            sa_ref, ta_ref = ((s0_ref, t0_ref), (s1_ref, t1_ref))[slot_a]
            sb_ref, tb_ref = ((s0_ref, t0_ref), (s1_ref, t1_ref))[slot_b]
            m_old = ml_ref[0:1, :]
            m_new = jnp.maximum(m_old, tb_ref[0:1, :])
            a = jnp.exp2(m_old - m_new)
            ml_ref[0:1, :] = m_new
            vt = jnp.concatenate([vt_ref[0, kb_b, hh * DIFF_DV:(hh + 1) * DIFF_DV, :], ones_rows], axis=0)
            tmax = None
            for j in range(nslab):
                cols = slice(j * width, (j + 1) * width)
                p = jnp.exp2(sb_ref[:, cols] - m_new[:, cols])
                i, c0 = divmod(j * width, blk)
                acc_ref[i, :, c0:c0 + width] = (a[:, cols] * acc_ref[i, :, c0:c0 + width]
                                                + jnp.dot(vt, p.astype(BF16), preferred_element_type=F32))
                kt = k_ref[0, pl.ds(pl.multiple_of(kb_a * blk + j * rows, rows), rows),
                           hh * LANES:(hh + 1) * LANES]
                s = jnp.dot(kt, w_ref[...], preferred_element_type=F32)
                if bias_idx is not None:
                    bias = bias_ref[hh, bias_idx, j * rows:(j + 1) * rows, :]
                    s = s + jnp.concatenate([bias, bias], axis=1)
                if extra is not None:
                    s = s + extra
                sa_ref[j * rows:(j + 1) * rows, :] = s
                smax = jnp.max(s, 0, keepdims=True)
                tmax = smax if tmax is None else jnp.maximum(tmax, smax)
            ta_ref[0:1, :] = tmax

    scores(qi, 0, bias_idx=0)
    fused(prev_block, 1, qi, 0, bias_idx=1, extra=jnp.where(qi >= 1, 0.0, -jnp.inf).astype(F32))

    def pair(j, carry):
        kb = 2 * j
        fused(kb, 0, jnp.where(j == 0, prev_block, kb - 1), 1)
        fused(kb + 1, 1, kb, 0)
        return carry

    n_pairs = n_far // 2
    lax.fori_loop(0, n_pairs, pair, 0)
    last_slot1 = jnp.where(n_pairs == 0, prev_block, 2 * n_pairs - 1)

    def finish(hh):
        acc_ref = heads[hh][5]
        lam = (jnp.exp(jnp.sum(lam_ref[0:1, :] * lam_ref[1:2, :], keepdims=True))
               - jnp.exp(jnp.sum(lam_ref[2:3, :] * lam_ref[3:4, :], keepdims=True)) + lambda_init)
        o1, o2 = acc_ref[0], acc_ref[1]
        ot = (o1[:DIFF_DV] / o1[DIFF_DV:DIFF_DV + 1] - lam * (o2[:DIFF_DV] / o2[DIFF_DV:DIFF_DV + 1]))
        ot = ot * lax.rsqrt(jnp.mean(ot * ot, 0, keepdims=True) + LN_EPS) * g_ref[...] * (1.0 - lambda_init)
        o_ref[0, :, hh * DIFF_DV:(hh + 1) * DIFF_DV] = ot.T.astype(BF16)

    @pl.when(n_far % 2 == 1)
    def _():
        fused(n_far - 1, 0, last_slot1, 1)
        accumulate(n_far - 1, 0, then=finish)

    @pl.when(n_far % 2 == 0)
    def _():
        accumulate(last_slot1, 1, then=finish)


def _diff_attention(qt, k12, vt, bias, lam4, g_sub, lambda_init):
    B, nqb, _, _ = qt.shape
    S = k12.shape[1]
    kernel = functools.partial(_attn_kernel, lambda_init=lambda_init)
    once = pl.Buffered(1)
    head_scratch = [pltpu.VMEM((2 * DIFF_DH, 2 * ATT_BLK), BF16),
                    pltpu.VMEM((ATT_BLK, 2 * ATT_BLK), F32), pltpu.VMEM((ATT_BLK, 2 * ATT_BLK), F32),
                    pltpu.VMEM((8, 2 * ATT_BLK), F32), pltpu.VMEM((8, 2 * ATT_BLK), F32),
                    pltpu.VMEM((2, DIFF_DV + ATT_SUM_ROWS, ATT_BLK), F32), pltpu.VMEM((8, 2 * ATT_BLK), F32)]
    return pl.pallas_call(
        kernel,
        grid=(B, DIFF_HEADS // ATT_HPS, nqb),
        in_specs=[
            _full((4, DIFF_DH)),
            pl.BlockSpec((1, 1, ATT_HPS * 2 * DIFF_DH, ATT_BLK), lambda b, h, i: (b, i, h, 0)),
            pl.BlockSpec((1, S, ATT_HPS * 2 * DIFF_DH), lambda b, h, i: (b, 0, h), pipeline_mode=once),
            pl.BlockSpec((1, nqb, ATT_HPS * DIFF_DV, ATT_BLK), lambda b, h, i: (b, 0, h, 0),
                         pipeline_mode=once),
            pl.BlockSpec((ATT_HPS, 2, ATT_BLK, ATT_BLK), lambda b, h, i: (h, 0, 0, 0), pipeline_mode=once),
            _full((DIFF_DV, 1)),
        ],
        out_specs=pl.BlockSpec((1, ATT_BLK, ATT_HPS * DIFF_DV), lambda b, h, i: (b, i, h)),
        out_shape=jax.ShapeDtypeStruct((B, S, DIFF_HEADS * DIFF_DV), BF16),
        scratch_shapes=head_scratch * ATT_HPS,
        compiler_params=pltpu.CompilerParams(dimension_semantics=("parallel", "parallel", "arbitrary"),
                                             vmem_limit_bytes=VMEM_LIMIT),
        name="diff_attn",
    )(lam4, qt, k12, vt, bias, g_sub.reshape(-1, 1))


def _proj_ln_kernel(h_ref, a_ref, w_ref, lng_ref, lnb_ref, o_ref):
    mix = jnp.dot(a_ref[...], w_ref[...], preferred_element_type=F32)
    o_ref[...] = _layer_norm(DEEPNORM_ALPHA * h_ref[...] + mix, lng_ref[...], lnb_ref[...])


def _proj_ln(h, a, w_out, ln_g, ln_b, *, tb=512):
    T, D = h.shape
    return pl.pallas_call(
        _proj_ln_kernel,
        grid=(T // tb,),
        in_specs=[pl.BlockSpec((tb, D), lambda i: (i, 0)), pl.BlockSpec((tb, a.shape[1]), lambda i: (i, 0)),
                  _full(w_out.shape), _full((1, D)), _full((1, D))],
        out_specs=pl.BlockSpec((tb, D), lambda i: (i, 0)),
        out_shape=jax.ShapeDtypeStruct((T, D), F32),
        compiler_params=pltpu.CompilerParams(dimension_semantics=("parallel",), vmem_limit_bytes=VMEM_LIMIT),
        name="attn_out_ln",
    )(h, a, w_out.astype(BF16), ln_g.reshape(1, -1), ln_b.reshape(1, -1))


def kernel(x, a_w_in, a_w_gate2, a_b_gate, a_g_norm, a_w_out, kv_w, b_w_q, b_lam_q1, b_lam_k1, b_lam_q2, b_lam_k2, b_g_sub, b_w_out, rel_table, moe_w_group, moe_b_group, moe_w_router, moe_b_router, moe_w_gate, moe_w_up, moe_w_down, ln_g, ln_b):
    B, S, D = x.shape
    h = x
    bias = None
    k12 = vt = None
    for layer in range(DEPTH):
        if layer < N_A_LAYERS:
            h = _gla_layer(h, a_w_in[layer], a_w_gate2[layer], a_b_gate[layer], a_g_norm[layer],
                           a_w_out[layer], ln_g[layer, 0], ln_b[layer, 0])
        else:
            j = layer - N_A_LAYERS
            lambda_init = 0.8 - 0.6 * math.exp(-0.3 * layer)
            if bias is None:
                bias = _bias_tiles(rel_table)
            qt = _q_proj(h, b_w_q[j])
            lam4 = jnp.stack([b_lam_q1[j], b_lam_k1[j], b_lam_q2[j], b_lam_k2[j]])
            a = _diff_attention(qt, k12, vt, bias, lam4, b_g_sub[j], lambda_init)
            h = _proj_ln(h.reshape(B * S, D), a.reshape(B * S, -1), b_w_out[j], ln_g[layer, 0],
                         ln_b[layer, 0]).reshape(B, S, D)
        h = _moe_layer(h.reshape(B * S, D), moe_w_group[layer], moe_b_group[layer], moe_w_router[layer],
                       moe_b_router[layer], moe_w_gate[layer], moe_w_up[layer], moe_w_down[layer],
                       ln_g[layer, 1], ln_b[layer, 1]).reshape(B, S, D)
        if layer == N_A_LAYERS - 1:
            k12, vt = _kv_proj(h, kv_w)
    return h
```

```python
import functools
import math

import jax
import jax.numpy as jnp
from jax import lax
from jax.experimental import pallas as pl
from jax.experimental.pallas import tpu as pltpu

F32 = jnp.float32
BF16 = jnp.bfloat16

D_MODEL = 1024
DEPTH = 4
CHUNK = 64
N_A_LAYERS = 2

GLA_HEADS = 4
GLA_DK = 128
GLA_DV = 256
GLA_GATE_RANK = 16
GLA_GATE_TAU = 16.0
GLA_HK = GLA_HEADS * GLA_DK
GLA_HV = GLA_HEADS * GLA_DV

DIFF_HEADS = 8
DIFF_DH = 64
DIFF_DV = 128
DIFF_QK = DIFF_HEADS * DIFF_DH

REL_BUCKETS = 32

MOE_GROUPS = 4
MOE_EPG = 4
MOE_EXPERTS = 16
MOE_FF = 512

DEEPNORM_ALPHA = (2.0 * DEPTH) ** 0.25
LN_EPS = 1e-5
NEG_INF = -1e30

LANES = 128
ATT_BLK = 512
ATT_HPS = 2
ATT_SUM_ROWS = 16
LOG2E = math.log2(math.e)
VMEM_LIMIT = 48 * 1024 * 1024
MOE_VMEM_LIMIT = 58 * 1024 * 1024
MOE_SUB = 128
MOE_EPS = 2
MOE_POS_LANE = 12
MOE_TR_ROWS = 16

_HIGHEST = lax.Precision.HIGHEST


def _layer_norm(y, g, b):
    mu = jnp.mean(y, -1, keepdims=True)
    d = y - mu
    var = jnp.mean(d * d, -1, keepdims=True)
    return d * lax.rsqrt(var + LN_EPS) * g + b


def _sigmoid(x):
    return 1.0 / (1.0 + jnp.exp(-x))


def _full(shape):
    return pl.BlockSpec(shape, lambda *_: (0,) * len(shape))


def _gla_kernel(x_ref, w_main_ref, w_kt_ref, w_lr_ref, w_g2_ref, b_g_ref, gn_ref, w_out_ref, lng_ref, lnb_ref,
                tri_ref, o_ref, st_ref, mix_ref):
    nb, tb, d = x_ref.shape
    nt = (((1,), (1,)), ((), ()))

    @pl.when(pl.program_id(0) == 0)
    def _():
        st_ref[...] = jnp.zeros_like(st_ref)

    x = x_ref[...].reshape(nb * tb, d)
    xb = x.astype(BF16)
    lrt = lax.dot_general(w_lr_ref[...], xb, nt, preferred_element_type=F32)
    kt = lax.dot_general(w_kt_ref[...], xb, nt, preferred_element_type=F32)
    hq = jnp.dot(xb, w_main_ref[:, :GLA_HK], preferred_element_type=F32)
    glog = jnp.dot(w_g2_ref[...], lrt.astype(BF16), preferred_element_type=F32) + b_g_ref[...]
    hv = jnp.dot(xb, w_main_ref[:, GLA_HK:GLA_HK + GLA_HV], preferred_element_type=F32)
    la = (jnp.minimum(glog, 0.0) - jnp.log(1.0 + jnp.exp(-jnp.abs(glog)))) * (1.0 / GLA_GATE_TAU)
    la_hi = la.astype(BF16)
    la_lo = (la - la_hi.astype(F32)).astype(BF16)
    tri = tri_ref[...]
    cum = [jnp.dot(la_hi[:, b * tb:(b + 1) * tb], tri, preferred_element_type=F32)
           + jnp.dot(la_lo[:, b * tb:(b + 1) * tb], tri, preferred_element_type=F32) for b in range(nb)]
    r = jnp.dot(xb, w_main_ref[:, GLA_HK + GLA_HV:], preferred_element_type=F32)

    lane = lax.broadcasted_iota(jnp.int32, (GLA_HK, 2 * CHUNK), 1)
    halves = (lane < CHUNK, lane >= CHUNK)
    heads = [(slice(h * GLA_DK, (h + 1) * GLA_DK), slice(h * GLA_DV, (h + 1) * GLA_DV)) for h in range(GLA_HEADS)]
    for pair in range(tb // (2 * CHUNK)):
        c0 = pair * 2 * CHUNK
        dec, upd = {}, {}
        for b in range(nb):
            t0 = b * tb + c0
            cum_p = cum[b][:, c0:c0 + 2 * CHUNK]
            last = (cum_p[:, CHUNK - 1:CHUNK], cum_p[:, 2 * CHUNK - 1:2 * CHUNK])
            kdec = kt[:, t0:t0 + 2 * CHUNK] * jnp.exp(jnp.where(halves[0], last[0], last[1]) - cum_p)
            v_pair = hv[t0:t0 + 2 * CHUNK].astype(BF16)
            for half in range(2):
                kd = jnp.where(halves[half], kdec, 0.0).astype(BF16)
                dec[b, half] = jnp.exp(last[half])
                for h, (ks, vs) in enumerate(heads):
                    upd[b, half, h] = jnp.dot(kd[ks, :], v_pair[:, vs], preferred_element_type=F32)
        for half in range(2):
            for b in range(nb):
                r0 = b * tb + c0 + half * CHUNK
                qc = (hq[r0:r0 + CHUNK] * (GLA_DK ** -0.5)).astype(BF16)
                for h, (ks, vs) in enumerate(heads):
                    st = st_ref[b, h] * dec[b, half][ks, :] + upd[b, half, h]
                    st_ref[b, h] = st
                    o = jnp.dot(qc[:, ks], st.astype(BF16), preferred_element_type=F32)
                    o = o * lax.rsqrt(jnp.mean(o * o, -1, keepdims=True) + LN_EPS) * gn_ref[:, vs]
                    mix_ref[r0:r0 + CHUNK, vs] = o

    gated = mix_ref[...] * (r * _sigmoid(r))
    mix = jnp.dot(gated.astype(BF16), w_out_ref[...], preferred_element_type=F32)
    o_ref[...] = _layer_norm(DEEPNORM_ALPHA * x + mix, lng_ref[...], lnb_ref[...]).reshape(nb, tb, d)


def _gla_layer(h, w_in, w_gate2, b_gate, g_norm, w_out, ln_g, ln_b, *, tb=256):
    B, S, D = h.shape
    n_k = 2 * GLA_HK
    n_main = GLA_HK + 2 * GLA_HV
    n_all = 2 * GLA_HK + 2 * GLA_HV
    w_main = jnp.concatenate([w_in[:, :GLA_HK], w_in[:, n_k:n_all]], axis=1).astype(BF16)
    w_kt = w_in[:, GLA_HK:n_k].T.astype(BF16)
    w_lr = jnp.zeros((LANES, D), F32).at[:GLA_GATE_RANK].set(w_in[:, n_all:].T).astype(BF16)
    w_g2 = jnp.zeros((GLA_HK, LANES), F32).at[:, :GLA_GATE_RANK].set(w_gate2.T).astype(BF16)
    idx = jnp.arange(tb)
    tri = ((idx[:, None] <= idx[None, :]) & (idx[:, None] // CHUNK == idx[None, :] // CHUNK)).astype(BF16)
    return pl.pallas_call(
        _gla_kernel,
        grid=(S // tb,),
        in_specs=[
            pl.BlockSpec((B, tb, D), lambda j: (0, j, 0)),
            _full((D, n_main)), _full((GLA_HK, D)), _full((LANES, D)), _full((GLA_HK, LANES)),
            _full((GLA_HK, 1)), _full((1, GLA_HV)), _full((GLA_HV, D)), _full((1, D)), _full((1, D)),
            _full((tb, tb)),
        ],
        out_specs=pl.BlockSpec((B, tb, D), lambda j: (0, j, 0)),
        out_shape=jax.ShapeDtypeStruct((B, S, D), F32),
        scratch_shapes=[pltpu.VMEM((B, GLA_HEADS, GLA_DK, GLA_DV), F32), pltpu.VMEM((B * tb, GLA_HV), F32)],
        compiler_params=pltpu.CompilerParams(dimension_semantics=("arbitrary",),
                                             vmem_limit_bytes=VMEM_LIMIT),
        name="gla_layer",
    )(h, w_main, w_kt, w_lr, w_g2, b_gate.reshape(-1, 1), g_norm.reshape(1, -1), w_out.astype(BF16),
      ln_g.reshape(1, -1), ln_b.reshape(1, -1), tri)


def _first_max(vals):
    top = functools.reduce(jnp.maximum, vals)
    idx = jnp.full_like(top, float(len(vals) - 1))
    for j in range(len(vals) - 2, -1, -1):
        idx = jnp.where(vals[j] == top, float(j), idx)
    return top, idx


def _route(lt):
    gl = [lt[g:g + 1, :] for g in range(MOE_GROUPS)]
    gmax, gidx = _first_max(gl)
    gw = 1.0 / functools.reduce(jnp.add, [jnp.exp(v - gmax) for v in gl])
    el = []
    for j in range(MOE_EPG):
        rows = [lt[MOE_GROUPS + g * MOE_EPG + j:MOE_GROUPS + g * MOE_EPG + j + 1, :] for g in range(MOE_GROUPS)]
        sel = rows[MOE_GROUPS - 1]
        for g in range(MOE_GROUPS - 2, -1, -1):
            sel = jnp.where(gidx == float(g), rows[g], sel)
        el.append(sel)
    emax = functools.reduce(jnp.maximum, el)
    pe = [jnp.exp(v - emax) for v in el]
    psum = functools.reduce(jnp.add, pe)
    prob = [v / psum for v in pe]
    p1, i1 = _first_max(prob)
    p2, i2 = _first_max([jnp.where(i1 == float(j), -1.0, prob[j]) for j in range(MOE_EPG)])
    den = p1 + p2
    weights = [(jnp.where(i1 == float(j), p1 / den, 0.0) + jnp.where(i2 == float(j), p2 / den, 0.0)) * gw
               for j in range(MOE_EPG)]
    return gidx, weights


def _split3(v):
    hi = v.astype(BF16).astype(F32)
    mid = (v - hi).astype(BF16).astype(F32)
    lo = (v - hi - mid).astype(BF16).astype(F32)
    return hi, mid, lo


def _router_logits(x, wr_ref, br_ref):
    tb = x.shape[0]
    xh = x.astype(BF16)
    xl = (x - xh.astype(F32)).astype(BF16)
    x2 = jnp.concatenate([xh, xl], axis=1)
    half = tb // 2
    both = jnp.concatenate([jnp.dot(x2[:half], wr_ref[...], preferred_element_type=F32),
                            jnp.dot(x2[half:], wr_ref[...], preferred_element_type=F32)], axis=0)
    return both[:, :LANES] + both[:, LANES:] + br_ref[...]


def _moe_route(logits, tr_ref, posr_ref, cwn_ref, meta_ref, meta_base):
    tb = logits.shape[0]
    gidx, weights = _route(logits.T)
    onehot = [jnp.where(gidx == float(g), 1.0, 0.0) for g in range(MOE_GROUPS)]
    oht = jnp.concatenate(onehot + [jnp.zeros((MOE_TR_ROWS - MOE_GROUPS, tb), F32)], axis=0)
    ohb = oht.astype(BF16)
    earlier = jnp.where(lax.broadcasted_iota(jnp.int32, (LANES, LANES), 0)
                        < lax.broadcasted_iota(jnp.int32, (LANES, LANES), 1), 1.0, 0.0).astype(BF16)
    before, carry = [], jnp.zeros((MOE_TR_ROWS, 1), F32)
    for piece in range(tb // LANES):
        cols = slice(piece * LANES, (piece + 1) * LANES)
        before.append(jnp.dot(ohb[:, cols], earlier, preferred_element_type=F32) + carry)
        carry = carry + jnp.sum(oht[:, cols], -1, keepdims=True)
    rank = jnp.sum(jnp.concatenate(before, axis=1) * oht, 0, keepdims=True)
    ntile = jnp.floor((carry + (MOE_SUB - 1.0)) * (1.0 / MOE_SUB))
    pos, first_row = rank, jnp.zeros((1, 1), F32)
    for g in range(MOE_GROUPS):
        meta_ref[meta_base + g] = jnp.sum(ntile[g:g + 1, :]).astype(jnp.int32)
        meta_ref[meta_base + MOE_GROUPS + g] = jnp.sum(first_row).astype(jnp.int32)
        pos = pos + onehot[g] * first_row
        first_row = first_row + ntile[g:g + 1, :] * MOE_SUB
    posr_ref[0:1, :] = pos
    for j in range(MOE_EPG):
        for k, part in enumerate(_split3(weights[j])):
            tr_ref[j + k * MOE_EPG:j + k * MOE_EPG + 1, :] = part
    tr_ref[MOE_POS_LANE:MOE_POS_LANE + 1, :] = pos
    tr_ref[MOE_POS_LANE + 1:MOE_TR_ROWS, :] = jnp.zeros((MOE_TR_ROWS - MOE_POS_LANE - 1, tb), F32)
    cwn_ref[...] = tr_ref[...].T


def _moe_kernel(x_ref, wr_ref, br_ref, wg_ref, wu_ref, wd_ref, lng_ref, lnb_ref, o_ref,
                xs_ref, cw_ref, pt_ref, y_ref, res_ref, tr_ref, posr_ref, cwn_ref, meta_ref):
    blk_i = pl.program_id(0)
    step = pl.program_id(1)
    steps_per_group = MOE_EPG // MOE_EPS
    grp = step // steps_per_group
    j0 = (step % steps_per_group) * MOE_EPS
    tb = x_ref.shape[0]
    tbs = xs_ref.shape[0]
    d = x_ref.shape[1]
    meta_cur = (blk_i % 2) * (2 * MOE_GROUPS)
    meta_next = 2 * MOE_GROUPS - meta_cur

    @pl.when(step == 0)
    def _():
        @pl.when(blk_i == 0)
        def _():
            tr_ref[...] = jnp.zeros_like(tr_ref)
            _moe_route(_router_logits(x_ref[...], wr_ref, br_ref), tr_ref, posr_ref, cwn_ref, meta_ref, meta_cur)

        x = x_ref[...]
        res_ref[...] = x
        cwn = cwn_ref[...]
        pos_col = cwn[:, MOE_POS_LANE:MOE_POS_LANE + 1].astype(jnp.int32)
        pt_ref[...] = jnp.where(lax.broadcasted_iota(jnp.int32, (tb, tbs), 1) == pos_col, 1.0, 0.0).astype(BF16)
        pos_row = posr_ref[0:1, :].astype(jnp.int32)
        perm = jnp.where(lax.broadcasted_iota(jnp.int32, (tbs, tb), 0) == pos_row, 1.0, 0.0).astype(BF16)
        sorted_all = jnp.dot(perm, jnp.concatenate([x.astype(BF16), cwn.astype(BF16)], axis=1),
                             preferred_element_type=F32)
        xs_ref[...] = sorted_all[:, :d].astype(BF16)
        cw_ref[...] = sorted_all[:, d:]
        y_ref[...] = jnp.zeros_like(y_ref)

    base = meta_ref[meta_cur + MOE_GROUPS + grp]

    def experts(start, nrows):
        rows = pl.ds(pl.multiple_of(start, MOE_SUB), nrows)
        xt = xs_ref[rows, :]
        cwt = cw_ref[rows, :]
        lane = lax.broadcasted_iota(jnp.int32, (nrows, LANES), 1)
        hidden = []
        for k in range(MOE_EPS):
            gt = jnp.dot(xt, wg_ref[k], preferred_element_type=F32)
            ut = jnp.dot(xt, wu_ref[k], preferred_element_type=F32)
            hidden.append(((gt * _sigmoid(gt)) * ut).astype(BF16))
        total = None
        for k in range(MOE_EPS):
            y = jnp.dot(hidden[k], wd_ref[k], preferred_element_type=F32)
            mine = (lane % MOE_EPG == j0 + k) & (lane < 3 * MOE_EPG)
            c = jnp.sum(jnp.where(mine, cwt, 0.0), -1, keepdims=True)
            total = c * y if total is None else total + c * y
        y_ref[rows, :] += total

    n_tiles = meta_ref[meta_cur + grp]

    def wide(i, carry):
        experts(base + i * (2 * MOE_SUB), 2 * MOE_SUB)
        return carry

    lax.fori_loop(0, n_tiles // 2, wide, 0)

    @pl.when(n_tiles % 2 == 1)
    def _():
        experts(base + (n_tiles - 1) * MOE_SUB, MOE_SUB)

    @pl.when(step == MOE_EXPERTS // MOE_EPS - 1)
    def _():
        logits = _router_logits(x_ref[...], wr_ref, br_ref)
        yb = y_ref[...].astype(BF16)
        npiece = 4
        piece = tb // npiece
        for c in range(npiece):
            if c == npiece - 1:
                _moe_route(logits, tr_ref, posr_ref, cwn_ref, meta_ref, meta_next)
            rows = slice(c * piece, (c + 1) * piece)
            ffn = jnp.dot(pt_ref[rows, :], yb, preferred_element_type=F32)
            o_ref[rows, :] = _layer_norm(DEEPNORM_ALPHA * res_ref[rows, :] + ffn, lng_ref[...], lnb_ref[...])


def _moe_layer(h, w_group, b_group, w_router, b_router, w_gate, w_up, w_down, ln_g, ln_b, *, tb=1024):
    T, D = h.shape
    wr = jnp.zeros((D, LANES), F32).at[:, :MOE_GROUPS].set(w_group)
    wr = wr.at[:, MOE_GROUPS:MOE_GROUPS + MOE_EXPERTS].set(w_router)
    wrh = wr.astype(BF16)
    wrl = (wr - wrh.astype(F32)).astype(BF16)
    wr2 = jnp.concatenate([jnp.concatenate([wrh, wrl], axis=1),
                           jnp.concatenate([wrh, jnp.zeros_like(wrl)], axis=1)], axis=0)
    br = jnp.zeros((1, LANES), F32).at[0, :MOE_GROUPS].set(b_group)
    br = br.at[0, MOE_GROUPS:MOE_GROUPS + MOE_EXPERTS].set(b_router.reshape(-1))
    tbs = tb + MOE_GROUPS * MOE_SUB
    nblk = T // tb
    return pl.pallas_call(
        _moe_kernel,
        grid=(nblk, MOE_EXPERTS // MOE_EPS),
        in_specs=[
            pl.BlockSpec((tb, D), lambda i, e: (jnp.where(e == 0, i, jnp.minimum(i + 1, nblk - 1)), 0)),
            _full((2 * D, 2 * LANES)), _full((1, LANES)),
            pl.BlockSpec((MOE_EPS, D, MOE_FF), lambda i, e: (e, 0, 0)),
            pl.BlockSpec((MOE_EPS, D, MOE_FF), lambda i, e: (e, 0, 0)),
            pl.BlockSpec((MOE_EPS, MOE_FF, D), lambda i, e: (e, 0, 0)),
            _full((1, D)), _full((1, D)),
        ],
        out_specs=pl.BlockSpec((tb, D), lambda i, e: (i, 0)),
        out_shape=jax.ShapeDtypeStruct((T, D), F32),
        scratch_shapes=[pltpu.VMEM((tbs, D), BF16), pltpu.VMEM((tbs, LANES), F32), pltpu.VMEM((tb, tbs), BF16),
                        pltpu.VMEM((tbs, D), F32), pltpu.VMEM((tb, D), F32), pltpu.VMEM((LANES, tb), F32),
                        pltpu.VMEM((8, tb), F32), pltpu.VMEM((tb, LANES), F32),
                        pltpu.SMEM((4 * MOE_GROUPS,), jnp.int32)],
        compiler_params=pltpu.CompilerParams(dimension_semantics=("arbitrary", "arbitrary"),
                                             vmem_limit_bytes=MOE_VMEM_LIMIT),
        name="moe_layer",
    )(h, wr2, br, w_gate.astype(BF16), w_up.astype(BF16), w_down.astype(BF16),
      ln_g.reshape(1, -1), ln_b.reshape(1, -1))


def _kv_kernel(x_ref, wk_ref, wvt_ref, k_ref, vt_ref):
    xb = x_ref[0].astype(BF16)
    k_ref[0] = jnp.dot(xb, wk_ref[...], preferred_element_type=F32).astype(BF16)
    vt = lax.dot_general(wvt_ref[...], xb, (((1,), (1,)), ((), ())), preferred_element_type=F32)
    for j in range(vt_ref.shape[1]):
        vt_ref[0, j] = vt[:, j * ATT_BLK:(j + 1) * ATT_BLK].astype(BF16)


def _kv_proj(h, kv_w, *, ts=512):
    B, S, D = h.shape
    k1 = kv_w[:, :DIFF_QK].reshape(D, DIFF_HEADS, DIFF_DH)
    k2 = kv_w[:, DIFF_QK:2 * DIFF_QK].reshape(D, DIFF_HEADS, DIFF_DH)
    wk = jnp.stack([k1, k2], axis=2).reshape(D, 2 * DIFF_QK).astype(BF16)
    wvt = kv_w[:, 2 * DIFF_QK:].T.astype(BF16)
    nv = DIFF_HEADS * DIFF_DV
    nj = ts // ATT_BLK
    return pl.pallas_call(
        _kv_kernel,
        grid=(B, S // ts),
        in_specs=[pl.BlockSpec((1, ts, D), lambda b, j: (b, j, 0)), _full((D, 2 * DIFF_QK)), _full((nv, D))],
        out_specs=[pl.BlockSpec((1, ts, 2 * DIFF_QK), lambda b, j: (b, j, 0)),
                   pl.BlockSpec((1, nj, nv, ATT_BLK), lambda b, j: (b, j, 0, 0))],
        out_shape=[jax.ShapeDtypeStruct((B, S, 2 * DIFF_QK), BF16),
                   jax.ShapeDtypeStruct((B, S // ATT_BLK, nv, ATT_BLK), BF16)],
        compiler_params=pltpu.CompilerParams(dimension_semantics=("parallel", "parallel"),
                                             vmem_limit_bytes=VMEM_LIMIT),
        name="kv_proj",
    )(h, wk, wvt)


def _q_kernel(x_ref, wqt_ref, qt_ref):
    xb = x_ref[0].astype(BF16)
    qt = lax.dot_general(wqt_ref[...], xb, (((1,), (1,)), ((), ())), preferred_element_type=F32)
    qt = qt * (DIFF_DH ** -0.5 * LOG2E)
    for j in range(qt_ref.shape[1]):
        qt_ref[0, j] = qt[:, j * ATT_BLK:(j + 1) * ATT_BLK].astype(BF16)


def _q_proj(h, w_q, *, ts=512):
    B, S, D = h.shape
    q1 = w_q[:, :DIFF_QK].reshape(D, DIFF_HEADS, DIFF_DH)
    q2 = w_q[:, DIFF_QK:].reshape(D, DIFF_HEADS, DIFF_DH)
    wqt = jnp.stack([q1, q2], axis=2).reshape(D, 2 * DIFF_QK).T.astype(BF16)
    nq = 2 * DIFF_QK
    nj = ts // ATT_BLK
    return pl.pallas_call(
        _q_kernel,
        grid=(B, S // ts),
        in_specs=[pl.BlockSpec((1, ts, D), lambda b, j: (b, j, 0)), _full((nq, D))],
        out_specs=pl.BlockSpec((1, nj, nq, ATT_BLK), lambda b, j: (b, j, 0, 0)),
        out_shape=jax.ShapeDtypeStruct((B, S // ATT_BLK, nq, ATT_BLK), BF16),
        compiler_params=pltpu.CompilerParams(dimension_semantics=("parallel", "parallel"),
                                             vmem_limit_bytes=VMEM_LIMIT),
        name="q_proj",
    )(h, wqt)


def _bias_kernel(tab_ref, o_ref):
    h = pl.program_id(0)
    kk = lax.broadcasted_iota(jnp.int32, (ATT_BLK, ATT_BLK), 0)
    qq = lax.broadcasted_iota(jnp.int32, (ATT_BLK, ATT_BLK), 1)
    nb = REL_BUCKETS // 2
    far = tab_ref[nb - 1, h]
    for t in range(2):
        rel = kk - qq - t * ATT_BLK
        n = jnp.abs(rel)
        large = nb // 2
        for thr in (12, 16, 23, 32, 46, 64, 91):
            large = large + (n >= thr).astype(jnp.int32)
        bucket = jnp.where(rel > 0, nb, 0) + jnp.where(n < nb // 2, n, large)
        bias = jnp.zeros((ATT_BLK, ATT_BLK), F32)
        for b in range(REL_BUCKETS):
            bias = jnp.where(bucket == b, tab_ref[b, h], bias)
        bias = (bias - far) * LOG2E
        if t == 0:
            visible = (kk // CHUNK) <= (qq // CHUNK)
            bias = jnp.where(visible, bias, NEG_INF)
        o_ref[0, t] = bias


def _bias_tiles(rel_table):
    return pl.pallas_call(
        _bias_kernel,
        grid=(DIFF_HEADS,),
        in_specs=[pl.BlockSpec(memory_space=pltpu.SMEM)],
        out_specs=pl.BlockSpec((1, 2, ATT_BLK, ATT_BLK), lambda h: (h, 0, 0, 0)),
        out_shape=jax.ShapeDtypeStruct((DIFF_HEADS, 2, ATT_BLK, ATT_BLK), F32),
        name="bias_tiles",
    )(rel_table)


def _attn_kernel(lam_ref, qt_ref, k_ref, vt_ref, bias_ref, g_ref, o_ref, *scratch, lambda_init):
    blk = ATT_BLK
    qi = pl.program_id(2)
    n_far = jnp.maximum(qi - 1, 0)
    prev_block = jnp.maximum(qi - 1, 0)
    heads = [scratch[hh * 7:(hh + 1) * 7] for hh in range(ATT_HPS)]
    z = jnp.zeros((DIFF_DH, blk), BF16)
    ones_rows = jnp.ones((ATT_SUM_ROWS, blk), BF16)
    for hh, (w_ref, _, _, _, _, acc_ref, ml_ref) in enumerate(heads):
        qt = qt_ref[0, 0, hh * 2 * DIFF_DH:(hh + 1) * 2 * DIFF_DH, :]
        w_ref[...] = jnp.concatenate(
            [jnp.concatenate([qt[:DIFF_DH], z], 0), jnp.concatenate([z, qt[DIFF_DH:]], 0)], 1)
        ml_ref[0:1, :] = jnp.full((1, 2 * blk), NEG_INF, F32)
        acc_ref[...] = jnp.zeros_like(acc_ref)

    def scores(kb, slot, bias_idx=None, extra=None):
        for hh, (w_ref, s0_ref, s1_ref, t0_ref, t1_ref, _, _) in enumerate(heads):
            s_ref, t_ref = ((s0_ref, t0_ref), (s1_ref, t1_ref))[slot]
            kt = k_ref[0, pl.ds(pl.multiple_of(kb * blk, blk), blk), hh * LANES:(hh + 1) * LANES]
            s = jnp.dot(kt, w_ref[...], preferred_element_type=F32)
            if bias_idx is not None:
                bias = bias_ref[hh, bias_idx]
                s = s + jnp.concatenate([bias, bias], axis=1)
            if extra is not None:
                s = s + extra
            s_ref[...] = s
            t_ref[0:1, :] = jnp.max(s, 0, keepdims=True)

    def accumulate(kb, slot, then=None):
        for hh, (_, s0_ref, s1_ref, t0_ref, t1_ref, acc_ref, ml_ref) in enumerate(heads):
            s_ref, t_ref = ((s0_ref, t0_ref), (s1_ref, t1_ref))[slot]
            m_old = ml_ref[0:1, :]
            m_new = jnp.maximum(m_old, t_ref[0:1, :])
            a = jnp.exp2(m_old - m_new)
            p = jnp.exp2(s_ref[...] - m_new)
            ml_ref[0:1, :] = m_new
            pb = p.astype(BF16)
            vt = jnp.concatenate([vt_ref[0, kb, hh * DIFF_DV:(hh + 1) * DIFF_DV, :], ones_rows], axis=0)
            for i in range(2):
                cols = slice(i * blk, (i + 1) * blk)
                acc_ref[i] = a[:, cols] * acc_ref[i] + jnp.dot(vt, pb[:, cols], preferred_element_type=F32)
            if then is not None:
                then(hh)

    def fused(kb_a, slot_a, kb_b, slot_b, bias_idx=None, extra=None):
        nslab = 4
        rows = blk // nslab
        width = 2 * blk // nslab
        for hh, (w_ref, s0_ref, s1_ref, t0_ref, t1_ref, acc_ref, ml_ref) in enumerate(heads):
            sa_ref, ta_ref = ((s0_ref, t0_ref), (s1_ref, t1_ref))[slot_a]
            sb_ref, tb_ref = ((s0_ref, t0_ref), (s1_ref, t1_ref))[slot_b]
            m_old = ml_ref[0:1, :]
            m_new = jnp.maximum(m_old, tb_ref[0:1, :])
            a = jnp.exp2(m_old - m_new)
            ml_ref[0:1, :] = m_new
            vt = jnp.concatenate([vt_ref[0, kb_b, hh * DIFF_DV:(hh + 1) * DIFF_DV, :], ones_rows], axis=0)
            tmax = None
            for j in range(nslab):
                cols = slice(j * width, (j + 1) * width)
                p = jnp.exp2(sb_ref[:, cols] - m_new[:, cols])
                i, c0 = divmod(j * width, blk)
                acc_ref[i, :, c0:c0 + width] = (a[:, cols] * acc_ref[i, :, c0:c0 + width]
                                                + jnp.dot(vt, p.astype(BF16), preferred_element_type=F32))
                kt = k_ref[0, pl.ds(pl.multiple_of(kb_a * blk + j * rows, rows), rows),
                           hh * LANES:(hh + 1) * LANES]
                s = jnp.dot(kt, w_ref[...], preferred_element_type=F32)
                if bias_idx is not None:
                    bias = bias_ref[hh, bias_idx, j * rows:(j + 1) * rows, :]
                    s = s + jnp.concatenate([bias, bias], axis=1)
                if extra is not None:
                    s = s + extra
                sa_ref[j * rows:(j + 1) * rows, :] = s
                smax = jnp.max(s, 0, keepdims=True)
                tmax = smax if tmax is None else jnp.maximum(tmax, smax)
            ta_ref[0:1, :] = tmax

    scores(qi, 0, bias_idx=0)
    fused(prev_block, 1, qi, 0, bias_idx=1, extra=jnp.where(qi >= 1, 0.0, -jnp.inf).astype(F32))

    def pair(j):
        kb = 2 * j
        fused(kb, 0, jnp.where(j == 0, prev_block, kb - 1), 1)
        fused(kb + 1, 1, kb, 0)

    def two_pairs(i, carry):
        pair(2 * i)
        pair(2 * i + 1)
        return carry

    n_pairs = n_far // 2
    lax.fori_loop(0, n_pairs // 2, two_pairs, 0)

    @pl.when(n_pairs % 2 == 1)
    def _():
        pair(n_pairs - 1)

    last_slot1 = jnp.where(n_pairs == 0, prev_block, 2 * n_pairs - 1)

    def finish(hh):
        acc_ref = heads[hh][5]
        lam = (jnp.exp(jnp.sum(lam_ref[0:1, :] * lam_ref[1:2, :], keepdims=True))
               - jnp.exp(jnp.sum(lam_ref[2:3, :] * lam_ref[3:4, :], keepdims=True)) + lambda_init)
        o1, o2 = acc_ref[0], acc_ref[1]
        ot = (o1[:DIFF_DV] / o1[DIFF_DV:DIFF_DV + 1] - lam * (o2[:DIFF_DV] / o2[DIFF_DV:DIFF_DV + 1]))
        ot = ot * lax.rsqrt(jnp.mean(ot * ot, 0, keepdims=True) + LN_EPS) * g_ref[...] * (1.0 - lambda_init)
        o_ref[0, :, hh * DIFF_DV:(hh + 1) * DIFF_DV] = ot.T.astype(BF16)

    @pl.when(n_far % 2 == 1)
    def _():
        fused(n_far - 1, 0, last_slot1, 1)
        accumulate(n_far - 1, 0, then=finish)

    @pl.when(n_far % 2 == 0)
    def _():
        accumulate(last_slot1, 1, then=finish)


def _diff_attention(qt, k12, vt, bias, lam4, g_sub, lambda_init):
    B, nqb, _, _ = qt.shape
    S = k12.shape[1]
    kernel = functools.partial(_attn_kernel, lambda_init=lambda_init)
    once = pl.Buffered(1)
    head_scratch = [pltpu.VMEM((2 * DIFF_DH, 2 * ATT_BLK), BF16),
                    pltpu.VMEM((ATT_BLK, 2 * ATT_BLK), F32), pltpu.VMEM((ATT_BLK, 2 * ATT_BLK), F32),
                    pltpu.VMEM((8, 2 * ATT_BLK), F32), pltpu.VMEM((8, 2 * ATT_BLK), F32),
                    pltpu.VMEM((2, DIFF_DV + ATT_SUM_ROWS, ATT_BLK), F32), pltpu.VMEM((8, 2 * ATT_BLK), F32)]
    return pl.pallas_call(
        kernel,
        grid=(B, DIFF_HEADS // ATT_HPS, nqb),
        in_specs=[
            _full((4, DIFF_DH)),
            pl.BlockSpec((1, 1, ATT_HPS * 2 * DIFF_DH, ATT_BLK), lambda b, h, i: (b, i, h, 0)),
            pl.BlockSpec((1, S, ATT_HPS * 2 * DIFF_DH), lambda b, h, i: (b, 0, h), pipeline_mode=once),
            pl.BlockSpec((1, nqb, ATT_HPS * DIFF_DV, ATT_BLK), lambda b, h, i: (b, 0, h, 0),
                         pipeline_mode=once),
            pl.BlockSpec((ATT_HPS, 2, ATT_BLK, ATT_BLK), lambda b, h, i: (h, 0, 0, 0), pipeline_mode=once),
            _full((DIFF_DV, 1)),
        ],
        out_specs=pl.BlockSpec((1, ATT_BLK, ATT_HPS * DIFF_DV), lambda b, h, i: (b, i, h)),
        out_shape=jax.ShapeDtypeStruct((B, S, DIFF_HEADS * DIFF_DV), BF16),
        scratch_shapes=head_scratch * ATT_HPS,
        compiler_params=pltpu.CompilerParams(dimension_semantics=("parallel", "parallel", "arbitrary"),
                                             vmem_limit_bytes=VMEM_LIMIT),
        name="diff_attn",
    )(lam4, qt, k12, vt, bias, g_sub.reshape(-1, 1))


def _proj_ln_kernel(h_ref, a_ref, w_ref, lng_ref, lnb_ref, o_ref):
    mix = jnp.dot(a_ref[...], w_ref[...], preferred_element_type=F32)
    o_ref[...] = _layer_norm(DEEPNORM_ALPHA * h_ref[...] + mix, lng_ref[...], lnb_ref[...])


def _proj_ln(h, a, w_out, ln_g, ln_b, *, tb=512):
    T, D = h.shape
    return pl.pallas_call(
        _proj_ln_kernel,
        grid=(T // tb,),
        in_specs=[pl.BlockSpec((tb, D), lambda i: (i, 0)), pl.BlockSpec((tb, a.shape[1]), lambda i: (i, 0)),
                  _full(w_out.shape), _full((1, D)), _full((1, D))],
        out_specs=pl.BlockSpec((tb, D), lambda i: (i, 0)),
        out_shape=jax.ShapeDtypeStruct((T, D), F32),
        compiler_params=pltpu.CompilerParams(dimension_semantics=("parallel",), vmem_limit_bytes=VMEM_LIMIT),
        name="attn_out_ln",
    )(h, a, w_out.astype(BF16), ln_g.reshape(1, -1), ln_b.reshape(1, -1))


def kernel(x, a_w_in, a_w_gate2, a_b_gate, a_g_norm, a_w_out, kv_w, b_w_q, b_lam_q1, b_lam_k1, b_lam_q2, b_lam_k2, b_g_sub, b_w_out, rel_table, moe_w_group, moe_b_group, moe_w_router, moe_b_router, moe_w_gate, moe_w_up, moe_w_down, ln_g, ln_b):
    B, S, D = x.shape
    h = x
    bias = None
    k12 = vt = None
    for layer in range(DEPTH):
        if layer < N_A_LAYERS:
            h = _gla_layer(h, a_w_in[layer], a_w_gate2[layer], a_b_gate[layer], a_g_norm[layer],
                           a_w_out[layer], ln_g[layer, 0], ln_b[layer, 0])
        else:
            j = layer - N_A_LAYERS
            lambda_init = 0.8 - 0.6 * math.exp(-0.3 * layer)
            if bias is None:
                bias = _bias_tiles(rel_table)
            qt = _q_proj(h, b_w_q[j])
            lam4 = jnp.stack([b_lam_q1[j], b_lam_k1[j], b_lam_q2[j], b_lam_k2[j]])
            a = _diff_attention(qt, k12, vt, bias, lam4, b_g_sub[j], lambda_init)
            h = _proj_ln(h.reshape(B * S, D), a.reshape(B * S, -1), b_w_out[j], ln_g[layer, 0],
                         ln_b[layer, 0]).reshape(B, S, D)
        h = _moe_layer(h.reshape(B * S, D), moe_w_group[layer], moe_b_group[layer], moe_w_router[layer],
                       moe_b_router[layer], moe_w_gate[layer], moe_w_up[layer], moe_w_down[layer],
                       ln_g[layer, 1], ln_b[layer, 1]).reshape(B, S, D)
        if layer == N_A_LAYERS - 1:
            k12, vt = _kv_proj(h, kv_w)
    return h
```

```python
import functools
import math

import jax
import jax.numpy as jnp
from jax import lax
from jax.experimental import pallas as pl
from jax.experimental.pallas import tpu as pltpu

F32 = jnp.float32
BF16 = jnp.bfloat16

D_MODEL = 1024
DEPTH = 4
CHUNK = 64
N_A_LAYERS = 2

GLA_HEADS = 4
GLA_DK = 128
GLA_DV = 256
GLA_GATE_RANK = 16
GLA_GATE_TAU = 16.0
GLA_HK = GLA_HEADS * GLA_DK
GLA_HV = GLA_HEADS * GLA_DV

DIFF_HEADS = 8
DIFF_DH = 64
DIFF_DV = 128
DIFF_QK = DIFF_HEADS * DIFF_DH

REL_BUCKETS = 32

MOE_GROUPS = 4
MOE_EPG = 4
MOE_EXPERTS = 16
MOE_FF = 512

DEEPNORM_ALPHA = (2.0 * DEPTH) ** 0.25
LN_EPS = 1e-5
NEG_INF = -1e30

LANES = 128
ATT_BLK = 512
ATT_HPS = 2
ATT_SUM_ROWS = 16
LOG2E = math.log2(math.e)
VMEM_LIMIT = 48 * 1024 * 1024
MOE_VMEM_LIMIT = 58 * 1024 * 1024
MOE_SUB = 128
MOE_EPS = 2
MOE_POS_LANE = 12
MOE_TR_ROWS = 16


def _layer_norm(y, g, b):
    mu = jnp.mean(y, -1, keepdims=True)
    d = y - mu
    var = jnp.mean(d * d, -1, keepdims=True)
    return d * lax.rsqrt(var + LN_EPS) * g + b


def _sigmoid(x):
    return 1.0 / (1.0 + jnp.exp(-x))


def _full(shape):
    return pl.BlockSpec(shape, lambda *_: (0,) * len(shape))


def _gla_kernel(x_ref, w_main_ref, w_kt_ref, w_lr_ref, w_g2_ref, b_g_ref, gn_ref, w_out_ref, lng_ref, lnb_ref,
                tri_ref, o_ref, st_ref, mix_ref):
    nb, tb, d = x_ref.shape
    nt = (((1,), (1,)), ((), ()))

    @pl.when(pl.program_id(0) == 0)
    def _():
        st_ref[...] = jnp.zeros_like(st_ref)

    x = x_ref[...].reshape(nb * tb, d)
    xb = x.astype(BF16)
    lrt = lax.dot_general(w_lr_ref[...], xb, nt, preferred_element_type=F32)
    kt = lax.dot_general(w_kt_ref[...], xb, nt, preferred_element_type=F32)
    hq = jnp.dot(xb, w_main_ref[:, :GLA_HK], preferred_element_type=F32)
    glog = jnp.dot(w_g2_ref[...], lrt.astype(BF16), preferred_element_type=F32) + b_g_ref[...]
    hv = jnp.dot(xb, w_main_ref[:, GLA_HK:GLA_HK + GLA_HV], preferred_element_type=F32)
    la = (jnp.minimum(glog, 0.0) - jnp.log(1.0 + jnp.exp(-jnp.abs(glog)))) * (1.0 / GLA_GATE_TAU)
    la_hi = la.astype(BF16)
    la_lo = (la - la_hi.astype(F32)).astype(BF16)
    tri = tri_ref[...]
    cum = [jnp.dot(la_hi[:, b * tb:(b + 1) * tb], tri, preferred_element_type=F32)
           + jnp.dot(la_lo[:, b * tb:(b + 1) * tb], tri, preferred_element_type=F32) for b in range(nb)]
    r = jnp.dot(xb, w_main_ref[:, GLA_HK + GLA_HV:], preferred_element_type=F32)

    lane = lax.broadcasted_iota(jnp.int32, (GLA_HK, 2 * CHUNK), 1)
    halves = (lane < CHUNK, lane >= CHUNK)
    heads = [(slice(h * GLA_DK, (h + 1) * GLA_DK), slice(h * GLA_DV, (h + 1) * GLA_DV)) for h in range(GLA_HEADS)]
    for pair in range(tb // (2 * CHUNK)):
        c0 = pair * 2 * CHUNK
        dec, upd = {}, {}
        for b in range(nb):
            t0 = b * tb + c0
            cum_p = cum[b][:, c0:c0 + 2 * CHUNK]
            last = (cum_p[:, CHUNK - 1:CHUNK], cum_p[:, 2 * CHUNK - 1:2 * CHUNK])
            kdec = kt[:, t0:t0 + 2 * CHUNK] * jnp.exp(jnp.where(halves[0], last[0], last[1]) - cum_p)
            v_pair = hv[t0:t0 + 2 * CHUNK].astype(BF16)
            for half in range(2):
                kd = jnp.where(halves[half], kdec, 0.0).astype(BF16)
                dec[b, half] = jnp.exp(last[half])
                for h, (ks, vs) in enumerate(heads):
                    upd[b, half, h] = jnp.dot(kd[ks, :], v_pair[:, vs], preferred_element_type=F32)
        for half in range(2):
            for b in range(nb):
                r0 = b * tb + c0 + half * CHUNK
                qc = (hq[r0:r0 + CHUNK] * (GLA_DK ** -0.5)).astype(BF16)
                for h, (ks, vs) in enumerate(heads):
                    st = st_ref[b, h] * dec[b, half][ks, :] + upd[b, half, h]
                    st_ref[b, h] = st
                    o = jnp.dot(qc[:, ks], st.astype(BF16), preferred_element_type=F32)
                    o = o * lax.rsqrt(jnp.mean(o * o, -1, keepdims=True) + LN_EPS) * gn_ref[:, vs]
                    mix_ref[r0:r0 + CHUNK, vs] = o

    gated = mix_ref[...] * (r * _sigmoid(r))
    mix = jnp.dot(gated.astype(BF16), w_out_ref[...], preferred_element_type=F32)
    o_ref[...] = _layer_norm(DEEPNORM_ALPHA * x + mix, lng_ref[...], lnb_ref[...]).reshape(nb, tb, d)


def _gla_layer(h, w_in, w_gate2, b_gate, g_norm, w_out, ln_g, ln_b, *, tb=256):
    B, S, D = h.shape
    n_k = 2 * GLA_HK
    n_main = GLA_HK + 2 * GLA_HV
    n_all = 2 * GLA_HK + 2 * GLA_HV
    w_main = jnp.concatenate([w_in[:, :GLA_HK], w_in[:, n_k:n_all]], axis=1).astype(BF16)
    w_kt = w_in[:, GLA_HK:n_k].T.astype(BF16)
    w_lr = jnp.zeros((LANES, D), F32).at[:GLA_GATE_RANK].set(w_in[:, n_all:].T).astype(BF16)
    w_g2 = jnp.zeros((GLA_HK, LANES), F32).at[:, :GLA_GATE_RANK].set(w_gate2.T).astype(BF16)
    idx = jnp.arange(tb)
    tri = ((idx[:, None] <= idx[None, :]) & (idx[:, None] // CHUNK == idx[None, :] // CHUNK)).astype(BF16)
    return pl.pallas_call(
        _gla_kernel,
        grid=(S // tb,),
        in_specs=[
            pl.BlockSpec((B, tb, D), lambda j: (0, j, 0)),
            _full((D, n_main)), _full((GLA_HK, D)), _full((LANES, D)), _full((GLA_HK, LANES)),
            _full((GLA_HK, 1)), _full((1, GLA_HV)), _full((GLA_HV, D)), _full((1, D)), _full((1, D)),
            _full((tb, tb)),
        ],
        out_specs=pl.BlockSpec((B, tb, D), lambda j: (0, j, 0)),
        out_shape=jax.ShapeDtypeStruct((B, S, D), F32),
        scratch_shapes=[pltpu.VMEM((B, GLA_HEADS, GLA_DK, GLA_DV), F32), pltpu.VMEM((B * tb, GLA_HV), F32)],
        compiler_params=pltpu.CompilerParams(dimension_semantics=("arbitrary",),
                                             vmem_limit_bytes=VMEM_LIMIT),
        name="gla_layer",
    )(h, w_main, w_kt, w_lr, w_g2, b_gate.reshape(-1, 1), g_norm.reshape(1, -1), w_out.astype(BF16),
      ln_g.reshape(1, -1), ln_b.reshape(1, -1), tri)


def _first_max(vals):
    top = functools.reduce(jnp.maximum, vals)
    idx = jnp.full_like(top, float(len(vals) - 1))
    for j in range(len(vals) - 2, -1, -1):
        idx = jnp.where(vals[j] == top, float(j), idx)
    return top, idx


def _route(lt):
    gl = [lt[g:g + 1, :] for g in range(MOE_GROUPS)]
    gmax, gidx = _first_max(gl)
    gw = 1.0 / functools.reduce(jnp.add, [jnp.exp(v - gmax) for v in gl])
    el = []
    for j in range(MOE_EPG):
        rows = [lt[MOE_GROUPS + g * MOE_EPG + j:MOE_GROUPS + g * MOE_EPG + j + 1, :] for g in range(MOE_GROUPS)]
        sel = rows[MOE_GROUPS - 1]
        for g in range(MOE_GROUPS - 2, -1, -1):
            sel = jnp.where(gidx == float(g), rows[g], sel)
        el.append(sel)
    emax = functools.reduce(jnp.maximum, el)
    pe = [jnp.exp(v - emax) for v in el]
    psum = functools.reduce(jnp.add, pe)
    prob = [v / psum for v in pe]
    p1, i1 = _first_max(prob)
    p2, i2 = _first_max([jnp.where(i1 == float(j), -1.0, prob[j]) for j in range(MOE_EPG)])
    den = p1 + p2
    weights = [(jnp.where(i1 == float(j), p1 / den, 0.0) + jnp.where(i2 == float(j), p2 / den, 0.0)) * gw
               for j in range(MOE_EPG)]
    return gidx, weights


def _split3(v):
    hi = v.astype(BF16).astype(F32)
    mid = (v - hi).astype(BF16).astype(F32)
    lo = (v - hi - mid).astype(BF16).astype(F32)
    return hi, mid, lo


def _router_logits(x, wr_ref, br_ref):
    tb = x.shape[0]
    xh = x.astype(BF16)
    xl = (x - xh.astype(F32)).astype(BF16)
    x2 = jnp.concatenate([xh, xl], axis=1)
    half = tb // 2
    both = jnp.concatenate([jnp.dot(x2[:half], wr_ref[...], preferred_element_type=F32),
                            jnp.dot(x2[half:], wr_ref[...], preferred_element_type=F32)], axis=0)
    return both[:, :LANES] + both[:, LANES:] + br_ref[...]


def _moe_route(logits, tr_ref, posr_ref, cwn_ref, meta_ref, meta_base):
    tb = logits.shape[0]
    gidx, weights = _route(logits.T)
    onehot = [jnp.where(gidx == float(g), 1.0, 0.0) for g in range(MOE_GROUPS)]
    oht = jnp.concatenate(onehot + [jnp.zeros((MOE_TR_ROWS - MOE_GROUPS, tb), F32)], axis=0)
    ohb = oht.astype(BF16)
    earlier = jnp.where(lax.broadcasted_iota(jnp.int32, (LANES, LANES), 0)
                        < lax.broadcasted_iota(jnp.int32, (LANES, LANES), 1), 1.0, 0.0).astype(BF16)
    before, carry = [], jnp.zeros((MOE_TR_ROWS, 1), F32)
    for piece in range(tb // LANES):
        cols = slice(piece * LANES, (piece + 1) * LANES)
        before.append(jnp.dot(ohb[:, cols], earlier, preferred_element_type=F32) + carry)
        carry = carry + jnp.sum(oht[:, cols], -1, keepdims=True)
    rank = jnp.sum(jnp.concatenate(before, axis=1) * oht, 0, keepdims=True)
    ntile = jnp.floor((carry + (MOE_SUB - 1.0)) * (1.0 / MOE_SUB))
    pos, first_row = rank, jnp.zeros((1, 1), F32)
    for g in range(MOE_GROUPS):
        meta_ref[meta_base + g] = jnp.sum(ntile[g:g + 1, :]).astype(jnp.int32)
        meta_ref[meta_base + MOE_GROUPS + g] = jnp.sum(first_row).astype(jnp.int32)
        pos = pos + onehot[g] * first_row
        first_row = first_row + ntile[g:g + 1, :] * MOE_SUB
    posr_ref[0:1, :] = pos
    for j in range(MOE_EPG):
        for k, part in enumerate(_split3(weights[j])):
            tr_ref[j + k * MOE_EPG:j + k * MOE_EPG + 1, :] = part
    tr_ref[MOE_POS_LANE:MOE_POS_LANE + 1, :] = pos
    tr_ref[MOE_POS_LANE + 1:MOE_TR_ROWS, :] = jnp.zeros((MOE_TR_ROWS - MOE_POS_LANE - 1, tb), F32)
    cwn_ref[...] = tr_ref[...].T


def _moe_kernel(x_ref, wr_ref, br_ref, wg_ref, wu_ref, wd_ref, lng_ref, lnb_ref, o_ref,
                xs_ref, cw_ref, pt_ref, y_ref, res_ref, tr_ref, posr_ref, cwn_ref, meta_ref):
    blk_i = pl.program_id(0)
    step = pl.program_id(1)
    steps_per_group = MOE_EPG // MOE_EPS
    grp = step // steps_per_group
    j0 = (step % steps_per_group) * MOE_EPS
    tb = x_ref.shape[0]
    tbs = xs_ref.shape[0]
    d = x_ref.shape[1]
    meta_cur = (blk_i % 2) * (2 * MOE_GROUPS)
    meta_next = 2 * MOE_GROUPS - meta_cur

    @pl.when(step == 0)
    def _():
        @pl.when(blk_i == 0)
        def _():
            tr_ref[...] = jnp.zeros_like(tr_ref)
            _moe_route(_router_logits(x_ref[...], wr_ref, br_ref), tr_ref, posr_ref, cwn_ref, meta_ref, meta_cur)

        x = x_ref[...]
        res_ref[...] = x
        cwn = cwn_ref[...]
        pos_col = cwn[:, MOE_POS_LANE:MOE_POS_LANE + 1].astype(jnp.int32)
        pt_ref[...] = jnp.where(lax.broadcasted_iota(jnp.int32, (tb, tbs), 1) == pos_col, 1.0, 0.0).astype(BF16)
        pos_row = posr_ref[0:1, :].astype(jnp.int32)
        perm = jnp.where(lax.broadcasted_iota(jnp.int32, (tbs, tb), 0) == pos_row, 1.0, 0.0).astype(BF16)
        sorted_all = jnp.dot(perm, jnp.concatenate([x.astype(BF16), cwn.astype(BF16)], axis=1),
                             preferred_element_type=F32)
        xs_ref[...] = sorted_all[:, :d].astype(BF16)
        cw_ref[...] = sorted_all[:, d:]
        y_ref[...] = jnp.zeros_like(y_ref)

    base = meta_ref[meta_cur + MOE_GROUPS + grp]

    def experts(start, nrows):
        rows = pl.ds(pl.multiple_of(start, MOE_SUB), nrows)
        xt = xs_ref[rows, :]
        cwt = cw_ref[rows, :]
        lane = lax.broadcasted_iota(jnp.int32, (nrows, LANES), 1)
        hidden = []
        for k in range(MOE_EPS):
            gt = jnp.dot(xt, wg_ref[k], preferred_element_type=F32)
            ut = jnp.dot(xt, wu_ref[k], preferred_element_type=F32)
            hidden.append(((gt * _sigmoid(gt)) * ut).astype(BF16))
        total = None
        for k in range(MOE_EPS):
            y = jnp.dot(hidden[k], wd_ref[k], preferred_element_type=F32)
            mine = (lane % MOE_EPG == j0 + k) & (lane < 3 * MOE_EPG)
            c = jnp.sum(jnp.where(mine, cwt, 0.0), -1, keepdims=True)
            total = c * y if total is None else total + c * y
        y_ref[rows, :] += total

    n_tiles = meta_ref[meta_cur + grp]

    def wide(i, carry):
        experts(base + i * (2 * MOE_SUB), 2 * MOE_SUB)
        return carry

    lax.fori_loop(0, n_tiles // 2, wide, 0)

    @pl.when(n_tiles % 2 == 1)
    def _():
        experts(base + (n_tiles - 1) * MOE_SUB, MOE_SUB)

    @pl.when(step == MOE_EXPERTS // MOE_EPS - 1)
    def _():
        logits = _router_logits(x_ref[...], wr_ref, br_ref)
        yb = y_ref[...].astype(BF16)
        npiece = 4
        piece = tb // npiece
        for c in range(npiece):
            if c == npiece - 1:
                _moe_route(logits, tr_ref, posr_ref, cwn_ref, meta_ref, meta_next)
            rows = slice(c * piece, (c + 1) * piece)
            ffn = jnp.dot(pt_ref[rows, :], yb, preferred_element_type=F32)
            o_ref[rows, :] = _layer_norm(DEEPNORM_ALPHA * res_ref[rows, :] + ffn, lng_ref[...], lnb_ref[...])


def _moe_layer(h, w_group, b_group, w_router, b_router, w_gate, w_up, w_down, ln_g, ln_b, *, tb=1024):
    T, D = h.shape
    wr = jnp.zeros((D, LANES), F32).at[:, :MOE_GROUPS].set(w_group)
    wr = wr.at[:, MOE_GROUPS:MOE_GROUPS + MOE_EXPERTS].set(w_router)
    wrh = wr.astype(BF16)
    wrl = (wr - wrh.astype(F32)).astype(BF16)
    wr2 = jnp.concatenate([jnp.concatenate([wrh, wrl], axis=1),
                           jnp.concatenate([wrh, jnp.zeros_like(wrl)], axis=1)], axis=0)
    br = jnp.zeros((1, LANES), F32).at[0, :MOE_GROUPS].set(b_group)
    br = br.at[0, MOE_GROUPS:MOE_GROUPS + MOE_EXPERTS].set(b_router.reshape(-1))
    tbs = tb + MOE_GROUPS * MOE_SUB
    nblk = T // tb
    return pl.pallas_call(
        _moe_kernel,
        grid=(nblk, MOE_EXPERTS // MOE_EPS),
        in_specs=[
            pl.BlockSpec((tb, D), lambda i, e: (jnp.where(e == 0, i, jnp.minimum(i + 1, nblk - 1)), 0)),
            _full((2 * D, 2 * LANES)), _full((1, LANES)),
            pl.BlockSpec((MOE_EPS, D, MOE_FF), lambda i, e: (e, 0, 0)),
            pl.BlockSpec((MOE_EPS, D, MOE_FF), lambda i, e: (e, 0, 0)),
            pl.BlockSpec((MOE_EPS, MOE_FF, D), lambda i, e: (e, 0, 0)),
            _full((1, D)), _full((1, D)),
        ],
        out_specs=pl.BlockSpec((tb, D), lambda i, e: (i, 0)),
        out_shape=jax.ShapeDtypeStruct((T, D), F32),
        scratch_shapes=[pltpu.VMEM((tbs, D), BF16), pltpu.VMEM((tbs, LANES), F32), pltpu.VMEM((tb, tbs), BF16),
                        pltpu.VMEM((tbs, D), F32), pltpu.VMEM((tb, D), F32), pltpu.VMEM((LANES, tb), F32),
                        pltpu.VMEM((8, tb), F32), pltpu.VMEM((tb, LANES), F32),
                        pltpu.SMEM((4 * MOE_GROUPS,), jnp.int32)],
        compiler_params=pltpu.CompilerParams(dimension_semantics=("arbitrary", "arbitrary"),
                                             vmem_limit_bytes=MOE_VMEM_LIMIT),
        name="moe_layer",
    )(h, wr2, br, w_gate.astype(BF16), w_up.astype(BF16), w_down.astype(BF16),
      ln_g.reshape(1, -1), ln_b.reshape(1, -1))


def _kv_kernel(x_ref, wk_ref, wvt_ref, k_ref, vt_ref):
    xb = x_ref[0].astype(BF16)
    k_ref[0] = jnp.dot(xb, wk_ref[...], preferred_element_type=F32).astype(BF16)
    vt = lax.dot_general(wvt_ref[...], xb, (((1,), (1,)), ((), ())), preferred_element_type=F32)
    for j in range(vt_ref.shape[1]):
        vt_ref[0, j] = vt[:, j * ATT_BLK:(j + 1) * ATT_BLK].astype(BF16)


def _kv_proj(h, kv_w, *, ts=1024):
    B, S, D = h.shape
    k1 = kv_w[:, :DIFF_QK].reshape(D, DIFF_HEADS, DIFF_DH)
    k2 = kv_w[:, DIFF_QK:2 * DIFF_QK].reshape(D, DIFF_HEADS, DIFF_DH)
    wk = jnp.stack([k1, k2], axis=2).reshape(D, 2 * DIFF_QK).astype(BF16)
    wvt = kv_w[:, 2 * DIFF_QK:].T.astype(BF16)
    nv = DIFF_HEADS * DIFF_DV
    nj = ts // ATT_BLK
    return pl.pallas_call(
        _kv_kernel,
        grid=(B, S // ts),
        in_specs=[pl.BlockSpec((1, ts, D), lambda b, j: (b, j, 0)), _full((D, 2 * DIFF_QK)), _full((nv, D))],
        out_specs=[pl.BlockSpec((1, ts, 2 * DIFF_QK), lambda b, j: (b, j, 0)),
                   pl.BlockSpec((1, nj, nv, ATT_BLK), lambda b, j: (b, j, 0, 0))],
        out_shape=[jax.ShapeDtypeStruct((B, S, 2 * DIFF_QK), BF16),
                   jax.ShapeDtypeStruct((B, S // ATT_BLK, nv, ATT_BLK), BF16)],
        compiler_params=pltpu.CompilerParams(dimension_semantics=("parallel", "parallel"),
                                             vmem_limit_bytes=VMEM_LIMIT),
        name="kv_proj",
    )(h, wk, wvt)


def _q_kernel(x_ref, wqt_ref, qt_ref):
    xb = x_ref[0].astype(BF16)
    qt = lax.dot_general(wqt_ref[...], xb, (((1,), (1,)), ((), ())), preferred_element_type=F32)
    qt = qt * (DIFF_DH ** -0.5 * LOG2E)
    for j in range(qt_ref.shape[1]):
        qt_ref[0, j] = qt[:, j * ATT_BLK:(j + 1) * ATT_BLK].astype(BF16)


def _q_proj(h, w_q, *, ts=1024):
    B, S, D = h.shape
    q1 = w_q[:, :DIFF_QK].reshape(D, DIFF_HEADS, DIFF_DH)
    q2 = w_q[:, DIFF_QK:].reshape(D, DIFF_HEADS, DIFF_DH)
    wqt = jnp.stack([q1, q2], axis=2).reshape(D, 2 * DIFF_QK).T.astype(BF16)
    nq = 2 * DIFF_QK
    nj = ts // ATT_BLK
    return pl.pallas_call(
        _q_kernel,
        grid=(B, S // ts),
        in_specs=[pl.BlockSpec((1, ts, D), lambda b, j: (b, j, 0)), _full((nq, D))],
        out_specs=pl.BlockSpec((1, nj, nq, ATT_BLK), lambda b, j: (b, j, 0, 0)),
        out_shape=jax.ShapeDtypeStruct((B, S // ATT_BLK, nq, ATT_BLK), BF16),
        compiler_params=pltpu.CompilerParams(dimension_semantics=("parallel", "parallel"),
                                             vmem_limit_bytes=VMEM_LIMIT),
        name="q_proj",
    )(h, wqt)


def _bias_kernel(tab_ref, o_ref):
    h = pl.program_id(0)
    kk = lax.broadcasted_iota(jnp.int32, (ATT_BLK, ATT_BLK), 0)
    qq = lax.broadcasted_iota(jnp.int32, (ATT_BLK, ATT_BLK), 1)
    nb = REL_BUCKETS // 2
    far = tab_ref[nb - 1, h]
    for t in range(2):
        rel = kk - qq - t * ATT_BLK
        n = jnp.abs(rel)
        large = nb // 2
        for thr in (12, 16, 23, 32, 46, 64, 91):
            large = large + (n >= thr).astype(jnp.int32)
        bucket = jnp.where(rel > 0, nb, 0) + jnp.where(n < nb // 2, n, large)
        bias = jnp.zeros((ATT_BLK, ATT_BLK), F32)
        for b in range(REL_BUCKETS):
            bias = jnp.where(bucket == b, tab_ref[b, h], bias)
        bias = (bias - far) * LOG2E
        if t == 0:
            visible = (kk // CHUNK) <= (qq // CHUNK)
            bias = jnp.where(visible, bias, NEG_INF)
        o_ref[0, t] = bias


def _bias_tiles(rel_table):
    return pl.pallas_call(
        _bias_kernel,
        grid=(DIFF_HEADS,),
        in_specs=[pl.BlockSpec(memory_space=pltpu.SMEM)],
        out_specs=pl.BlockSpec((1, 2, ATT_BLK, ATT_BLK), lambda h: (h, 0, 0, 0)),
        out_shape=jax.ShapeDtypeStruct((DIFF_HEADS, 2, ATT_BLK, ATT_BLK), F32),
        name="bias_tiles",
    )(rel_table)


def _attn_kernel(lam_ref, qt_ref, k_ref, vt_ref, bias_ref, g_ref, o_ref, *scratch, lambda_init):
    blk = ATT_BLK
    qi = pl.program_id(2)
    n_far = jnp.maximum(qi - 1, 0)
    prev_block = jnp.maximum(qi - 1, 0)
    heads = [scratch[hh * 7:(hh + 1) * 7] for hh in range(ATT_HPS)]
    z = jnp.zeros((DIFF_DH, blk), BF16)
    ones_rows = jnp.ones((ATT_SUM_ROWS, blk), BF16)
    for hh, (w_ref, _, _, _, _, acc_ref, ml_ref) in enumerate(heads):
        qt = qt_ref[0, 0, hh * 2 * DIFF_DH:(hh + 1) * 2 * DIFF_DH, :]
        w_ref[...] = jnp.concatenate(
            [jnp.concatenate([qt[:DIFF_DH], z], 0), jnp.concatenate([z, qt[DIFF_DH:]], 0)], 1)
        ml_ref[0:1, :] = jnp.full((1, 2 * blk), NEG_INF, F32)
        acc_ref[...] = jnp.zeros_like(acc_ref)

    def scores(kb, slot, bias_idx):
        for hh, (w_ref, s0_ref, s1_ref, t0_ref, t1_ref, _, _) in enumerate(heads):
            s_ref, t_ref = ((s0_ref, t0_ref), (s1_ref, t1_ref))[slot]
            kt = k_ref[0, pl.ds(pl.multiple_of(kb * blk, blk), blk), hh * LANES:(hh + 1) * LANES]
            bias = bias_ref[hh, bias_idx]
            s = jnp.dot(kt, w_ref[...], preferred_element_type=F32) + jnp.concatenate([bias, bias], axis=1)
            s_ref[...] = s
            t_ref[0:1, :] = jnp.max(s, 0, keepdims=True)

    def accumulate(kb, slot, then=None):
        for hh, (_, s0_ref, s1_ref, t0_ref, t1_ref, acc_ref, ml_ref) in enumerate(heads):
            s_ref, t_ref = ((s0_ref, t0_ref), (s1_ref, t1_ref))[slot]
            m_old = ml_ref[0:1, :]
            m_new = jnp.maximum(m_old, t_ref[0:1, :])
            a = jnp.exp2(m_old - m_new)
            p = jnp.exp2(s_ref[...] - m_new)
            ml_ref[0:1, :] = m_new
            pb = p.astype(BF16)
            vt = jnp.concatenate([vt_ref[0, kb, hh * DIFF_DV:(hh + 1) * DIFF_DV, :], ones_rows], axis=0)
            for i in range(2):
                cols = slice(i * blk, (i + 1) * blk)
                acc_ref[i] = a[:, cols] * acc_ref[i] + jnp.dot(vt, pb[:, cols], preferred_element_type=F32)
            if then is not None:
                then(hh)

    def fused(kb_a, slot_a, kb_b, slot_b, bias_idx=None, extra=None):
        nslab = 4
        rows = blk // nslab
        width = 2 * blk // nslab
        for hh, (w_ref, s0_ref, s1_ref, t0_ref, t1_ref, acc_ref, ml_ref) in enumerate(heads):
            sa_ref, ta_ref = ((s0_ref, t0_ref), (s1_ref, t1_ref))[slot_a]
            sb_ref, tb_ref = ((s0_ref, t0_ref), (s1_ref, t1_ref))[slot_b]
            m_old = ml_ref[0:1, :]
            m_new = jnp.maximum(m_old, tb_ref[0:1, :])
            a = jnp.exp2(m_old - m_new)
            ml_ref[0:1, :] = m_new
            vt = jnp.concatenate([vt_ref[0, kb_b, hh * DIFF_DV:(hh + 1) * DIFF_DV, :], ones_rows], axis=0)
            tmax = None
            for j in range(nslab):
                cols = slice(j * width, (j + 1) * width)
                p = jnp.exp2(sb_ref[:, cols] - m_new[:, cols])
                i, c0 = divmod(j * width, blk)
                acc_ref[i, :, c0:c0 + width] = (a[:, cols] * acc_ref[i, :, c0:c0 + width]
                                                + jnp.dot(vt, p.astype(BF16), preferred_element_type=F32))
                kt = k_ref[0, pl.ds(pl.multiple_of(kb_a * blk + j * rows, rows), rows),
                           hh * LANES:(hh + 1) * LANES]
                s = jnp.dot(kt, w_ref[...], preferred_element_type=F32)
                if bias_idx is not None:
                    bias = bias_ref[hh, bias_idx, j * rows:(j + 1) * rows, :]
                    s = s + jnp.concatenate([bias, bias], axis=1)
                if extra is not None:
                    s = s + extra
                sa_ref[j * rows:(j + 1) * rows, :] = s
                smax = jnp.max(s, 0, keepdims=True)
                tmax = smax if tmax is None else jnp.maximum(tmax, smax)
            ta_ref[0:1, :] = tmax

    scores(qi, 0, bias_idx=0)
    fused(prev_block, 1, qi, 0, bias_idx=1, extra=jnp.where(qi >= 1, 0.0, -jnp.inf).astype(F32))

    def pair(j):
        kb = 2 * j
        fused(kb, 0, jnp.where(j == 0, prev_block, kb - 1), 1)
        fused(kb + 1, 1, kb, 0)

    def two_pairs(i, carry):
        pair(2 * i)
        pair(2 * i + 1)
        return carry

    n_pairs = n_far // 2
    lax.fori_loop(0, n_pairs // 2, two_pairs, 0)

    @pl.when(n_pairs % 2 == 1)
    def _():
        pair(n_pairs - 1)

    last_slot1 = jnp.where(n_pairs == 0, prev_block, 2 * n_pairs - 1)

    def finish(hh):
        acc_ref = heads[hh][5]
        lam = (jnp.exp(jnp.sum(lam_ref[0:1, :] * lam_ref[1:2, :], keepdims=True))
               - jnp.exp(jnp.sum(lam_ref[2:3, :] * lam_ref[3:4, :], keepdims=True)) + lambda_init)
        o1, o2 = acc_ref[0], acc_ref[1]
        ot = (o1[:DIFF_DV] / o1[DIFF_DV:DIFF_DV + 1] - lam * (o2[:DIFF_DV] / o2[DIFF_DV:DIFF_DV + 1]))
        ot = ot * lax.rsqrt(jnp.mean(ot * ot, 0, keepdims=True) + LN_EPS) * g_ref[...] * (1.0 - lambda_init)
        o_ref[0, :, hh * DIFF_DV:(hh + 1) * DIFF_DV] = ot.T.astype(BF16)

    @pl.when(n_far % 2 == 1)
    def _():
        fused(n_far - 1, 0, last_slot1, 1)
        accumulate(n_far - 1, 0, then=finish)

    @pl.when(n_far % 2 == 0)
    def _():
        accumulate(last_slot1, 1, then=finish)


def _diff_attention(qt, k12, vt, bias, lam4, g_sub, lambda_init):
    B, nqb, _, _ = qt.shape
    S = k12.shape[1]
    kernel = functools.partial(_attn_kernel, lambda_init=lambda_init)
    once = pl.Buffered(1)
    head_scratch = [pltpu.VMEM((2 * DIFF_DH, 2 * ATT_BLK), BF16),
                    pltpu.VMEM((ATT_BLK, 2 * ATT_BLK), F32), pltpu.VMEM((ATT_BLK, 2 * ATT_BLK), F32),
                    pltpu.VMEM((8, 2 * ATT_BLK), F32), pltpu.VMEM((8, 2 * ATT_BLK), F32),
                    pltpu.VMEM((2, DIFF_DV + ATT_SUM_ROWS, ATT_BLK), F32), pltpu.VMEM((8, 2 * ATT_BLK), F32)]
    return pl.pallas_call(
        kernel,
        grid=(B, DIFF_HEADS // ATT_HPS, nqb),
        in_specs=[
            _full((4, DIFF_DH)),
            pl.BlockSpec((1, 1, ATT_HPS * 2 * DIFF_DH, ATT_BLK), lambda b, h, i: (b, i, h, 0)),
            pl.BlockSpec((1, S, ATT_HPS * 2 * DIFF_DH), lambda b, h, i: (b, 0, h), pipeline_mode=once),
            pl.BlockSpec((1, nqb, ATT_HPS * DIFF_DV, ATT_BLK), lambda b, h, i: (b, 0, h, 0),
                         pipeline_mode=once),
            pl.BlockSpec((ATT_HPS, 2, ATT_BLK, ATT_BLK), lambda b, h, i: (h, 0, 0, 0), pipeline_mode=once),
            _full((DIFF_DV, 1)),
        ],
        out_specs=pl.BlockSpec((1, ATT_BLK, ATT_HPS * DIFF_DV), lambda b, h, i: (b, i, h)),
        out_shape=jax.ShapeDtypeStruct((B, S, DIFF_HEADS * DIFF_DV), BF16),
        scratch_shapes=head_scratch * ATT_HPS,
        compiler_params=pltpu.CompilerParams(dimension_semantics=("parallel", "parallel", "arbitrary"),
                                             vmem_limit_bytes=VMEM_LIMIT),
        name="diff_attn",
    )(lam4, qt, k12, vt, bias, g_sub.reshape(-1, 1))


def _proj_ln_kernel(h_ref, a_ref, w_ref, lng_ref, lnb_ref, o_ref):
    npiece = 2
    piece = h_ref.shape[0] // npiece
    for c in range(npiece):
        rows = slice(c * piece, (c + 1) * piece)
        mix = jnp.dot(a_ref[rows, :], w_ref[...], preferred_element_type=F32)
        o_ref[rows, :] = _layer_norm(DEEPNORM_ALPHA * h_ref[rows, :] + mix, lng_ref[...], lnb_ref[...])


def _proj_ln(h, a, w_out, ln_g, ln_b, *, tb=1024):
    T, D = h.shape
    return pl.pallas_call(
        _proj_ln_kernel,
        grid=(T // tb,),
        in_specs=[pl.BlockSpec((tb, D), lambda i: (i, 0)), pl.BlockSpec((tb, a.shape[1]), lambda i: (i, 0)),
                  _full(w_out.shape), _full((1, D)), _full((1, D))],
        out_specs=pl.BlockSpec((tb, D), lambda i: (i, 0)),
        out_shape=jax.ShapeDtypeStruct((T, D), F32),
        compiler_params=pltpu.CompilerParams(dimension_semantics=("parallel",), vmem_limit_bytes=VMEM_LIMIT),
        name="attn_out_ln",
    )(h, a, w_out.astype(BF16), ln_g.reshape(1, -1), ln_b.reshape(1, -1))


def kernel(x, a_w_in, a_w_gate2, a_b_gate, a_g_norm, a_w_out, kv_w, b_w_q, b_lam_q1, b_lam_k1, b_lam_q2, b_lam_k2, b_g_sub, b_w_out, rel_table, moe_w_group, moe_b_group, moe_w_router, moe_b_router, moe_w_gate, moe_w_up, moe_w_down, ln_g, ln_b):
    B, S, D = x.shape
    h = x
    bias = None
    k12 = vt = None
    for layer in range(DEPTH):
        if layer < N_A_LAYERS:
            h = _gla_layer(h, a_w_in[layer], a_w_gate2[layer], a_b_gate[layer], a_g_norm[layer],
                           a_w_out[layer], ln_g[layer, 0], ln_b[layer, 0])
        else:
            j = layer - N_A_LAYERS
            lambda_init = 0.8 - 0.6 * math.exp(-0.3 * layer)
            if bias is None:
                bias = _bias_tiles(rel_table)
            qt = _q_proj(h, b_w_q[j])
            lam4 = jnp.stack([b_lam_q1[j], b_lam_k1[j], b_lam_q2[j], b_lam_k2[j]])
            a = _diff_attention(qt, k12, vt, bias, lam4, b_g_sub[j], lambda_init)
            h = _proj_ln(h.reshape(B * S, D), a.reshape(B * S, -1), b_w_out[j], ln_g[layer, 0],
                         ln_b[layer, 0]).reshape(B, S, D)
        h = _moe_layer(h.reshape(B * S, D), moe_w_group[layer], moe_b_group[layer], moe_w_router[layer],
                       moe_b_router[layer], moe_w_gate[layer], moe_w_up[layer], moe_w_down[layer],
                       ln_g[layer, 1], ln_b[layer, 1]).reshape(B, S, D)
        if layer == N_A_LAYERS - 1:
            k12, vt = _kv_proj(h, kv_w)
    return h
```

```python
import functools
import math

import jax
import jax.numpy as jnp
from jax import lax
from jax.experimental import pallas as pl
from jax.experimental.pallas import tpu as pltpu

F32 = jnp.float32
BF16 = jnp.bfloat16

D_MODEL = 1024
DEPTH = 4
CHUNK = 64
N_A_LAYERS = 2

GLA_HEADS = 4
GLA_DK = 128
GLA_DV = 256
GLA_GATE_RANK = 16
GLA_GATE_TAU = 16.0
GLA_HK = GLA_HEADS * GLA_DK
GLA_HV = GLA_HEADS * GLA_DV

DIFF_HEADS = 8
DIFF_DH = 64
DIFF_DV = 128
DIFF_QK = DIFF_HEADS * DIFF_DH

REL_BUCKETS = 32

MOE_GROUPS = 4
MOE_EPG = 4
MOE_EXPERTS = 16
MOE_FF = 512

DEEPNORM_ALPHA = (2.0 * DEPTH) ** 0.25
LN_EPS = 1e-5
NEG_INF = -1e30

LANES = 128
ATT_BLK = 512
ATT_HPS = 2
ATT_SUM_ROWS = 16
LOG2E = math.log2(math.e)
VMEM_LIMIT = 48 * 1024 * 1024
MOE_VMEM_LIMIT = 58 * 1024 * 1024
MOE_SUB = 128
MOE_EPS = 2
MOE_POS_LANE = 12
MOE_TR_ROWS = 16


def _layer_norm(y, g, b):
    mu = jnp.mean(y, -1, keepdims=True)
    d = y - mu
    var = jnp.mean(d * d, -1, keepdims=True)
    return d * lax.rsqrt(var + LN_EPS) * g + b


def _sigmoid(x):
    return 1.0 / (1.0 + jnp.exp(-x))


def _full(shape):
    return pl.BlockSpec(shape, lambda *_: (0,) * len(shape))


def _gla_kernel(x_ref, w_main_ref, w_kt_ref, w_lr_ref, w_g2_ref, b_g_ref, gn_ref, w_out_ref, lng_ref, lnb_ref,
                tri_ref, o_ref, st_ref, mix_ref):
    nb, tb, d = x_ref.shape
    nt = (((1,), (1,)), ((), ()))

    @pl.when(pl.program_id(0) == 0)
    def _():
        st_ref[...] = jnp.zeros_like(st_ref)

    x = x_ref[...].reshape(nb * tb, d)
    xb = x.astype(BF16)
    lrt = lax.dot_general(w_lr_ref[...], xb, nt, preferred_element_type=F32)
    kt = lax.dot_general(w_kt_ref[...], xb, nt, preferred_element_type=F32)
    hq = jnp.dot(xb, w_main_ref[:, :GLA_HK], preferred_element_type=F32)
    glog = jnp.dot(w_g2_ref[...], lrt.astype(BF16), preferred_element_type=F32) + b_g_ref[...]
    hv = jnp.dot(xb, w_main_ref[:, GLA_HK:GLA_HK + GLA_HV], preferred_element_type=F32)
    la = (jnp.minimum(glog, 0.0) - jnp.log(1.0 + jnp.exp(-jnp.abs(glog)))) * (1.0 / GLA_GATE_TAU)
    la_hi = la.astype(BF16)
    la_lo = (la - la_hi.astype(F32)).astype(BF16)
    tri = tri_ref[...]
    cum = [jnp.dot(la_hi[:, b * tb:(b + 1) * tb], tri, preferred_element_type=F32)
           + jnp.dot(la_lo[:, b * tb:(b + 1) * tb], tri, preferred_element_type=F32) for b in range(nb)]
    r = jnp.dot(xb, w_main_ref[:, GLA_HK + GLA_HV:], preferred_element_type=F32)

    lane = lax.broadcasted_iota(jnp.int32, (GLA_HK, 2 * CHUNK), 1)
    halves = (lane < CHUNK, lane >= CHUNK)
    heads = [(slice(h * GLA_DK, (h + 1) * GLA_DK), slice(h * GLA_DV, (h + 1) * GLA_DV)) for h in range(GLA_HEADS)]
    for pair in range(tb // (2 * CHUNK)):
        c0 = pair * 2 * CHUNK
        dec, upd = {}, {}
        for b in range(nb):
            t0 = b * tb + c0
            cum_p = cum[b][:, c0:c0 + 2 * CHUNK]
            last = (cum_p[:, CHUNK - 1:CHUNK], cum_p[:, 2 * CHUNK - 1:2 * CHUNK])
            kdec = kt[:, t0:t0 + 2 * CHUNK] * jnp.exp(jnp.where(halves[0], last[0], last[1]) - cum_p)
            v_pair = hv[t0:t0 + 2 * CHUNK].astype(BF16)
            for half in range(2):
                kd = jnp.where(halves[half], kdec, 0.0).astype(BF16)
                dec[b, half] = jnp.exp(last[half])
                for h, (ks, vs) in enumerate(heads):
                    upd[b, half, h] = jnp.dot(kd[ks, :], v_pair[:, vs], preferred_element_type=F32)
        for half in range(2):
            for b in range(nb):
                r0 = b * tb + c0 + half * CHUNK
                qc = (hq[r0:r0 + CHUNK] * (GLA_DK ** -0.5)).astype(BF16)
                for h, (ks, vs) in enumerate(heads):
                    st = st_ref[b, h] * dec[b, half][ks, :] + upd[b, half, h]
                    st_ref[b, h] = st
                    o = jnp.dot(qc[:, ks], st.astype(BF16), preferred_element_type=F32)
                    o = o * lax.rsqrt(jnp.mean(o * o, -1, keepdims=True) + LN_EPS) * gn_ref[:, vs]
                    mix_ref[r0:r0 + CHUNK, vs] = o

    gated = mix_ref[...] * (r * _sigmoid(r))
    mix = jnp.dot(gated.astype(BF16), w_out_ref[...], preferred_element_type=F32)
    o_ref[...] = _layer_norm(DEEPNORM_ALPHA * x + mix, lng_ref[...], lnb_ref[...]).reshape(nb, tb, d)


def _gla_layer(h, w_in, w_gate2, b_gate, g_norm, w_out, ln_g, ln_b, *, tb=256):
    B, S, D = h.shape
    n_k = 2 * GLA_HK
    n_main = GLA_HK + 2 * GLA_HV
    n_all = 2 * GLA_HK + 2 * GLA_HV
    w_main = jnp.concatenate([w_in[:, :GLA_HK], w_in[:, n_k:n_all]], axis=1).astype(BF16)
    w_kt = w_in[:, GLA_HK:n_k].T.astype(BF16)
    w_lr = jnp.zeros((LANES, D), F32).at[:GLA_GATE_RANK].set(w_in[:, n_all:].T).astype(BF16)
    w_g2 = jnp.zeros((GLA_HK, LANES), F32).at[:, :GLA_GATE_RANK].set(w_gate2.T).astype(BF16)
    idx = jnp.arange(tb)
    tri = ((idx[:, None] <= idx[None, :]) & (idx[:, None] // CHUNK == idx[None, :] // CHUNK)).astype(BF16)
    return pl.pallas_call(
        _gla_kernel,
        grid=(S // tb,),
        in_specs=[
            pl.BlockSpec((B, tb, D), lambda j: (0, j, 0)),
            _full((D, n_main)), _full((GLA_HK, D)), _full((LANES, D)), _full((GLA_HK, LANES)),
            _full((GLA_HK, 1)), _full((1, GLA_HV)), _full((GLA_HV, D)), _full((1, D)), _full((1, D)),
            _full((tb, tb)),
        ],
        out_specs=pl.BlockSpec((B, tb, D), lambda j: (0, j, 0)),
        out_shape=jax.ShapeDtypeStruct((B, S, D), F32),
        scratch_shapes=[pltpu.VMEM((B, GLA_HEADS, GLA_DK, GLA_DV), F32), pltpu.VMEM((B * tb, GLA_HV), F32)],
        compiler_params=pltpu.CompilerParams(dimension_semantics=("arbitrary",),
                                             vmem_limit_bytes=VMEM_LIMIT),
        name="gla_layer",
    )(h, w_main, w_kt, w_lr, w_g2, b_gate.reshape(-1, 1), g_norm.reshape(1, -1), w_out.astype(BF16),
      ln_g.reshape(1, -1), ln_b.reshape(1, -1), tri)


def _first_max(vals):
    top = functools.reduce(jnp.maximum, vals)
    idx = jnp.full_like(top, float(len(vals) - 1))
    for j in range(len(vals) - 2, -1, -1):
        idx = jnp.where(vals[j] == top, float(j), idx)
    return top, idx


def _route(lt):
    gl = [lt[g:g + 1, :] for g in range(MOE_GROUPS)]
    gmax, gidx = _first_max(gl)
    gw = 1.0 / functools.reduce(jnp.add, [jnp.exp(v - gmax) for v in gl])
    el = []
    for j in range(MOE_EPG):
        rows = [lt[MOE_GROUPS + g * MOE_EPG + j:MOE_GROUPS + g * MOE_EPG + j + 1, :] for g in range(MOE_GROUPS)]
        sel = rows[MOE_GROUPS - 1]
        for g in range(MOE_GROUPS - 2, -1, -1):
            sel = jnp.where(gidx == float(g), rows[g], sel)
        el.append(sel)
    emax = functools.reduce(jnp.maximum, el)
    pe = [jnp.exp(v - emax) for v in el]
    psum = functools.reduce(jnp.add, pe)
    prob = [v / psum for v in pe]
    p1, i1 = _first_max(prob)
    p2, i2 = _first_max([jnp.where(i1 == float(j), -1.0, prob[j]) for j in range(MOE_EPG)])
    den = p1 + p2
    weights = [(jnp.where(i1 == float(j), p1 / den, 0.0) + jnp.where(i2 == float(j), p2 / den, 0.0)) * gw
               for j in range(MOE_EPG)]
    return gidx, weights


def _split3(v):
    hi = v.astype(BF16).astype(F32)
    mid = (v - hi).astype(BF16).astype(F32)
    lo = (v - hi - mid).astype(BF16).astype(F32)
    return hi, mid, lo


def _router_logits(x, wr_ref, br_ref):
    tb = x.shape[0]
    xh = x.astype(BF16)
    xl = (x - xh.astype(F32)).astype(BF16)
    x2 = jnp.concatenate([xh, xl], axis=1)
    half = tb // 2
    both = jnp.concatenate([jnp.dot(x2[:half], wr_ref[...], preferred_element_type=F32),
                            jnp.dot(x2[half:], wr_ref[...], preferred_element_type=F32)], axis=0)
    return both[:, :LANES] + both[:, LANES:] + br_ref[...]


def _moe_route(logits, tr_ref, posr_ref, cwn_ref, meta_ref, meta_base):
    tb = logits.shape[0]
    gidx, weights = _route(logits.T)
    onehot = [jnp.where(gidx == float(g), 1.0, 0.0) for g in range(MOE_GROUPS)]
    oht = jnp.concatenate(onehot + [jnp.zeros((MOE_TR_ROWS - MOE_GROUPS, tb), F32)], axis=0)
    ohb = oht.astype(BF16)
    earlier = jnp.where(lax.broadcasted_iota(jnp.int32, (LANES, LANES), 0)
                        < lax.broadcasted_iota(jnp.int32, (LANES, LANES), 1), 1.0, 0.0).astype(BF16)
    before, carry = [], jnp.zeros((MOE_TR_ROWS, 1), F32)
    for piece in range(tb // LANES):
        cols = slice(piece * LANES, (piece + 1) * LANES)
        before.append(jnp.dot(ohb[:, cols], earlier, preferred_element_type=F32) + carry)
        carry = carry + jnp.sum(oht[:, cols], -1, keepdims=True)
    rank = jnp.sum(jnp.concatenate(before, axis=1) * oht, 0, keepdims=True)
    ntile = jnp.floor((carry + (MOE_SUB - 1.0)) * (1.0 / MOE_SUB))
    pos, first_row = rank, jnp.zeros((1, 1), F32)
    for g in range(MOE_GROUPS):
        meta_ref[meta_base + g] = jnp.sum(ntile[g:g + 1, :]).astype(jnp.int32)
        meta_ref[meta_base + MOE_GROUPS + g] = jnp.sum(first_row).astype(jnp.int32)
        pos = pos + onehot[g] * first_row
        first_row = first_row + ntile[g:g + 1, :] * MOE_SUB
    posr_ref[0:1, :] = pos
    for j in range(MOE_EPG):
        for k, part in enumerate(_split3(weights[j])):
            tr_ref[j + k * MOE_EPG:j + k * MOE_EPG + 1, :] = part
    tr_ref[MOE_POS_LANE:MOE_POS_LANE + 1, :] = pos
    tr_ref[MOE_POS_LANE + 1:MOE_TR_ROWS, :] = jnp.zeros((MOE_TR_ROWS - MOE_POS_LANE - 1, tb), F32)
    cwn_ref[...] = tr_ref[...].T


def _moe_kernel(x_ref, wr_ref, br_ref, wg_ref, wu_ref, wd_ref, lng_ref, lnb_ref, o_ref,
                xs_ref, cw_ref, pt_ref, y_ref, res_ref, tr_ref, posr_ref, cwn_ref, meta_ref):
    blk_i = pl.program_id(0)
    step = pl.program_id(1)
    steps_per_group = MOE_EPG // MOE_EPS
    grp = step // steps_per_group
    j0 = (step % steps_per_group) * MOE_EPS
    tb = x_ref.shape[0]
    tbs = xs_ref.shape[0]
    d = x_ref.shape[1]
    meta_cur = (blk_i % 2) * (2 * MOE_GROUPS)
    meta_next = 2 * MOE_GROUPS - meta_cur

    @pl.when(step == 0)
    def _():
        @pl.when(blk_i == 0)
        def _():
            tr_ref[...] = jnp.zeros_like(tr_ref)
            _moe_route(_router_logits(x_ref[...], wr_ref, br_ref), tr_ref, posr_ref, cwn_ref, meta_ref, meta_cur)

        x = x_ref[...]
        res_ref[...] = x
        cwn = cwn_ref[...]
        pos_col = cwn[:, MOE_POS_LANE:MOE_POS_LANE + 1].astype(jnp.int32)
        pt_ref[...] = jnp.where(lax.broadcasted_iota(jnp.int32, (tb, tbs), 1) == pos_col, 1.0, 0.0).astype(BF16)
        pos_row = posr_ref[0:1, :].astype(jnp.int32)
        perm = jnp.where(lax.broadcasted_iota(jnp.int32, (tbs, tb), 0) == pos_row, 1.0, 0.0).astype(BF16)
        sorted_all = jnp.dot(perm, jnp.concatenate([x.astype(BF16), cwn.astype(BF16)], axis=1),
                             preferred_element_type=F32)
        xs_ref[...] = sorted_all[:, :d].astype(BF16)
        cw_ref[...] = sorted_all[:, d:]
        y_ref[...] = jnp.zeros_like(y_ref)

    base = meta_ref[meta_cur + MOE_GROUPS + grp]

    def experts(start, nrows):
        rows = pl.ds(pl.multiple_of(start, MOE_SUB), nrows)
        xt = xs_ref[rows, :]
        cwt = cw_ref[rows, :]
        lane = lax.broadcasted_iota(jnp.int32, (nrows, LANES), 1)
        hidden = []
        for k in range(MOE_EPS):
            gt = jnp.dot(xt, wg_ref[k], preferred_element_type=F32)
            ut = jnp.dot(xt, wu_ref[k], preferred_element_type=F32)
            hidden.append(((gt * _sigmoid(gt)) * ut).astype(BF16))
        total = None
        for k in range(MOE_EPS):
            y = jnp.dot(hidden[k], wd_ref[k], preferred_element_type=F32)
            mine = (lane % MOE_EPG == j0 + k) & (lane < 3 * MOE_EPG)
            c = jnp.sum(jnp.where(mine, cwt, 0.0), -1, keepdims=True)
            total = c * y if total is None else total + c * y
        y_ref[rows, :] += total

    n_tiles = meta_ref[meta_cur + grp]

    def wide(i, carry):
        experts(base + i * (2 * MOE_SUB), 2 * MOE_SUB)
        return carry

    lax.fori_loop(0, n_tiles // 2, wide, 0)

    @pl.when(n_tiles % 2 == 1)
    def _():
        experts(base + (n_tiles - 1) * MOE_SUB, MOE_SUB)

    @pl.when(step == MOE_EXPERTS // MOE_EPS - 1)
    def _():
        logits = _router_logits(x_ref[...], wr_ref, br_ref)
        yb = y_ref[...].astype(BF16)
        npiece = 4
        piece = tb // npiece
        for c in range(npiece):
            if c == npiece - 1:
                _moe_route(logits, tr_ref, posr_ref, cwn_ref, meta_ref, meta_next)
            rows = slice(c * piece, (c + 1) * piece)
            ffn = jnp.dot(pt_ref[rows, :], yb, preferred_element_type=F32)
            o_ref[rows, :] = _layer_norm(DEEPNORM_ALPHA * res_ref[rows, :] + ffn, lng_ref[...], lnb_ref[...])


def _moe_layer(h, w_group, b_group, w_router, b_router, w_gate, w_up, w_down, ln_g, ln_b, *, tb=1024):
    T, D = h.shape
    wr = jnp.zeros((D, LANES), F32).at[:, :MOE_GROUPS].set(w_group)
    wr = wr.at[:, MOE_GROUPS:MOE_GROUPS + MOE_EXPERTS].set(w_router)
    wrh = wr.astype(BF16)
    wrl = (wr - wrh.astype(F32)).astype(BF16)
    wr2 = jnp.concatenate([jnp.concatenate([wrh, wrl], axis=1),
                           jnp.concatenate([wrh, jnp.zeros_like(wrl)], axis=1)], axis=0)
    br = jnp.zeros((1, LANES), F32).at[0, :MOE_GROUPS].set(b_group)
    br = br.at[0, MOE_GROUPS:MOE_GROUPS + MOE_EXPERTS].set(b_router.reshape(-1))
    tbs = tb + MOE_GROUPS * MOE_SUB
    nblk = T // tb
    return pl.pallas_call(
        _moe_kernel,
        grid=(nblk, MOE_EXPERTS // MOE_EPS),
        in_specs=[
            pl.BlockSpec((tb, D), lambda i, e: (jnp.where(e == 0, i, jnp.minimum(i + 1, nblk - 1)), 0)),
            _full((2 * D, 2 * LANES)), _full((1, LANES)),
            pl.BlockSpec((MOE_EPS, D, MOE_FF), lambda i, e: (e, 0, 0)),
            pl.BlockSpec((MOE_EPS, D, MOE_FF), lambda i, e: (e, 0, 0)),
            pl.BlockSpec((MOE_EPS, MOE_FF, D), lambda i, e: (e, 0, 0)),
            _full((1, D)), _full((1, D)),
        ],
        out_specs=pl.BlockSpec((tb, D), lambda i, e: (i, 0)),
        out_shape=jax.ShapeDtypeStruct((T, D), F32),
        scratch_shapes=[pltpu.VMEM((tbs, D), BF16), pltpu.VMEM((tbs, LANES), F32), pltpu.VMEM((tb, tbs), BF16),
                        pltpu.VMEM((tbs, D), F32), pltpu.VMEM((tb, D), F32), pltpu.VMEM((LANES, tb), F32),
                        pltpu.VMEM((8, tb), F32), pltpu.VMEM((tb, LANES), F32),
                        pltpu.SMEM((4 * MOE_GROUPS,), jnp.int32)],
        compiler_params=pltpu.CompilerParams(dimension_semantics=("arbitrary", "arbitrary"),
                                             vmem_limit_bytes=MOE_VMEM_LIMIT),
        name="moe_layer",
    )(h, wr2, br, w_gate.astype(BF16), w_up.astype(BF16), w_down.astype(BF16),
      ln_g.reshape(1, -1), ln_b.reshape(1, -1))


def _kv_kernel(x_ref, wk_ref, wvt_ref, k_ref, vt_ref):
    xb = x_ref[0].astype(BF16)
    k_ref[0] = jnp.dot(xb, wk_ref[...], preferred_element_type=F32).astype(BF16)
    vt = lax.dot_general(wvt_ref[...], xb, (((1,), (1,)), ((), ())), preferred_element_type=F32)
    for j in range(vt_ref.shape[1]):
        vt_ref[0, j] = vt[:, j * ATT_BLK:(j + 1) * ATT_BLK].astype(BF16)


def _kv_proj(h, kv_w, *, ts=1024):
    B, S, D = h.shape
    k1 = kv_w[:, :DIFF_QK].reshape(D, DIFF_HEADS, DIFF_DH)
    k2 = kv_w[:, DIFF_QK:2 * DIFF_QK].reshape(D, DIFF_HEADS, DIFF_DH)
    wk = jnp.stack([k1, k2], axis=2).reshape(D, 2 * DIFF_QK).astype(BF16)
    wvt = kv_w[:, 2 * DIFF_QK:].T.astype(BF16)
    nv = DIFF_HEADS * DIFF_DV
    nj = ts // ATT_BLK
    return pl.pallas_call(
        _kv_kernel,
        grid=(B, S // ts),
        in_specs=[pl.BlockSpec((1, ts, D), lambda b, j: (b, j, 0)), _full((D, 2 * DIFF_QK)), _full((nv, D))],
        out_specs=[pl.BlockSpec((1, ts, 2 * DIFF_QK), lambda b, j: (b, j, 0)),
                   pl.BlockSpec((1, nj, nv, ATT_BLK), lambda b, j: (b, j, 0, 0))],
        out_shape=[jax.ShapeDtypeStruct((B, S, 2 * DIFF_QK), BF16),
                   jax.ShapeDtypeStruct((B, S // ATT_BLK, nv, ATT_BLK), BF16)],
        compiler_params=pltpu.CompilerParams(dimension_semantics=("parallel", "parallel"),
                                             vmem_limit_bytes=VMEM_LIMIT),
        name="kv_proj",
    )(h, wk, wvt)


def _q_kernel(x_ref, wqt_ref, qt_ref):
    xb = x_ref[0].astype(BF16)
    qt = lax.dot_general(wqt_ref[...], xb, (((1,), (1,)), ((), ())), preferred_element_type=F32)
    qt = qt * (DIFF_DH ** -0.5 * LOG2E)
    for j in range(qt_ref.shape[1]):
        qt_ref[0, j] = qt[:, j * ATT_BLK:(j + 1) * ATT_BLK].astype(BF16)


def _q_proj(h, w_q, *, ts=1024):
    B, S, D = h.shape
    q1 = w_q[:, :DIFF_QK].reshape(D, DIFF_HEADS, DIFF_DH)
    q2 = w_q[:, DIFF_QK:].reshape(D, DIFF_HEADS, DIFF_DH)
    wqt = jnp.stack([q1, q2], axis=2).reshape(D, 2 * DIFF_QK).T.astype(BF16)
    nq = 2 * DIFF_QK
    nj = ts // ATT_BLK
    return pl.pallas_call(
        _q_kernel,
        grid=(B, S // ts),
        in_specs=[pl.BlockSpec((1, ts, D), lambda b, j: (b, j, 0)), _full((nq, D))],
        out_specs=pl.BlockSpec((1, nj, nq, ATT_BLK), lambda b, j: (b, j, 0, 0)),
        out_shape=jax.ShapeDtypeStruct((B, S // ATT_BLK, nq, ATT_BLK), BF16),
        compiler_params=pltpu.CompilerParams(dimension_semantics=("parallel", "parallel"),
                                             vmem_limit_bytes=VMEM_LIMIT),
        name="q_proj",
    )(h, wqt)


def _bias_kernel(tab_ref, o_ref):
    h = pl.program_id(0)
    kk = lax.broadcasted_iota(jnp.int32, (ATT_BLK, ATT_BLK), 0)
    qq = lax.broadcasted_iota(jnp.int32, (ATT_BLK, ATT_BLK), 1)
    nb = REL_BUCKETS // 2
    far = tab_ref[nb - 1, h]
    for t in range(2):
        rel = kk - qq - t * ATT_BLK
        n = jnp.abs(rel)
        large = nb // 2
        for thr in (12, 16, 23, 32, 46, 64, 91):
            large = large + (n >= thr).astype(jnp.int32)
        bucket = jnp.where(rel > 0, nb, 0) + jnp.where(n < nb // 2, n, large)
        bias = jnp.zeros((ATT_BLK, ATT_BLK), F32)
        for b in range(REL_BUCKETS):
            bias = jnp.where(bucket == b, tab_ref[b, h], bias)
        bias = (bias - far) * LOG2E
        if t == 0:
            visible = (kk // CHUNK) <= (qq // CHUNK)
            bias = jnp.where(visible, bias, NEG_INF)
        o_ref[0, t] = bias


def _bias_tiles(rel_table):
    return pl.pallas_call(
        _bias_kernel,
        grid=(DIFF_HEADS,),
        in_specs=[pl.BlockSpec(memory_space=pltpu.SMEM)],
        out_specs=pl.BlockSpec((1, 2, ATT_BLK, ATT_BLK), lambda h: (h, 0, 0, 0)),
        out_shape=jax.ShapeDtypeStruct((DIFF_HEADS, 2, ATT_BLK, ATT_BLK), F32),
        name="bias_tiles",
    )(rel_table)


def _attn_kernel(lam_ref, qt_ref, k_ref, vt_ref, bias_ref, g_ref, o_ref, *scratch, lambda_init):
    blk = ATT_BLK
    qi = pl.program_id(2)
    n_far = jnp.maximum(qi - 1, 0)
    prev_block = jnp.maximum(qi - 1, 0)
    heads = [scratch[hh * 7:(hh + 1) * 7] for hh in range(ATT_HPS)]
    z = jnp.zeros((DIFF_DH, blk), BF16)
    ones_rows = jnp.ones((ATT_SUM_ROWS, blk), BF16)
    for hh, (w_ref, _, _, _, _, acc_ref, ml_ref) in enumerate(heads):
        qt = qt_ref[0, 0, hh * 2 * DIFF_DH:(hh + 1) * 2 * DIFF_DH, :]
        w_ref[...] = jnp.concatenate(
            [jnp.concatenate([qt[:DIFF_DH], z], 0), jnp.concatenate([z, qt[DIFF_DH:]], 0)], 1)
        ml_ref[0:1, :] = jnp.full((1, 2 * blk), NEG_INF, F32)
        acc_ref[...] = jnp.zeros_like(acc_ref)

    def scores_diag(slot):
        nslab = 4
        rows = blk // nslab
        for hh, (w_ref, s0_ref, s1_ref, t0_ref, t1_ref, _, _) in enumerate(heads):
            s_ref, t_ref = ((s0_ref, t0_ref), (s1_ref, t1_ref))[slot]
            tmax = None
            for j in range(nslab):
                kt = k_ref[0, pl.ds(pl.multiple_of(qi * blk + j * rows, rows), rows),
                           hh * LANES:(hh + 1) * LANES]
                bias = bias_ref[hh, 0, j * rows:(j + 1) * rows, j * rows:blk]
                parts = []
                for i in range(2):
                    if j:
                        parts.append(jnp.full((rows, j * rows), NEG_INF, F32))
                    parts.append(jnp.dot(kt, w_ref[:, i * blk + j * rows:(i + 1) * blk],
                                         preferred_element_type=F32) + bias)
                s = jnp.concatenate(parts, axis=1)
                s_ref[j * rows:(j + 1) * rows, :] = s
                smax = jnp.max(s, 0, keepdims=True)
                tmax = smax if tmax is None else jnp.maximum(tmax, smax)
            t_ref[0:1, :] = tmax

    def accumulate(kb, slot, then=None):
        for hh, (_, s0_ref, s1_ref, t0_ref, t1_ref, acc_ref, ml_ref) in enumerate(heads):
            s_ref, t_ref = ((s0_ref, t0_ref), (s1_ref, t1_ref))[slot]
            m_old = ml_ref[0:1, :]
            m_new = jnp.maximum(m_old, t_ref[0:1, :])
            a = jnp.exp2(m_old - m_new)
            p = jnp.exp2(s_ref[...] - m_new)
            ml_ref[0:1, :] = m_new
            pb = p.astype(BF16)
            vt = jnp.concatenate([vt_ref[0, kb, hh * DIFF_DV:(hh + 1) * DIFF_DV, :], ones_rows], axis=0)
            for i in range(2):
                cols = slice(i * blk, (i + 1) * blk)
                acc_ref[i] = a[:, cols] * acc_ref[i] + jnp.dot(vt, pb[:, cols], preferred_element_type=F32)
            if then is not None:
                then(hh)

    def fused(kb_a, slot_a, kb_b, slot_b, bias_idx=None, extra=None, diag_b=False):
        nslab = 4
        rows = blk // nslab
        width = 2 * blk // nslab
        for hh, (w_ref, s0_ref, s1_ref, t0_ref, t1_ref, acc_ref, ml_ref) in enumerate(heads):
            sa_ref, ta_ref = ((s0_ref, t0_ref), (s1_ref, t1_ref))[slot_a]
            sb_ref, tb_ref = ((s0_ref, t0_ref), (s1_ref, t1_ref))[slot_b]
            m_old = ml_ref[0:1, :]
            m_new = jnp.maximum(m_old, tb_ref[0:1, :])
            a = jnp.exp2(m_old - m_new)
            ml_ref[0:1, :] = m_new
            vt = jnp.concatenate([vt_ref[0, kb_b, hh * DIFF_DV:(hh + 1) * DIFF_DV, :], ones_rows], axis=0)
            tmax = None
            for j in range(nslab):
                cols = slice(j * width, (j + 1) * width)
                keys = blk // 2 if (diag_b and j % 2 == 0) else blk
                p = jnp.exp2(sb_ref[:keys, cols] - m_new[:, cols])
                i, c0 = divmod(j * width, blk)
                acc_ref[i, :, c0:c0 + width] = (a[:, cols] * acc_ref[i, :, c0:c0 + width]
                                                + jnp.dot(vt[:, :keys], p.astype(BF16),
                                                          preferred_element_type=F32))
                kt = k_ref[0, pl.ds(pl.multiple_of(kb_a * blk + j * rows, rows), rows),
                           hh * LANES:(hh + 1) * LANES]
                s = jnp.dot(kt, w_ref[...], preferred_element_type=F32)
                if bias_idx is not None:
                    bias = bias_ref[hh, bias_idx, j * rows:(j + 1) * rows, :]
                    s = s + jnp.concatenate([bias, bias], axis=1)
                if extra is not None:
                    s = s + extra
                sa_ref[j * rows:(j + 1) * rows, :] = s
                smax = jnp.max(s, 0, keepdims=True)
                tmax = smax if tmax is None else jnp.maximum(tmax, smax)
            ta_ref[0:1, :] = tmax

    scores_diag(0)
    fused(prev_block, 1, qi, 0, bias_idx=1, extra=jnp.where(qi >= 1, 0.0, -jnp.inf).astype(F32), diag_b=True)

    def pair(j):
        kb = 2 * j
        fused(kb, 0, jnp.where(j == 0, prev_block, kb - 1), 1)
        fused(kb + 1, 1, kb, 0)

    def two_pairs(i, carry):
        pair(2 * i)
        pair(2 * i + 1)
        return carry

    n_pairs = n_far // 2
    lax.fori_loop(0, n_pairs // 2, two_pairs, 0)

    @pl.when(n_pairs % 2 == 1)
    def _():
        pair(n_pairs - 1)

    last_slot1 = jnp.where(n_pairs == 0, prev_block, 2 * n_pairs - 1)

    def finish(hh):
        acc_ref = heads[hh][5]
        lam = (jnp.exp(jnp.sum(lam_ref[0:1, :] * lam_ref[1:2, :], keepdims=True))
               - jnp.exp(jnp.sum(lam_ref[2:3, :] * lam_ref[3:4, :], keepdims=True)) + lambda_init)
        o1, o2 = acc_ref[0], acc_ref[1]
        ot = (o1[:DIFF_DV] / o1[DIFF_DV:DIFF_DV + 1] - lam * (o2[:DIFF_DV] / o2[DIFF_DV:DIFF_DV + 1]))
        ot = ot * lax.rsqrt(jnp.mean(ot * ot, 0, keepdims=True) + LN_EPS) * g_ref[...] * (1.0 - lambda_init)
        o_ref[0, :, hh * DIFF_DV:(hh + 1) * DIFF_DV] = ot.T.astype(BF16)

    @pl.when(n_far % 2 == 1)
    def _():
        fused(n_far - 1, 0, last_slot1, 1)
        accumulate(n_far - 1, 0, then=finish)

    @pl.when(n_far % 2 == 0)
    def _():
        accumulate(last_slot1, 1, then=finish)


def _diff_attention(qt, k12, vt, bias, lam4, g_sub, lambda_init):
    B, nqb, _, _ = qt.shape
    S = k12.shape[1]
    kernel = functools.partial(_attn_kernel, lambda_init=lambda_init)
    once = pl.Buffered(1)
    head_scratch = [pltpu.VMEM((2 * DIFF_DH, 2 * ATT_BLK), BF16),
                    pltpu.VMEM((ATT_BLK, 2 * ATT_BLK), F32), pltpu.VMEM((ATT_BLK, 2 * ATT_BLK), F32),
                    pltpu.VMEM((8, 2 * ATT_BLK), F32), pltpu.VMEM((8, 2 * ATT_BLK), F32),
                    pltpu.VMEM((2, DIFF_DV + ATT_SUM_ROWS, ATT_BLK), F32), pltpu.VMEM((8, 2 * ATT_BLK), F32)]
    return pl.pallas_call(
        kernel,
        grid=(B, DIFF_HEADS // ATT_HPS, nqb),
        in_specs=[
            _full((4, DIFF_DH)),
            pl.BlockSpec((1, 1, ATT_HPS * 2 * DIFF_DH, ATT_BLK), lambda b, h, i: (b, i, h, 0)),
            pl.BlockSpec((1, S, ATT_HPS * 2 * DIFF_DH), lambda b, h, i: (b, 0, h), pipeline_mode=once),
            pl.BlockSpec((1, nqb, ATT_HPS * DIFF_DV, ATT_BLK), lambda b, h, i: (b, 0, h, 0),
                         pipeline_mode=once),
            pl.BlockSpec((ATT_HPS, 2, ATT_BLK, ATT_BLK), lambda b, h, i: (h, 0, 0, 0), pipeline_mode=once),
            _full((DIFF_DV, 1)),
        ],
        out_specs=pl.BlockSpec((1, ATT_BLK, ATT_HPS * DIFF_DV), lambda b, h, i: (b, i, h)),
        out_shape=jax.ShapeDtypeStruct((B, S, DIFF_HEADS * DIFF_DV), BF16),
        scratch_shapes=head_scratch * ATT_HPS,
        compiler_params=pltpu.CompilerParams(dimension_semantics=("parallel", "parallel", "arbitrary"),
                                             vmem_limit_bytes=VMEM_LIMIT),
        name="diff_attn",
    )(lam4, qt, k12, vt, bias, g_sub.reshape(-1, 1))


def _proj_ln_kernel(h_ref, a_ref, w_ref, lng_ref, lnb_ref, o_ref):
    npiece = 2
    piece = h_ref.shape[0] // npiece
    for c in range(npiece):
        rows = slice(c * piece, (c + 1) * piece)
        mix = jnp.dot(a_ref[rows, :], w_ref[...], preferred_element_type=F32)
        o_ref[rows, :] = _layer_norm(DEEPNORM_ALPHA * h_ref[rows, :] + mix, lng_ref[...], lnb_ref[...])


def _proj_ln(h, a, w_out, ln_g, ln_b, *, tb=1024):
    T, D = h.shape
    return pl.pallas_call(
        _proj_ln_kernel,
        grid=(T // tb,),
        in_specs=[pl.BlockSpec((tb, D), lambda i: (i, 0)), pl.BlockSpec((tb, a.shape[1]), lambda i: (i, 0)),
                  _full(w_out.shape), _full((1, D)), _full((1, D))],
        out_specs=pl.BlockSpec((tb, D), lambda i: (i, 0)),
        out_shape=jax.ShapeDtypeStruct((T, D), F32),
        compiler_params=pltpu.CompilerParams(dimension_semantics=("parallel",), vmem_limit_bytes=VMEM_LIMIT),
        name="attn_out_ln",
    )(h, a, w_out.astype(BF16), ln_g.reshape(1, -1), ln_b.reshape(1, -1))


def kernel(x, a_w_in, a_w_gate2, a_b_gate, a_g_norm, a_w_out, kv_w, b_w_q, b_lam_q1, b_lam_k1, b_lam_q2, b_lam_k2, b_g_sub, b_w_out, rel_table, moe_w_group, moe_b_group, moe_w_router, moe_b_router, moe_w_gate, moe_w_up, moe_w_down, ln_g, ln_b):
    B, S, D = x.shape
    h = x
    bias = None
    k12 = vt = None
    for layer in range(DEPTH):
        if layer < N_A_LAYERS:
            h = _gla_layer(h, a_w_in[layer], a_w_gate2[layer], a_b_gate[layer], a_g_norm[layer],
                           a_w_out[layer], ln_g[layer, 0], ln_b[layer, 0])
        else:
            j = layer - N_A_LAYERS
            lambda_init = 0.8 - 0.6 * math.exp(-0.3 * layer)
            if bias is None:
                bias = _bias_tiles(rel_table)
            qt = _q_proj(h, b_w_q[j])
            lam4 = jnp.stack([b_lam_q1[j], b_lam_k1[j], b_lam_q2[j], b_lam_k2[j]])
            a = _diff_attention(qt, k12, vt, bias, lam4, b_g_sub[j], lambda_init)
            h = _proj_ln(h.reshape(B * S, D), a.reshape(B * S, -1), b_w_out[j], ln_g[layer, 0],
                         ln_b[layer, 0]).reshape(B, S, D)
        h = _moe_layer(h.reshape(B * S, D), moe_w_group[layer], moe_b_group[layer], moe_w_router[layer],
                       moe_b_router[layer], moe_w_gate[layer], moe_w_up[layer], moe_w_down[layer],
                       ln_g[layer, 1], ln_b[layer, 1]).reshape(B, S, D)
        if layer == N_A_LAYERS - 1:
            k12, vt = _kv_proj(h, kv_w)
    return h
```

```python
import functools
import math

import jax
import jax.numpy as jnp
from jax import lax
from jax.experimental import pallas as pl
from jax.experimental.pallas import tpu as pltpu

F32 = jnp.float32
BF16 = jnp.bfloat16

D_MODEL = 1024
DEPTH = 4
CHUNK = 64
N_A_LAYERS = 2

GLA_HEADS = 4
GLA_DK = 128
GLA_DV = 256
GLA_GATE_RANK = 16
GLA_GATE_TAU = 16.0
GLA_HK = GLA_HEADS * GLA_DK
GLA_HV = GLA_HEADS * GLA_DV

DIFF_HEADS = 8
DIFF_DH = 64
DIFF_DV = 128
DIFF_QK = DIFF_HEADS * DIFF_DH

REL_BUCKETS = 32

MOE_GROUPS = 4
MOE_EPG = 4
MOE_EXPERTS = 16
MOE_FF = 512

DEEPNORM_ALPHA = (2.0 * DEPTH) ** 0.25
LN_EPS = 1e-5
NEG_INF = -1e30

LANES = 128
ATT_BLK = 512
ATT_HPS = 2
ATT_SUM_ROWS = 16
LOG2E = math.log2(math.e)
VMEM_LIMIT = 48 * 1024 * 1024
MOE_VMEM_LIMIT = 58 * 1024 * 1024
MOE_SUB = 128
MOE_EPS = 2
MOE_POS_LANE = 12
MOE_TR_ROWS = 16


def _layer_norm(y, g, b):
    mu = jnp.mean(y, -1, keepdims=True)
    d = y - mu
    var = jnp.mean(d * d, -1, keepdims=True)
    return d * lax.rsqrt(var + LN_EPS) * g + b


def _sigmoid(x):
    return 1.0 / (1.0 + jnp.exp(-x))


def _full(shape):
    return pl.BlockSpec(shape, lambda *_: (0,) * len(shape))


def _gla_kernel(x_ref, w_main_ref, w_kt_ref, w_lr_ref, w_g2_ref, b_g_ref, gn_ref, w_out_ref, lng_ref, lnb_ref,
                tri_ref, o_ref, st_ref, mix_ref):
    nb, tb, d = x_ref.shape
    nt = (((1,), (1,)), ((), ()))

    @pl.when(pl.program_id(0) == 0)
    def _():
        st_ref[...] = jnp.zeros_like(st_ref)

    x = x_ref[...].reshape(nb * tb, d)
    xb = x.astype(BF16)
    lrt = lax.dot_general(w_lr_ref[...], xb, nt, preferred_element_type=F32)
    kt = lax.dot_general(w_kt_ref[...], xb, nt, preferred_element_type=F32)
    hq = jnp.dot(xb, w_main_ref[:, :GLA_HK], preferred_element_type=F32)
    glog = jnp.dot(w_g2_ref[...], lrt.astype(BF16), preferred_element_type=F32) + b_g_ref[...]
    hv = jnp.dot(xb, w_main_ref[:, GLA_HK:GLA_HK + GLA_HV], preferred_element_type=F32)
    la = (jnp.minimum(glog, 0.0) - jnp.log(1.0 + jnp.exp(-jnp.abs(glog)))) * (1.0 / GLA_GATE_TAU)
    la_hi = la.astype(BF16)
    la_lo = (la - la_hi.astype(F32)).astype(BF16)
    tri = tri_ref[...]
    cum = [jnp.dot(la_hi[:, b * tb:(b + 1) * tb], tri, preferred_element_type=F32)
           + jnp.dot(la_lo[:, b * tb:(b + 1) * tb], tri, preferred_element_type=F32) for b in range(nb)]
    r = jnp.dot(xb, w_main_ref[:, GLA_HK + GLA_HV:], preferred_element_type=F32)

    lane = lax.broadcasted_iota(jnp.int32, (GLA_HK, 2 * CHUNK), 1)
    halves = (lane < CHUNK, lane >= CHUNK)
    heads = [(slice(h * GLA_DK, (h + 1) * GLA_DK), slice(h * GLA_DV, (h + 1) * GLA_DV)) for h in range(GLA_HEADS)]
    for pair in range(tb // (2 * CHUNK)):
        c0 = pair * 2 * CHUNK
        dec, upd = {}, {}
        for b in range(nb):
            t0 = b * tb + c0
            cum_p = cum[b][:, c0:c0 + 2 * CHUNK]
            last = (cum_p[:, CHUNK - 1:CHUNK], cum_p[:, 2 * CHUNK - 1:2 * CHUNK])
            kdec = kt[:, t0:t0 + 2 * CHUNK] * jnp.exp(jnp.where(halves[0], last[0], last[1]) - cum_p)
            v_pair = hv[t0:t0 + 2 * CHUNK].astype(BF16)
            for half in range(2):
                kd = jnp.where(halves[half], kdec, 0.0).astype(BF16)
                dec[b, half] = jnp.exp(last[half])
                for h, (ks, vs) in enumerate(heads):
                    upd[b, half, h] = jnp.dot(kd[ks, :], v_pair[:, vs], preferred_element_type=F32)
        for half in range(2):
            for b in range(nb):
                r0 = b * tb + c0 + half * CHUNK
                qc = (hq[r0:r0 + CHUNK] * (GLA_DK ** -0.5)).astype(BF16)
                for h, (ks, vs) in enumerate(heads):
                    st = st_ref[b, h] * dec[b, half][ks, :] + upd[b, half, h]
                    st_ref[b, h] = st
                    o = jnp.dot(qc[:, ks], st.astype(BF16), preferred_element_type=F32)
                    o = o * lax.rsqrt(jnp.mean(o * o, -1, keepdims=True) + LN_EPS) * gn_ref[:, vs]
                    mix_ref[r0:r0 + CHUNK, vs] = o

    gated = mix_ref[...] * (r * _sigmoid(r))
    mix = jnp.dot(gated.astype(BF16), w_out_ref[...], preferred_element_type=F32)
    o_ref[...] = _layer_norm(DEEPNORM_ALPHA * x + mix, lng_ref[...], lnb_ref[...]).reshape(nb, tb, d)


def _gla_layer(h, w_in, w_gate2, b_gate, g_norm, w_out, ln_g, ln_b, *, tb=256):
    B, S, D = h.shape
    n_k = 2 * GLA_HK
    n_main = GLA_HK + 2 * GLA_HV
    n_all = 2 * GLA_HK + 2 * GLA_HV
    w_main = jnp.concatenate([w_in[:, :GLA_HK], w_in[:, n_k:n_all]], axis=1).astype(BF16)
    w_kt = w_in[:, GLA_HK:n_k].T.astype(BF16)
    w_lr = jnp.zeros((LANES, D), F32).at[:GLA_GATE_RANK].set(w_in[:, n_all:].T).astype(BF16)
    w_g2 = jnp.zeros((GLA_HK, LANES), F32).at[:, :GLA_GATE_RANK].set(w_gate2.T).astype(BF16)
    idx = jnp.arange(tb)
    tri = ((idx[:, None] <= idx[None, :]) & (idx[:, None] // CHUNK == idx[None, :] // CHUNK)).astype(BF16)
    return pl.pallas_call(
        _gla_kernel,
        grid=(S // tb,),
        in_specs=[
            pl.BlockSpec((B, tb, D), lambda j: (0, j, 0)),
            _full((D, n_main)), _full((GLA_HK, D)), _full((LANES, D)), _full((GLA_HK, LANES)),
            _full((GLA_HK, 1)), _full((1, GLA_HV)), _full((GLA_HV, D)), _full((1, D)), _full((1, D)),
            _full((tb, tb)),
        ],
        out_specs=pl.BlockSpec((B, tb, D), lambda j: (0, j, 0)),
        out_shape=jax.ShapeDtypeStruct((B, S, D), F32),
        scratch_shapes=[pltpu.VMEM((B, GLA_HEADS, GLA_DK, GLA_DV), F32), pltpu.VMEM((B * tb, GLA_HV), F32)],
        compiler_params=pltpu.CompilerParams(dimension_semantics=("arbitrary",),
                                             vmem_limit_bytes=VMEM_LIMIT),
        name="gla_layer",
    )(h, w_main, w_kt, w_lr, w_g2, b_gate.reshape(-1, 1), g_norm.reshape(1, -1), w_out.astype(BF16),
      ln_g.reshape(1, -1), ln_b.reshape(1, -1), tri)


def _first_max(vals):
    top = functools.reduce(jnp.maximum, vals)
    idx = jnp.full_like(top, float(len(vals) - 1))
    for j in range(len(vals) - 2, -1, -1):
        idx = jnp.where(vals[j] == top, float(j), idx)
    return top, idx


def _route(lt):
    gl = [lt[g:g + 1, :] for g in range(MOE_GROUPS)]
    gmax, gidx = _first_max(gl)
    gw = 1.0 / functools.reduce(jnp.add, [jnp.exp(v - gmax) for v in gl])
    el = []
    for j in range(MOE_EPG):
        rows = [lt[MOE_GROUPS + g * MOE_EPG + j:MOE_GROUPS + g * MOE_EPG + j + 1, :] for g in range(MOE_GROUPS)]
        sel = rows[MOE_GROUPS - 1]
        for g in range(MOE_GROUPS - 2, -1, -1):
            sel = jnp.where(gidx == float(g), rows[g], sel)
        el.append(sel)
    emax = functools.reduce(jnp.maximum, el)
    pe = [jnp.exp(v - emax) for v in el]
    psum = functools.reduce(jnp.add, pe)
    prob = [v / psum for v in pe]
    p1, i1 = _first_max(prob)
    p2, i2 = _first_max([jnp.where(i1 == float(j), -1.0, prob[j]) for j in range(MOE_EPG)])
    den = p1 + p2
    weights = [(jnp.where(i1 == float(j), p1 / den, 0.0) + jnp.where(i2 == float(j), p2 / den, 0.0)) * gw
               for j in range(MOE_EPG)]
    return gidx, weights


def _split3(v):
    hi = v.astype(BF16).astype(F32)
    mid = (v - hi).astype(BF16).astype(F32)
    lo = (v - hi - mid).astype(BF16).astype(F32)
    return hi, mid, lo


def _router_logits(x, wr_ref, br_ref):
    tb = x.shape[0]
    xh = x.astype(BF16)
    xl = (x - xh.astype(F32)).astype(BF16)
    x2 = jnp.concatenate([xh, xl], axis=1)
    half = tb // 2
    both = jnp.concatenate([jnp.dot(x2[:half], wr_ref[...], preferred_element_type=F32),
                            jnp.dot(x2[half:], wr_ref[...], preferred_element_type=F32)], axis=0)
    return both[:, :LANES] + both[:, LANES:] + br_ref[...]


def _moe_route(logits, tr_ref, posr_ref, cwn_ref, meta_ref, meta_base):
    tb = logits.shape[0]
    gidx, weights = _route(logits.T)
    onehot = [jnp.where(gidx == float(g), 1.0, 0.0) for g in range(MOE_GROUPS)]
    oht = jnp.concatenate(onehot + [jnp.zeros((MOE_TR_ROWS - MOE_GROUPS, tb), F32)], axis=0)
    ohb = oht.astype(BF16)
    earlier = jnp.where(lax.broadcasted_iota(jnp.int32, (LANES, LANES), 0)
                        < lax.broadcasted_iota(jnp.int32, (LANES, LANES), 1), 1.0, 0.0).astype(BF16)
    before, carry = [], jnp.zeros((MOE_TR_ROWS, 1), F32)
    for piece in range(tb // LANES):
        cols = slice(piece * LANES, (piece + 1) * LANES)
        before.append(jnp.dot(ohb[:, cols], earlier, preferred_element_type=F32) + carry)
        carry = carry + jnp.sum(oht[:, cols], -1, keepdims=True)
    rank = jnp.sum(jnp.concatenate(before, axis=1) * oht, 0, keepdims=True)
    ntile = jnp.floor((carry + (MOE_SUB - 1.0)) * (1.0 / MOE_SUB))
    pos, first_row = rank, jnp.zeros((1, 1), F32)
    for g in range(MOE_GROUPS):
        meta_ref[meta_base + g] = jnp.sum(ntile[g:g + 1, :]).astype(jnp.int32)
        meta_ref[meta_base + MOE_GROUPS + g] = jnp.sum(first_row).astype(jnp.int32)
        pos = pos + onehot[g] * first_row
        first_row = first_row + ntile[g:g + 1, :] * MOE_SUB
    posr_ref[0:1, :] = pos
    for j in range(MOE_EPG):
        for k, part in enumerate(_split3(weights[j])):
            tr_ref[j + k * MOE_EPG:j + k * MOE_EPG + 1, :] = part
    tr_ref[MOE_POS_LANE:MOE_POS_LANE + 1, :] = pos
    tr_ref[MOE_POS_LANE + 1:MOE_TR_ROWS, :] = jnp.zeros((MOE_TR_ROWS - MOE_POS_LANE - 1, tb), F32)
    cwn_ref[...] = tr_ref[...].T


def _moe_kernel(x_ref, wr_ref, br_ref, wg_ref, wu_ref, wd_ref, lng_ref, lnb_ref, o_ref,
                xs_ref, cw_ref, pt_ref, y_ref, res_ref, tr_ref, posr_ref, cwn_ref, meta_ref):
    blk_i = pl.program_id(0)
    step = pl.program_id(1)
    steps_per_group = MOE_EPG // MOE_EPS
    grp = step // steps_per_group
    j0 = (step % steps_per_group) * MOE_EPS
    tb = x_ref.shape[0]
    tbs = xs_ref.shape[0]
    d = x_ref.shape[1]
    meta_cur = (blk_i % 2) * (2 * MOE_GROUPS)
    meta_next = 2 * MOE_GROUPS - meta_cur

    @pl.when(step == 0)
    def _():
        @pl.when(blk_i == 0)
        def _():
            tr_ref[...] = jnp.zeros_like(tr_ref)
            _moe_route(_router_logits(x_ref[...], wr_ref, br_ref), tr_ref, posr_ref, cwn_ref, meta_ref, meta_cur)

        x = x_ref[...]
        res_ref[...] = x
        cwn = cwn_ref[...]
        pos_col = cwn[:, MOE_POS_LANE:MOE_POS_LANE + 1].astype(jnp.int32)
        pt_ref[...] = jnp.where(lax.broadcasted_iota(jnp.int32, (tb, tbs), 1) == pos_col, 1.0, 0.0).astype(BF16)
        pos_row = posr_ref[0:1, :].astype(jnp.int32)
        perm = jnp.where(lax.broadcasted_iota(jnp.int32, (tbs, tb), 0) == pos_row, 1.0, 0.0).astype(BF16)
        sorted_all = jnp.dot(perm, jnp.concatenate([x.astype(BF16), cwn.astype(BF16)], axis=1),
                             preferred_element_type=F32)
        xs_ref[...] = sorted_all[:, :d].astype(BF16)
        cw_ref[...] = sorted_all[:, d:]
        y_ref[...] = jnp.zeros_like(y_ref)

    base = meta_ref[meta_cur + MOE_GROUPS + grp]

    def experts(start, nrows):
        rows = pl.ds(pl.multiple_of(start, MOE_SUB), nrows)
        xt = xs_ref[rows, :]
        cwt = cw_ref[rows, :]
        lane = lax.broadcasted_iota(jnp.int32, (nrows, LANES), 1)
        hidden = []
        for k in range(MOE_EPS):
            gt = jnp.dot(xt, wg_ref[k], preferred_element_type=F32)
            ut = jnp.dot(xt, wu_ref[k], preferred_element_type=F32)
            hidden.append(((gt * _sigmoid(gt)) * ut).astype(BF16))
        total = None
        for k in range(MOE_EPS):
            y = jnp.dot(hidden[k], wd_ref[k], preferred_element_type=F32)
            mine = (lane % MOE_EPG == j0 + k) & (lane < 3 * MOE_EPG)
            c = jnp.sum(jnp.where(mine, cwt, 0.0), -1, keepdims=True)
            total = c * y if total is None else total + c * y
        y_ref[rows, :] += total

    n_tiles = meta_ref[meta_cur + grp]

    def wide(i, carry):
        experts(base + i * (2 * MOE_SUB), 2 * MOE_SUB)
        return carry

    lax.fori_loop(0, n_tiles // 2, wide, 0)

    @pl.when(n_tiles % 2 == 1)
    def _():
        experts(base + (n_tiles - 1) * MOE_SUB, MOE_SUB)

    @pl.when(step == MOE_EXPERTS // MOE_EPS - 1)
    def _():
        logits = _router_logits(x_ref[...], wr_ref, br_ref)
        yb = y_ref[...].astype(BF16)
        npiece = 4
        piece = tb // npiece
        for c in range(npiece):
            if c == npiece - 1:
                _moe_route(logits, tr_ref, posr_ref, cwn_ref, meta_ref, meta_next)
            rows = slice(c * piece, (c + 1) * piece)
            ffn = jnp.dot(pt_ref[rows, :], yb, preferred_element_type=F32)
            o_ref[rows, :] = _layer_norm(DEEPNORM_ALPHA * res_ref[rows, :] + ffn, lng_ref[...], lnb_ref[...])


def _moe_layer(h, w_group, b_group, w_router, b_router, w_gate, w_up, w_down, ln_g, ln_b, *, tb=1024):
    T, D = h.shape
    wr = jnp.zeros((D, LANES), F32).at[:, :MOE_GROUPS].set(w_group)
    wr = wr.at[:, MOE_GROUPS:MOE_GROUPS + MOE_EXPERTS].set(w_router)
    wrh = wr.astype(BF16)
    wrl = (wr - wrh.astype(F32)).astype(BF16)
    wr2 = jnp.concatenate([jnp.concatenate([wrh, wrl], axis=1),
                           jnp.concatenate([wrh, jnp.zeros_like(wrl)], axis=1)], axis=0)
    br = jnp.zeros((1, LANES), F32).at[0, :MOE_GROUPS].set(b_group)
    br = br.at[0, MOE_GROUPS:MOE_GROUPS + MOE_EXPERTS].set(b_router.reshape(-1))
    tbs = tb + (MOE_GROUPS - 1) * MOE_SUB
    nblk = T // tb
    return pl.pallas_call(
        _moe_kernel,
        grid=(nblk, MOE_EXPERTS // MOE_EPS),
        in_specs=[
            pl.BlockSpec((tb, D), lambda i, e: (jnp.where(e == 0, i, jnp.minimum(i + 1, nblk - 1)), 0)),
            _full((2 * D, 2 * LANES)), _full((1, LANES)),
            pl.BlockSpec((MOE_EPS, D, MOE_FF), lambda i, e: (e, 0, 0)),
            pl.BlockSpec((MOE_EPS, D, MOE_FF), lambda i, e: (e, 0, 0)),
            pl.BlockSpec((MOE_EPS, MOE_FF, D), lambda i, e: (e, 0, 0)),
            _full((1, D)), _full((1, D)),
        ],
        out_specs=pl.BlockSpec((tb, D), lambda i, e: (i, 0)),
        out_shape=jax.ShapeDtypeStruct((T, D), F32),
        scratch_shapes=[pltpu.VMEM((tbs, D), BF16), pltpu.VMEM((tbs, LANES), F32), pltpu.VMEM((tb, tbs), BF16),
                        pltpu.VMEM((tbs, D), F32), pltpu.VMEM((tb, D), F32), pltpu.VMEM((LANES, tb), F32),
                        pltpu.VMEM((8, tb), F32), pltpu.VMEM((tb, LANES), F32),
                        pltpu.SMEM((4 * MOE_GROUPS,), jnp.int32)],
        compiler_params=pltpu.CompilerParams(dimension_semantics=("arbitrary", "arbitrary"),
                                             vmem_limit_bytes=MOE_VMEM_LIMIT),
        name="moe_layer",
    )(h, wr2, br, w_gate.astype(BF16), w_up.astype(BF16), w_down.astype(BF16),
      ln_g.reshape(1, -1), ln_b.reshape(1, -1))


def _kv_kernel(x_ref, wk_ref, wvt_ref, k_ref, vt_ref):
    xb = x_ref[0].astype(BF16)
    k_ref[0] = jnp.dot(xb, wk_ref[...], preferred_element_type=F32).astype(BF16)
    vt = lax.dot_general(wvt_ref[...], xb, (((1,), (1,)), ((), ())), preferred_element_type=F32)
    for j in range(vt_ref.shape[1]):
        vt_ref[0, j] = vt[:, j * ATT_BLK:(j + 1) * ATT_BLK].astype(BF16)


def _kv_proj(h, kv_w, *, ts=1024):
    B, S, D = h.shape
    k1 = kv_w[:, :DIFF_QK].reshape(D, DIFF_HEADS, DIFF_DH)
    k2 = kv_w[:, DIFF_QK:2 * DIFF_QK].reshape(D, DIFF_HEADS, DIFF_DH)
    wk = jnp.stack([k1, k2], axis=2).reshape(D, 2 * DIFF_QK).astype(BF16)
    wvt = kv_w[:, 2 * DIFF_QK:].T.astype(BF16)
    nv = DIFF_HEADS * DIFF_DV
    nj = ts // ATT_BLK
    return pl.pallas_call(
        _kv_kernel,
        grid=(B, S // ts),
        in_specs=[pl.BlockSpec((1, ts, D), lambda b, j: (b, j, 0)), _full((D, 2 * DIFF_QK)), _full((nv, D))],
        out_specs=[pl.BlockSpec((1, ts, 2 * DIFF_QK), lambda b, j: (b, j, 0)),
                   pl.BlockSpec((1, nj, nv, ATT_BLK), lambda b, j: (b, j, 0, 0))],
        out_shape=[jax.ShapeDtypeStruct((B, S, 2 * DIFF_QK), BF16),
                   jax.ShapeDtypeStruct((B, S // ATT_BLK, nv, ATT_BLK), BF16)],
        compiler_params=pltpu.CompilerParams(dimension_semantics=("parallel", "parallel"),
                                             vmem_limit_bytes=VMEM_LIMIT),
        name="kv_proj",
    )(h, wk, wvt)


def _q_kernel(x_ref, wqt_ref, qt_ref):
    xb = x_ref[0].astype(BF16)
    qt = lax.dot_general(wqt_ref[...], xb, (((1,), (1,)), ((), ())), preferred_element_type=F32)
    qt = qt * (DIFF_DH ** -0.5 * LOG2E)
    for j in range(qt_ref.shape[1]):
        qt_ref[0, j] = qt[:, j * ATT_BLK:(j + 1) * ATT_BLK].astype(BF16)


def _q_proj(h, w_q, *, ts=1024):
    B, S, D = h.shape
    q1 = w_q[:, :DIFF_QK].reshape(D, DIFF_HEADS, DIFF_DH)
    q2 = w_q[:, DIFF_QK:].reshape(D, DIFF_HEADS, DIFF_DH)
    wqt = jnp.stack([q1, q2], axis=2).reshape(D, 2 * DIFF_QK).T.astype(BF16)
    nq = 2 * DIFF_QK
    nj = ts // ATT_BLK
    return pl.pallas_call(
        _q_kernel,
        grid=(B, S // ts),
        in_specs=[pl.BlockSpec((1, ts, D), lambda b, j: (b, j, 0)), _full((nq, D))],
        out_specs=pl.BlockSpec((1, nj, nq, ATT_BLK), lambda b, j: (b, j, 0, 0)),
        out_shape=jax.ShapeDtypeStruct((B, S // ATT_BLK, nq, ATT_BLK), BF16),
        compiler_params=pltpu.CompilerParams(dimension_semantics=("parallel", "parallel"),
                                             vmem_limit_bytes=VMEM_LIMIT),
        name="q_proj",
    )(h, wqt)


def _bias_kernel(tab_ref, o_ref):
    h = pl.program_id(0)
    kk = lax.broadcasted_iota(jnp.int32, (ATT_BLK, ATT_BLK), 0)
    qq = lax.broadcasted_iota(jnp.int32, (ATT_BLK, ATT_BLK), 1)
    nb = REL_BUCKETS // 2
    far = tab_ref[nb - 1, h]
    for t in range(2):
        rel = kk - qq - t * ATT_BLK
        n = jnp.abs(rel)
        large = nb // 2
        for thr in (12, 16, 23, 32, 46, 64, 91):
            large = large + (n >= thr).astype(jnp.int32)
        bucket = jnp.where(rel > 0, nb, 0) + jnp.where(n < nb // 2, n, large)
        bias = jnp.zeros((ATT_BLK, ATT_BLK), F32)
        for b in range(REL_BUCKETS):
            bias = jnp.where(bucket == b, tab_ref[b, h], bias)
        bias = (bias - far) * LOG2E
        if t == 0:
            visible = (kk // CHUNK) <= (qq // CHUNK)
            bias = jnp.where(visible, bias, NEG_INF)
        o_ref[0, t] = bias


def _bias_tiles(rel_table):
    return pl.pallas_call(
        _bias_kernel,
        grid=(DIFF_HEADS,),
        in_specs=[pl.BlockSpec(memory_space=pltpu.SMEM)],
        out_specs=pl.BlockSpec((1, 2, ATT_BLK, ATT_BLK), lambda h: (h, 0, 0, 0)),
        out_shape=jax.ShapeDtypeStruct((DIFF_HEADS, 2, ATT_BLK, ATT_BLK), F32),
        name="bias_tiles",
    )(rel_table)


def _attn_kernel(lam_ref, qt_ref, k_ref, vt_ref, bias_ref, g_ref, o_ref, *scratch, lambda_init):
    blk = ATT_BLK
    qi = pl.program_id(2)
    n_far = jnp.maximum(qi - 1, 0)
    prev_block = jnp.maximum(qi - 1, 0)
    heads = [scratch[hh * 7:(hh + 1) * 7] for hh in range(ATT_HPS)]
    z = jnp.zeros((DIFF_DH, blk), BF16)
    ones_rows = jnp.ones((ATT_SUM_ROWS, blk), BF16)
    for hh, (w_ref, _, _, _, _, acc_ref, ml_ref) in enumerate(heads):
        qt = qt_ref[0, 0, hh * 2 * DIFF_DH:(hh + 1) * 2 * DIFF_DH, :]
        w_ref[...] = jnp.concatenate(
            [jnp.concatenate([qt[:DIFF_DH], z], 0), jnp.concatenate([z, qt[DIFF_DH:]], 0)], 1)
        ml_ref[0:1, :] = jnp.full((1, 2 * blk), NEG_INF, F32)
        acc_ref[...] = jnp.zeros_like(acc_ref)

    def scores_diag(slot):
        nslab = 4
        rows = blk // nslab
        for hh, (w_ref, s0_ref, s1_ref, t0_ref, t1_ref, _, _) in enumerate(heads):
            s_ref, t_ref = ((s0_ref, t0_ref), (s1_ref, t1_ref))[slot]
            tmax = None
            for j in range(nslab):
                kt = k_ref[0, pl.ds(pl.multiple_of(qi * blk + j * rows, rows), rows),
                           hh * LANES:(hh + 1) * LANES]
                bias = bias_ref[hh, 0, j * rows:(j + 1) * rows, j * rows:blk]
                parts = []
                for i in range(2):
                    if j:
                        parts.append(jnp.full((rows, j * rows), NEG_INF, F32))
                    parts.append(jnp.dot(kt, w_ref[:, i * blk + j * rows:(i + 1) * blk],
                                         preferred_element_type=F32) + bias)
                s = jnp.concatenate(parts, axis=1)
                s_ref[j * rows:(j + 1) * rows, :] = s
                smax = jnp.max(s, 0, keepdims=True)
                tmax = smax if tmax is None else jnp.maximum(tmax, smax)
            t_ref[0:1, :] = tmax

    def accumulate(kb, slot, then=None):
        for hh, (_, s0_ref, s1_ref, t0_ref, t1_ref, acc_ref, ml_ref) in enumerate(heads):
            s_ref, t_ref = ((s0_ref, t0_ref), (s1_ref, t1_ref))[slot]
            m_old = ml_ref[0:1, :]
            m_new = jnp.maximum(m_old, t_ref[0:1, :])
            a = jnp.exp2(m_old - m_new)
            p = jnp.exp2(s_ref[...] - m_new)
            ml_ref[0:1, :] = m_new
            pb = p.astype(BF16)
            vt = jnp.concatenate([vt_ref[0, kb, hh * DIFF_DV:(hh + 1) * DIFF_DV, :], ones_rows], axis=0)
            for i in range(2):
                cols = slice(i * blk, (i + 1) * blk)
                acc_ref[i] = a[:, cols] * acc_ref[i] + jnp.dot(vt, pb[:, cols], preferred_element_type=F32)
            if then is not None:
                then(hh)

    def fused(kb_a, slot_a, kb_b, slot_b, bias_idx=None, extra=None, diag_b=False):
        nslab = 4
        rows = blk // nslab
        width = 2 * blk // nslab
        for hh, (w_ref, s0_ref, s1_ref, t0_ref, t1_ref, acc_ref, ml_ref) in enumerate(heads):
            sa_ref, ta_ref = ((s0_ref, t0_ref), (s1_ref, t1_ref))[slot_a]
            sb_ref, tb_ref = ((s0_ref, t0_ref), (s1_ref, t1_ref))[slot_b]
            m_old = ml_ref[0:1, :]
            m_new = jnp.maximum(m_old, tb_ref[0:1, :])
            a = jnp.exp2(m_old - m_new)
            ml_ref[0:1, :] = m_new
            vt = jnp.concatenate([vt_ref[0, kb_b, hh * DIFF_DV:(hh + 1) * DIFF_DV, :], ones_rows], axis=0)
            tmax = None
            for j in range(nslab):
                cols = slice(j * width, (j + 1) * width)
                keys = blk // 2 if (diag_b and j % 2 == 0) else blk
                p = jnp.exp2(sb_ref[:keys, cols] - m_new[:, cols])
                i, c0 = divmod(j * width, blk)
                acc_ref[i, :, c0:c0 + width] = (a[:, cols] * acc_ref[i, :, c0:c0 + width]
                                                + jnp.dot(vt[:, :keys], p.astype(BF16),
                                                          preferred_element_type=F32))
                kt = k_ref[0, pl.ds(pl.multiple_of(kb_a * blk + j * rows, rows), rows),
                           hh * LANES:(hh + 1) * LANES]
                s = jnp.dot(kt, w_ref[...], preferred_element_type=F32)
                if bias_idx is not None:
                    bias = bias_ref[hh, bias_idx, j * rows:(j + 1) * rows, :]
                    s = s + jnp.concatenate([bias, bias], axis=1)
                if extra is not None:
                    s = s + extra
                sa_ref[j * rows:(j + 1) * rows, :] = s
                smax = jnp.max(s, 0, keepdims=True)
                tmax = smax if tmax is None else jnp.maximum(tmax, smax)
            ta_ref[0:1, :] = tmax

    scores_diag(0)
    fused(prev_block, 1, qi, 0, bias_idx=1, extra=jnp.where(qi >= 1, 0.0, -jnp.inf).astype(F32), diag_b=True)

    def pair(j):
        kb = 2 * j
        fused(kb, 0, jnp.where(j == 0, prev_block, kb - 1), 1)
        fused(kb + 1, 1, kb, 0)

    def two_pairs(i, carry):
        pair(2 * i)
        pair(2 * i + 1)
        return carry

    n_pairs = n_far // 2
    lax.fori_loop(0, n_pairs // 2, two_pairs, 0)

    @pl.when(n_pairs % 2 == 1)
    def _():
        pair(n_pairs - 1)

    last_slot1 = jnp.where(n_pairs == 0, prev_block, 2 * n_pairs - 1)

    def finish(hh):
        acc_ref = heads[hh][5]
        lam = (jnp.exp(jnp.sum(lam_ref[0:1, :] * lam_ref[1:2, :], keepdims=True))
               - jnp.exp(jnp.sum(lam_ref[2:3, :] * lam_ref[3:4, :], keepdims=True)) + lambda_init)
        o1, o2 = acc_ref[0], acc_ref[1]
        ot = (o1[:DIFF_DV] / o1[DIFF_DV:DIFF_DV + 1] - lam * (o2[:DIFF_DV] / o2[DIFF_DV:DIFF_DV + 1]))
        ot = ot * lax.rsqrt(jnp.mean(ot * ot, 0, keepdims=True) + LN_EPS) * g_ref[...] * (1.0 - lambda_init)
        o_ref[0, :, hh * DIFF_DV:(hh + 1) * DIFF_DV] = ot.T.astype(BF16)

    @pl.when(n_far % 2 == 1)
    def _():
        fused(n_far - 1, 0, last_slot1, 1)
        accumulate(n_far - 1, 0, then=finish)

    @pl.when(n_far % 2 == 0)
    def _():
        accumulate(last_slot1, 1, then=finish)


def _diff_attention(qt, k12, vt, bias, lam4, g_sub, lambda_init):
    B, nqb, _, _ = qt.shape
    S = k12.shape[1]
    kernel = functools.partial(_attn_kernel, lambda_init=lambda_init)
    once = pl.Buffered(1)
    head_scratch = [pltpu.VMEM((2 * DIFF_DH, 2 * ATT_BLK), BF16),
                    pltpu.VMEM((ATT_BLK, 2 * ATT_BLK), F32), pltpu.VMEM((ATT_BLK, 2 * ATT_BLK), F32),
                    pltpu.VMEM((8, 2 * ATT_BLK), F32), pltpu.VMEM((8, 2 * ATT_BLK), F32),
                    pltpu.VMEM((2, DIFF_DV + ATT_SUM_ROWS, ATT_BLK), F32), pltpu.VMEM((8, 2 * ATT_BLK), F32)]
    return pl.pallas_call(
        kernel,
        grid=(B, DIFF_HEADS // ATT_HPS, nqb),
        in_specs=[
            _full((4, DIFF_DH)),
            pl.BlockSpec((1, 1, ATT_HPS * 2 * DIFF_DH, ATT_BLK), lambda b, h, i: (b, i, h, 0)),
            pl.BlockSpec((1, S, ATT_HPS * 2 * DIFF_DH), lambda b, h, i: (b, 0, h), pipeline_mode=once),
            pl.BlockSpec((1, nqb, ATT_HPS * DIFF_DV, ATT_BLK), lambda b, h, i: (b, 0, h, 0),
                         pipeline_mode=once),
            pl.BlockSpec((ATT_HPS, 2, ATT_BLK, ATT_BLK), lambda b, h, i: (h, 0, 0, 0), pipeline_mode=once),
            _full((DIFF_DV, 1)),
        ],
        out_specs=pl.BlockSpec((1, ATT_BLK, ATT_HPS * DIFF_DV), lambda b, h, i: (b, i, h)),
        out_shape=jax.ShapeDtypeStruct((B, S, DIFF_HEADS * DIFF_DV), BF16),
        scratch_shapes=head_scratch * ATT_HPS,
        compiler_params=pltpu.CompilerParams(dimension_semantics=("parallel", "parallel", "arbitrary"),
                                             vmem_limit_bytes=VMEM_LIMIT),
        name="diff_attn",
    )(lam4, qt, k12, vt, bias, g_sub.reshape(-1, 1))


def _proj_ln_kernel(h_ref, a_ref, w_ref, lng_ref, lnb_ref, o_ref):
    npiece = 2
    piece = h_ref.shape[0] // npiece
    for c in range(npiece):
        rows = slice(c * piece, (c + 1) * piece)
        mix = jnp.dot(a_ref[rows, :], w_ref[...], preferred_element_type=F32)
        o_ref[rows, :] = _layer_norm(DEEPNORM_ALPHA * h_ref[rows, :] + mix, lng_ref[...], lnb_ref[...])


def _proj_ln(h, a, w_out, ln_g, ln_b, *, tb=1024):
    T, D = h.shape
    return pl.pallas_call(
        _proj_ln_kernel,
        grid=(T // tb,),
        in_specs=[pl.BlockSpec((tb, D), lambda i: (i, 0)), pl.BlockSpec((tb, a.shape[1]), lambda i: (i, 0)),
                  _full(w_out.shape), _full((1, D)), _full((1, D))],
        out_specs=pl.BlockSpec((tb, D), lambda i: (i, 0)),
        out_shape=jax.ShapeDtypeStruct((T, D), F32),
        compiler_params=pltpu.CompilerParams(dimension_semantics=("parallel",), vmem_limit_bytes=VMEM_LIMIT),
        name="attn_out_ln",
    )(h, a, w_out.astype(BF16), ln_g.reshape(1, -1), ln_b.reshape(1, -1))


def kernel(x, a_w_in, a_w_gate2, a_b_gate, a_g_norm, a_w_out, kv_w, b_w_q, b_lam_q1, b_lam_k1, b_lam_q2, b_lam_k2, b_g_sub, b_w_out, rel_table, moe_w_group, moe_b_group, moe_w_router, moe_b_router, moe_w_gate, moe_w_up, moe_w_down, ln_g, ln_b):
    B, S, D = x.shape
    h = x
    bias = None
    k12 = vt = None
    for layer in range(DEPTH):
        if layer < N_A_LAYERS:
            h = _gla_layer(h, a_w_in[layer], a_w_gate2[layer], a_b_gate[layer], a_g_norm[layer],
                           a_w_out[layer], ln_g[layer, 0], ln_b[layer, 0])
        else:
            j = layer - N_A_LAYERS
            lambda_init = 0.8 - 0.6 * math.exp(-0.3 * layer)
            if bias is None:
                bias = _bias_tiles(rel_table)
            qt = _q_proj(h, b_w_q[j])
            lam4 = jnp.stack([b_lam_q1[j], b_lam_k1[j], b_lam_q2[j], b_lam_k2[j]])
            a = _diff_attention(qt, k12, vt, bias, lam4, b_g_sub[j], lambda_init)
            h = _proj_ln(h.reshape(B * S, D), a.reshape(B * S, -1), b_w_out[j], ln_g[layer, 0],
                         ln_b[layer, 0]).reshape(B, S, D)
        h = _moe_layer(h.reshape(B * S, D), moe_w_group[layer], moe_b_group[layer], moe_w_router[layer],
                       moe_b_router[layer], moe_w_gate[layer], moe_w_up[layer], moe_w_down[layer],
                       ln_g[layer, 1], ln_b[layer, 1]).reshape(B, S, D)
        if layer == N_A_LAYERS - 1:
            k12, vt = _kv_proj(h, kv_w)
    return h
```

```python
import functools
import math

import jax
import jax.numpy as jnp
from jax import lax
from jax.experimental import pallas as pl
from jax.experimental.pallas import tpu as pltpu

F32 = jnp.float32
BF16 = jnp.bfloat16

D_MODEL = 1024
DEPTH = 4
CHUNK = 64
N_A_LAYERS = 2

GLA_HEADS = 4
GLA_DK = 128
GLA_DV = 256
GLA_GATE_RANK = 16
GLA_GATE_TAU = 16.0
GLA_HK = GLA_HEADS * GLA_DK
GLA_HV = GLA_HEADS * GLA_DV

DIFF_HEADS = 8
DIFF_DH = 64
DIFF_DV = 128
DIFF_QK = DIFF_HEADS * DIFF_DH

REL_BUCKETS = 32

MOE_GROUPS = 4
MOE_EPG = 4
MOE_EXPERTS = 16
MOE_FF = 512

DEEPNORM_ALPHA = (2.0 * DEPTH) ** 0.25
LN_EPS = 1e-5
NEG_INF = -1e30

LANES = 128
ATT_BLK = 512
ATT_HPS = 2
ATT_SUM_ROWS = 16
LOG2E = math.log2(math.e)
VMEM_LIMIT = 48 * 1024 * 1024
LARGE_VMEM_LIMIT = 58 * 1024 * 1024
MOE_SUB = 128
MOE_EPS = 2
MOE_POS_LANE = 12
MOE_TR_ROWS = 16


def _layer_norm(y, g, b):
    mu = jnp.mean(y, -1, keepdims=True)
    d = y - mu
    var = jnp.mean(d * d, -1, keepdims=True)
    return d * lax.rsqrt(var + LN_EPS) * g + b


def _sigmoid(x):
    return 1.0 / (1.0 + jnp.exp(-x))


def _full(shape):
    return pl.BlockSpec(shape, lambda *_: (0,) * len(shape))


def _gla_kernel(x_ref, w_main_ref, w_kt_ref, w_lr_ref, w_g2_ref, b_g_ref, gn_ref, w_out_ref, lng_ref, lnb_ref,
                tri_ref, o_ref, st_ref, mix_ref):
    nb, tb, d = x_ref.shape
    nt = (((1,), (1,)), ((), ()))

    @pl.when(pl.program_id(0) == 0)
    def _():
        st_ref[...] = jnp.zeros_like(st_ref)

    x = x_ref[...].reshape(nb * tb, d)
    xb = x.astype(BF16)
    lrt = lax.dot_general(w_lr_ref[...], xb, nt, preferred_element_type=F32)
    kt = lax.dot_general(w_kt_ref[...], xb, nt, preferred_element_type=F32)
    hq = jnp.dot(xb, w_main_ref[:, :GLA_HK], preferred_element_type=F32)
    glog = jnp.dot(w_g2_ref[...], lrt.astype(BF16), preferred_element_type=F32) + b_g_ref[...]
    hv = jnp.dot(xb, w_main_ref[:, GLA_HK:GLA_HK + GLA_HV], preferred_element_type=F32)
    la = (jnp.minimum(glog, 0.0) - jnp.log(1.0 + jnp.exp(-jnp.abs(glog)))) * (1.0 / GLA_GATE_TAU)
    la_hi = la.astype(BF16)
    la_lo = (la - la_hi.astype(F32)).astype(BF16)
    tri = tri_ref[...]
    cum = [jnp.dot(la_hi[:, b * tb:(b + 1) * tb], tri, preferred_element_type=F32)
           + jnp.dot(la_lo[:, b * tb:(b + 1) * tb], tri, preferred_element_type=F32) for b in range(nb)]
    r = jnp.dot(xb, w_main_ref[:, GLA_HK + GLA_HV:], preferred_element_type=F32)

    lane = lax.broadcasted_iota(jnp.int32, (GLA_HK, 2 * CHUNK), 1)
    halves = (lane < CHUNK, lane >= CHUNK)
    heads = [(slice(h * GLA_DK, (h + 1) * GLA_DK), slice(h * GLA_DV, (h + 1) * GLA_DV)) for h in range(GLA_HEADS)]
    for pair in range(tb // (2 * CHUNK)):
        c0 = pair * 2 * CHUNK
        dec, upd = {}, {}
        for b in range(nb):
            t0 = b * tb + c0
            cum_p = cum[b][:, c0:c0 + 2 * CHUNK]
            last = (cum_p[:, CHUNK - 1:CHUNK], cum_p[:, 2 * CHUNK - 1:2 * CHUNK])
            kdec = kt[:, t0:t0 + 2 * CHUNK] * jnp.exp(jnp.where(halves[0], last[0], last[1]) - cum_p)
            v_pair = hv[t0:t0 + 2 * CHUNK].astype(BF16)
            for half in range(2):
                kd = jnp.where(halves[half], kdec, 0.0).astype(BF16)
                dec[b, half] = jnp.exp(last[half])
                for h, (ks, vs) in enumerate(heads):
                    upd[b, half, h] = jnp.dot(kd[ks, :], v_pair[:, vs], preferred_element_type=F32)
        for half in range(2):
            for b in range(nb):
                r0 = b * tb + c0 + half * CHUNK
                qc = (hq[r0:r0 + CHUNK] * (GLA_DK ** -0.5)).astype(BF16)
                for h, (ks, vs) in enumerate(heads):
                    st = st_ref[b, h] * dec[b, half][ks, :] + upd[b, half, h]
                    st_ref[b, h] = st
                    o = jnp.dot(qc[:, ks], st.astype(BF16), preferred_element_type=F32)
                    o = o * lax.rsqrt(jnp.mean(o * o, -1, keepdims=True) + LN_EPS) * gn_ref[:, vs]
                    mix_ref[r0:r0 + CHUNK, vs] = o

    gated = mix_ref[...] * (r * _sigmoid(r))
    mix = jnp.dot(gated.astype(BF16), w_out_ref[...], preferred_element_type=F32)
    o_ref[...] = _layer_norm(DEEPNORM_ALPHA * x + mix, lng_ref[...], lnb_ref[...]).reshape(nb, tb, d)


def _gla_layer(h, w_in, w_gate2, b_gate, g_norm, w_out, ln_g, ln_b, *, tb=256):
    B, S, D = h.shape
    n_k = 2 * GLA_HK
    n_main = GLA_HK + 2 * GLA_HV
    n_all = 2 * GLA_HK + 2 * GLA_HV
    w_main = jnp.concatenate([w_in[:, :GLA_HK], w_in[:, n_k:n_all]], axis=1).astype(BF16)
    w_kt = w_in[:, GLA_HK:n_k].T.astype(BF16)
    w_lr = jnp.zeros((LANES, D), F32).at[:GLA_GATE_RANK].set(w_in[:, n_all:].T).astype(BF16)
    w_g2 = jnp.zeros((GLA_HK, LANES), F32).at[:, :GLA_GATE_RANK].set(w_gate2.T).astype(BF16)
    idx = jnp.arange(tb)
    tri = ((idx[:, None] <= idx[None, :]) & (idx[:, None] // CHUNK == idx[None, :] // CHUNK)).astype(BF16)
    return pl.pallas_call(
        _gla_kernel,
        grid=(S // tb,),
        in_specs=[
            pl.BlockSpec((B, tb, D), lambda j: (0, j, 0)),
            _full((D, n_main)), _full((GLA_HK, D)), _full((LANES, D)), _full((GLA_HK, LANES)),
            _full((GLA_HK, 1)), _full((1, GLA_HV)), _full((GLA_HV, D)), _full((1, D)), _full((1, D)),
            _full((tb, tb)),
        ],
        out_specs=pl.BlockSpec((B, tb, D), lambda j: (0, j, 0)),
        out_shape=jax.ShapeDtypeStruct((B, S, D), F32),
        scratch_shapes=[pltpu.VMEM((B, GLA_HEADS, GLA_DK, GLA_DV), F32), pltpu.VMEM((B * tb, GLA_HV), F32)],
        compiler_params=pltpu.CompilerParams(dimension_semantics=("arbitrary",),
                                             vmem_limit_bytes=VMEM_LIMIT),
        name="gla_layer",
    )(h, w_main, w_kt, w_lr, w_g2, b_gate.reshape(-1, 1), g_norm.reshape(1, -1), w_out.astype(BF16),
      ln_g.reshape(1, -1), ln_b.reshape(1, -1), tri)


def _first_max(vals):
    top = functools.reduce(jnp.maximum, vals)
    idx = jnp.full_like(top, float(len(vals) - 1))
    for j in range(len(vals) - 2, -1, -1):
        idx = jnp.where(vals[j] == top, float(j), idx)
    return top, idx


def _route(lt):
    gl = [lt[g:g + 1, :] for g in range(MOE_GROUPS)]
    gmax, gidx = _first_max(gl)
    gw = 1.0 / functools.reduce(jnp.add, [jnp.exp(v - gmax) for v in gl])
    el = []
    for j in range(MOE_EPG):
        rows = [lt[MOE_GROUPS + g * MOE_EPG + j:MOE_GROUPS + g * MOE_EPG + j + 1, :] for g in range(MOE_GROUPS)]
        sel = rows[MOE_GROUPS - 1]
        for g in range(MOE_GROUPS - 2, -1, -1):
            sel = jnp.where(gidx == float(g), rows[g], sel)
        el.append(sel)
    emax = functools.reduce(jnp.maximum, el)
    pe = [jnp.exp(v - emax) for v in el]
    psum = functools.reduce(jnp.add, pe)
    prob = [v / psum for v in pe]
    p1, i1 = _first_max(prob)
    p2, i2 = _first_max([jnp.where(i1 == float(j), -1.0, prob[j]) for j in range(MOE_EPG)])
    den = p1 + p2
    weights = [(jnp.where(i1 == float(j), p1 / den, 0.0) + jnp.where(i2 == float(j), p2 / den, 0.0)) * gw
               for j in range(MOE_EPG)]
    return gidx, weights


def _split3(v):
    hi = v.astype(BF16).astype(F32)
    mid = (v - hi).astype(BF16).astype(F32)
    lo = (v - hi - mid).astype(BF16).astype(F32)
    return hi, mid, lo


def _router_logits(x, wr_ref, br_ref):
    tb = x.shape[0]
    xh = x.astype(BF16)
    xl = (x - xh.astype(F32)).astype(BF16)
    x2 = jnp.concatenate([xh, xl], axis=1)
    half = tb // 2
    both = jnp.concatenate([jnp.dot(x2[:half], wr_ref[...], preferred_element_type=F32),
                            jnp.dot(x2[half:], wr_ref[...], preferred_element_type=F32)], axis=0)
    return both[:, :LANES] + both[:, LANES:] + br_ref[...]


def _moe_route(logits, tr_ref, posr_ref, cwn_ref, meta_ref, meta_base):
    tb = logits.shape[0]
    gidx, weights = _route(logits.T)
    onehot = [jnp.where(gidx == float(g), 1.0, 0.0) for g in range(MOE_GROUPS)]
    oht = jnp.concatenate(onehot + [jnp.zeros((MOE_TR_ROWS - MOE_GROUPS, tb), F32)], axis=0)
    ohb = oht.astype(BF16)
    earlier = jnp.where(lax.broadcasted_iota(jnp.int32, (LANES, LANES), 0)
                        < lax.broadcasted_iota(jnp.int32, (LANES, LANES), 1), 1.0, 0.0).astype(BF16)
    before, carry = [], jnp.zeros((MOE_TR_ROWS, 1), F32)
    for piece in range(tb // LANES):
        cols = slice(piece * LANES, (piece + 1) * LANES)
        before.append(jnp.dot(ohb[:, cols], earlier, preferred_element_type=F32) + carry)
        carry = carry + jnp.sum(oht[:, cols], -1, keepdims=True)
    rank = jnp.sum(jnp.concatenate(before, axis=1) * oht, 0, keepdims=True)
    ntile = jnp.floor((carry + (MOE_SUB - 1.0)) * (1.0 / MOE_SUB))
    pos, first_row = rank, jnp.zeros((1, 1), F32)
    for g in range(MOE_GROUPS):
        meta_ref[meta_base + g] = jnp.sum(ntile[g:g + 1, :]).astype(jnp.int32)
        meta_ref[meta_base + MOE_GROUPS + g] = jnp.sum(first_row).astype(jnp.int32)
        pos = pos + onehot[g] * first_row
        first_row = first_row + ntile[g:g + 1, :] * MOE_SUB
    posr_ref[0:1, :] = pos
    for j in range(MOE_EPG):
        for k, part in enumerate(_split3(weights[j])):
            tr_ref[j + k * MOE_EPG:j + k * MOE_EPG + 1, :] = part
    tr_ref[MOE_POS_LANE:MOE_POS_LANE + 1, :] = pos
    tr_ref[MOE_POS_LANE + 1:MOE_TR_ROWS, :] = jnp.zeros((MOE_TR_ROWS - MOE_POS_LANE - 1, tb), F32)
    cwn_ref[...] = tr_ref[...].T


def _moe_kernel(x_ref, wr_ref, br_ref, wg_ref, wu_ref, wd_ref, lng_ref, lnb_ref, o_ref,
                xs_ref, cw_ref, pt_ref, y_ref, res_ref, tr_ref, posr_ref, cwn_ref, meta_ref):
    blk_i = pl.program_id(0)
    step = pl.program_id(1)
    steps_per_group = MOE_EPG // MOE_EPS
    grp = step // steps_per_group
    j0 = (step % steps_per_group) * MOE_EPS
    tb = x_ref.shape[0]
    tbs = xs_ref.shape[0]
    d = x_ref.shape[1]
    meta_cur = (blk_i % 2) * (2 * MOE_GROUPS)
    meta_next = 2 * MOE_GROUPS - meta_cur

    @pl.when(step == 0)
    def _():
        @pl.when(blk_i == 0)
        def _():
            tr_ref[...] = jnp.zeros_like(tr_ref)
            _moe_route(_router_logits(x_ref[...], wr_ref, br_ref), tr_ref, posr_ref, cwn_ref, meta_ref, meta_cur)

        x = x_ref[...]
        res_ref[...] = x
        cwn = cwn_ref[...]
        pos_col = cwn[:, MOE_POS_LANE:MOE_POS_LANE + 1].astype(jnp.int32)
        pt_ref[...] = jnp.where(lax.broadcasted_iota(jnp.int32, (tb, tbs), 1) == pos_col, 1.0, 0.0).astype(BF16)
        pos_row = posr_ref[0:1, :].astype(jnp.int32)
        perm = jnp.where(lax.broadcasted_iota(jnp.int32, (tbs, tb), 0) == pos_row, 1.0, 0.0).astype(BF16)
        sorted_all = jnp.dot(perm, jnp.concatenate([x.astype(BF16), cwn.astype(BF16)], axis=1),
                             preferred_element_type=F32)
        xs_ref[...] = sorted_all[:, :d].astype(BF16)
        cw_ref[...] = sorted_all[:, d:]
        y_ref[...] = jnp.zeros_like(y_ref)

    base = meta_ref[meta_cur + MOE_GROUPS + grp]

    def experts(start, nrows):
        rows = pl.ds(pl.multiple_of(start, MOE_SUB), nrows)
        xt = xs_ref[rows, :]
        cwt = cw_ref[rows, :]
        lane = lax.broadcasted_iota(jnp.int32, (nrows, LANES), 1)
        hidden = []
        for k in range(MOE_EPS):
            gt = jnp.dot(xt, wg_ref[k], preferred_element_type=F32)
            ut = jnp.dot(xt, wu_ref[k], preferred_element_type=F32)
            hidden.append(((gt * _sigmoid(gt)) * ut).astype(BF16))
        total = None
        for k in range(MOE_EPS):
            y = jnp.dot(hidden[k], wd_ref[k], preferred_element_type=F32)
            mine = (lane % MOE_EPG == j0 + k) & (lane < 3 * MOE_EPG)
            c = jnp.sum(jnp.where(mine, cwt, 0.0), -1, keepdims=True)
            total = c * y if total is None else total + c * y
        y_ref[rows, :] += total

    n_tiles = meta_ref[meta_cur + grp]

    def wide(i, carry):
        experts(base + i * (2 * MOE_SUB), 2 * MOE_SUB)
        return carry

    lax.fori_loop(0, n_tiles // 2, wide, 0)

    @pl.when(n_tiles % 2 == 1)
    def _():
        experts(base + (n_tiles - 1) * MOE_SUB, MOE_SUB)

    @pl.when(step == MOE_EXPERTS // MOE_EPS - 1)
    def _():
        logits = _router_logits(x_ref[...], wr_ref, br_ref)
        yb = y_ref[...].astype(BF16)
        npiece = 4
        piece = tb // npiece
        for c in range(npiece):
            if c == npiece - 1:
                _moe_route(logits, tr_ref, posr_ref, cwn_ref, meta_ref, meta_next)
            rows = slice(c * piece, (c + 1) * piece)
            ffn = jnp.dot(pt_ref[rows, :], yb, preferred_element_type=F32)
            o_ref[rows, :] = _layer_norm(DEEPNORM_ALPHA * res_ref[rows, :] + ffn, lng_ref[...], lnb_ref[...])


def _moe_layer(h, w_group, b_group, w_router, b_router, w_gate, w_up, w_down, ln_g, ln_b, *, tb=1024):
    T, D = h.shape
    wr = jnp.zeros((D, LANES), F32).at[:, :MOE_GROUPS].set(w_group)
    wr = wr.at[:, MOE_GROUPS:MOE_GROUPS + MOE_EXPERTS].set(w_router)
    wrh = wr.astype(BF16)
    wrl = (wr - wrh.astype(F32)).astype(BF16)
    wr2 = jnp.concatenate([jnp.concatenate([wrh, wrl], axis=1),
                           jnp.concatenate([wrh, jnp.zeros_like(wrl)], axis=1)], axis=0)
    br = jnp.zeros((1, LANES), F32).at[0, :MOE_GROUPS].set(b_group)
    br = br.at[0, MOE_GROUPS:MOE_GROUPS + MOE_EXPERTS].set(b_router.reshape(-1))
    tbs = tb + (MOE_GROUPS - 1) * MOE_SUB
    nblk = T // tb
    return pl.pallas_call(
        _moe_kernel,
        grid=(nblk, MOE_EXPERTS // MOE_EPS),
        in_specs=[
            pl.BlockSpec((tb, D), lambda i, e: (jnp.where(e == 0, i, jnp.minimum(i + 1, nblk - 1)), 0)),
            _full((2 * D, 2 * LANES)), _full((1, LANES)),
            pl.BlockSpec((MOE_EPS, D, MOE_FF), lambda i, e: (e, 0, 0)),
            pl.BlockSpec((MOE_EPS, D, MOE_FF), lambda i, e: (e, 0, 0)),
            pl.BlockSpec((MOE_EPS, MOE_FF, D), lambda i, e: (e, 0, 0)),
            _full((1, D)), _full((1, D)),
        ],
        out_specs=pl.BlockSpec((tb, D), lambda i, e: (i, 0)),
        out_shape=jax.ShapeDtypeStruct((T, D), F32),
        scratch_shapes=[pltpu.VMEM((tbs, D), BF16), pltpu.VMEM((tbs, LANES), F32), pltpu.VMEM((tb, tbs), BF16),
                        pltpu.VMEM((tbs, D), F32), pltpu.VMEM((tb, D), F32), pltpu.VMEM((LANES, tb), F32),
                        pltpu.VMEM((8, tb), F32), pltpu.VMEM((tb, LANES), F32),
                        pltpu.SMEM((4 * MOE_GROUPS,), jnp.int32)],
        compiler_params=pltpu.CompilerParams(dimension_semantics=("arbitrary", "arbitrary"),
                                             vmem_limit_bytes=LARGE_VMEM_LIMIT),
        name="moe_layer",
    )(h, wr2, br, w_gate.astype(BF16), w_up.astype(BF16), w_down.astype(BF16),
      ln_g.reshape(1, -1), ln_b.reshape(1, -1))


def _kv_kernel(x_ref, wk_ref, wvt_ref, k_ref, vt_ref):
    xb = x_ref[0].astype(BF16)
    k_ref[0] = jnp.dot(xb, wk_ref[...], preferred_element_type=F32).astype(BF16)
    vt = lax.dot_general(wvt_ref[...], xb, (((1,), (1,)), ((), ())), preferred_element_type=F32)
    for j in range(vt_ref.shape[1]):
        vt_ref[0, j] = vt[:, j * ATT_BLK:(j + 1) * ATT_BLK].astype(BF16)


def _kv_proj(h, kv_w, *, ts=1024):
    B, S, D = h.shape
    k1 = kv_w[:, :DIFF_QK].reshape(D, DIFF_HEADS, DIFF_DH)
    k2 = kv_w[:, DIFF_QK:2 * DIFF_QK].reshape(D, DIFF_HEADS, DIFF_DH)
    wk = jnp.stack([k1, k2], axis=2).reshape(D, 2 * DIFF_QK).astype(BF16)
    wvt = kv_w[:, 2 * DIFF_QK:].T.astype(BF16)
    nv = DIFF_HEADS * DIFF_DV
    nj = ts // ATT_BLK
    return pl.pallas_call(
        _kv_kernel,
        grid=(B, S // ts),
        in_specs=[pl.BlockSpec((1, ts, D), lambda b, j: (b, j, 0)), _full((D, 2 * DIFF_QK)), _full((nv, D))],
        out_specs=[pl.BlockSpec((1, ts, 2 * DIFF_QK), lambda b, j: (b, j, 0)),
                   pl.BlockSpec((1, nj, nv, ATT_BLK), lambda b, j: (b, j, 0, 0))],
        out_shape=[jax.ShapeDtypeStruct((B, S, 2 * DIFF_QK), BF16),
                   jax.ShapeDtypeStruct((B, S // ATT_BLK, nv, ATT_BLK), BF16)],
        compiler_params=pltpu.CompilerParams(dimension_semantics=("parallel", "parallel"),
                                             vmem_limit_bytes=VMEM_LIMIT),
        name="kv_proj",
    )(h, wk, wvt)


def _q_kernel(x_ref, wqt_ref, qt_ref):
    xb = x_ref[0].astype(BF16)
    qt = lax.dot_general(wqt_ref[...], xb, (((1,), (1,)), ((), ())), preferred_element_type=F32)
    qt = qt * (DIFF_DH ** -0.5 * LOG2E)
    for j in range(qt_ref.shape[1]):
        qt_ref[0, j] = qt[:, j * ATT_BLK:(j + 1) * ATT_BLK].astype(BF16)


def _q_proj(h, w_q, *, ts=1024):
    B, S, D = h.shape
    q1 = w_q[:, :DIFF_QK].reshape(D, DIFF_HEADS, DIFF_DH)
    q2 = w_q[:, DIFF_QK:].reshape(D, DIFF_HEADS, DIFF_DH)
    wqt = jnp.stack([q1, q2], axis=2).reshape(D, 2 * DIFF_QK).T.astype(BF16)
    nq = 2 * DIFF_QK
    nj = ts // ATT_BLK
    return pl.pallas_call(
        _q_kernel,
        grid=(B, S // ts),
        in_specs=[pl.BlockSpec((1, ts, D), lambda b, j: (b, j, 0)), _full((nq, D))],
        out_specs=pl.BlockSpec((1, nj, nq, ATT_BLK), lambda b, j: (b, j, 0, 0)),
        out_shape=jax.ShapeDtypeStruct((B, S // ATT_BLK, nq, ATT_BLK), BF16),
        compiler_params=pltpu.CompilerParams(dimension_semantics=("parallel", "parallel"),
                                             vmem_limit_bytes=VMEM_LIMIT),
        name="q_proj",
    )(h, wqt)


def _bias_kernel(tab_ref, o_ref):
    h = pl.program_id(0)
    kk = lax.broadcasted_iota(jnp.int32, (ATT_BLK, ATT_BLK), 0)
    qq = lax.broadcasted_iota(jnp.int32, (ATT_BLK, ATT_BLK), 1)
    nb = REL_BUCKETS // 2
    far = tab_ref[nb - 1, h]
    for t in range(2):
        rel = kk - qq - t * ATT_BLK
        n = jnp.abs(rel)
        large = nb // 2
        for thr in (12, 16, 23, 32, 46, 64, 91):
            large = large + (n >= thr).astype(jnp.int32)
        bucket = jnp.where(rel > 0, nb, 0) + jnp.where(n < nb // 2, n, large)
        bias = jnp.zeros((ATT_BLK, ATT_BLK), F32)
        for b in range(REL_BUCKETS):
            bias = jnp.where(bucket == b, tab_ref[b, h], bias)
        bias = (bias - far) * LOG2E
        if t == 0:
            visible = (kk // CHUNK) <= (qq // CHUNK)
            bias = jnp.where(visible, bias, NEG_INF)
        o_ref[0, t] = bias


def _bias_tiles(rel_table):
    return pl.pallas_call(
        _bias_kernel,
        grid=(DIFF_HEADS,),
        in_specs=[pl.BlockSpec(memory_space=pltpu.SMEM)],
        out_specs=pl.BlockSpec((1, 2, ATT_BLK, ATT_BLK), lambda h: (h, 0, 0, 0)),
        out_shape=jax.ShapeDtypeStruct((DIFF_HEADS, 2, ATT_BLK, ATT_BLK), F32),
        name="bias_tiles",
    )(rel_table)


def _attn_kernel(lam_ref, qt_ref, k_ref, vt_ref, bias_ref, g_ref, o_ref, *scratch, lambda_init):
    blk = ATT_BLK
    qi = pl.program_id(2)
    n_far = jnp.maximum(qi - 1, 0)
    prev_block = jnp.maximum(qi - 1, 0)
    heads = [scratch[hh * 7:(hh + 1) * 7] for hh in range(ATT_HPS)]
    z = jnp.zeros((DIFF_DH, blk), BF16)
    ones_rows = jnp.ones((ATT_SUM_ROWS, blk), BF16)
    for hh, (w_ref, _, _, _, _, acc_ref, ml_ref) in enumerate(heads):
        qt = qt_ref[0, 0, hh * 2 * DIFF_DH:(hh + 1) * 2 * DIFF_DH, :]
        w_ref[...] = jnp.concatenate(
            [jnp.concatenate([qt[:DIFF_DH], z], 0), jnp.concatenate([z, qt[DIFF_DH:]], 0)], 1)
        ml_ref[0:1, :] = jnp.full((1, 2 * blk), NEG_INF, F32)
        acc_ref[...] = jnp.zeros_like(acc_ref)

    def scores_diag(slot):
        nslab = 4
        rows = blk // nslab
        for hh, (w_ref, s0_ref, s1_ref, t0_ref, t1_ref, _, _) in enumerate(heads):
            s_ref, t_ref = ((s0_ref, t0_ref), (s1_ref, t1_ref))[slot]
            tmax = None
            for j in range(nslab):
                kt = k_ref[0, pl.ds(pl.multiple_of(qi * blk + j * rows, rows), rows),
                           hh * LANES:(hh + 1) * LANES]
                bias = bias_ref[hh, 0, j * rows:(j + 1) * rows, j * rows:blk]
                parts = []
                for i in range(2):
                    if j:
                        parts.append(jnp.full((rows, j * rows), NEG_INF, F32))
                    parts.append(jnp.dot(kt, w_ref[:, i * blk + j * rows:(i + 1) * blk],
                                         preferred_element_type=F32) + bias)
                s = jnp.concatenate(parts, axis=1)
                s_ref[j * rows:(j + 1) * rows, :] = s
                smax = jnp.max(s, 0, keepdims=True)
                tmax = smax if tmax is None else jnp.maximum(tmax, smax)
            t_ref[0:1, :] = tmax

    def accumulate(kb, slot, then=None):
        for hh, (_, s0_ref, s1_ref, t0_ref, t1_ref, acc_ref, ml_ref) in enumerate(heads):
            s_ref, t_ref = ((s0_ref, t0_ref), (s1_ref, t1_ref))[slot]
            m_old = ml_ref[0:1, :]
            m_new = jnp.maximum(m_old, t_ref[0:1, :])
            a = jnp.exp2(m_old - m_new)
            p = jnp.exp2(s_ref[...] - m_new)
            ml_ref[0:1, :] = m_new
            pb = p.astype(BF16)
            vt = jnp.concatenate([vt_ref[0, kb, hh * DIFF_DV:(hh + 1) * DIFF_DV, :], ones_rows], axis=0)
            for i in range(2):
                cols = slice(i * blk, (i + 1) * blk)
                acc_ref[i] = a[:, cols] * acc_ref[i] + jnp.dot(vt, pb[:, cols], preferred_element_type=F32)
            if then is not None:
                then(hh)

    def fused(kb_a, slot_a, kb_b, slot_b, bias_idx=None, extra=None, diag_b=False):
        nslab = 4
        rows = blk // nslab
        width = 2 * blk // nslab
        for hh, (w_ref, s0_ref, s1_ref, t0_ref, t1_ref, acc_ref, ml_ref) in enumerate(heads):
            sa_ref, ta_ref = ((s0_ref, t0_ref), (s1_ref, t1_ref))[slot_a]
            sb_ref, tb_ref = ((s0_ref, t0_ref), (s1_ref, t1_ref))[slot_b]
            m_old = ml_ref[0:1, :]
            m_new = jnp.maximum(m_old, tb_ref[0:1, :])
            a = jnp.exp2(m_old - m_new)
            ml_ref[0:1, :] = m_new
            vt = jnp.concatenate([vt_ref[0, kb_b, hh * DIFF_DV:(hh + 1) * DIFF_DV, :], ones_rows], axis=0)
            tmax = None
            for j in range(nslab):
                cols = slice(j * width, (j + 1) * width)
                keys = blk // 2 if (diag_b and j % 2 == 0) else blk
                p = jnp.exp2(sb_ref[:keys, cols] - m_new[:, cols])
                i, c0 = divmod(j * width, blk)
                acc_ref[i, :, c0:c0 + width] = (a[:, cols] * acc_ref[i, :, c0:c0 + width]
                                                + jnp.dot(vt[:, :keys], p.astype(BF16),
                                                          preferred_element_type=F32))
                kt = k_ref[0, pl.ds(pl.multiple_of(kb_a * blk + j * rows, rows), rows),
                           hh * LANES:(hh + 1) * LANES]
                s = jnp.dot(kt, w_ref[...], preferred_element_type=F32)
                if bias_idx is not None:
                    bias = bias_ref[hh, bias_idx, j * rows:(j + 1) * rows, :]
                    s = s + jnp.concatenate([bias, bias], axis=1)
                if extra is not None:
                    s = s + extra
                sa_ref[j * rows:(j + 1) * rows, :] = s
                smax = jnp.max(s, 0, keepdims=True)
                tmax = smax if tmax is None else jnp.maximum(tmax, smax)
            ta_ref[0:1, :] = tmax

    scores_diag(0)
    fused(prev_block, 1, qi, 0, bias_idx=1, extra=jnp.where(qi >= 1, 0.0, -jnp.inf).astype(F32), diag_b=True)

    def pair(j):
        kb = 2 * j
        fused(kb, 0, jnp.where(j == 0, prev_block, kb - 1), 1)
        fused(kb + 1, 1, kb, 0)

    def two_pairs(i, carry):
        pair(2 * i)
        pair(2 * i + 1)
        return carry

    n_pairs = n_far // 2
    lax.fori_loop(0, n_pairs // 2, two_pairs, 0)

    @pl.when(n_pairs % 2 == 1)
    def _():
        pair(n_pairs - 1)

    last_slot1 = jnp.where(n_pairs == 0, prev_block, 2 * n_pairs - 1)

    def finish(hh):
        acc_ref = heads[hh][5]
        lam = (jnp.exp(jnp.sum(lam_ref[0:1, :] * lam_ref[1:2, :], keepdims=True))
               - jnp.exp(jnp.sum(lam_ref[2:3, :] * lam_ref[3:4, :], keepdims=True)) + lambda_init)
        o1, o2 = acc_ref[0], acc_ref[1]
        ot = (o1[:DIFF_DV] / o1[DIFF_DV:DIFF_DV + 1] - lam * (o2[:DIFF_DV] / o2[DIFF_DV:DIFF_DV + 1]))
        ot = ot * lax.rsqrt(jnp.mean(ot * ot, 0, keepdims=True) + LN_EPS) * g_ref[...] * (1.0 - lambda_init)
        o_ref[0, :, hh * DIFF_DV:(hh + 1) * DIFF_DV] = ot.T.astype(BF16)

    @pl.when(n_far % 2 == 1)
    def _():
        fused(n_far - 1, 0, last_slot1, 1)
        accumulate(n_far - 1, 0, then=finish)

    @pl.when(n_far % 2 == 0)
    def _():
        accumulate(last_slot1, 1, then=finish)


def _diff_attention(qt, k12, vt, bias, lam4, g_sub, lambda_init):
    B, nqb, _, _ = qt.shape
    S = k12.shape[1]
    kernel = functools.partial(_attn_kernel, lambda_init=lambda_init)
    once = pl.Buffered(1)
    head_scratch = [pltpu.VMEM((2 * DIFF_DH, 2 * ATT_BLK), BF16),
                    pltpu.VMEM((ATT_BLK, 2 * ATT_BLK), F32), pltpu.VMEM((ATT_BLK, 2 * ATT_BLK), F32),
                    pltpu.VMEM((8, 2 * ATT_BLK), F32), pltpu.VMEM((8, 2 * ATT_BLK), F32),
                    pltpu.VMEM((2, DIFF_DV + ATT_SUM_ROWS, ATT_BLK), F32), pltpu.VMEM((8, 2 * ATT_BLK), F32)]
    return pl.pallas_call(
        kernel,
        grid=(B, DIFF_HEADS // ATT_HPS, nqb),
        in_specs=[
            _full((4, DIFF_DH)),
            pl.BlockSpec((1, 1, ATT_HPS * 2 * DIFF_DH, ATT_BLK), lambda b, h, i: (b, i, h, 0)),
            pl.BlockSpec((1, S, ATT_HPS * 2 * DIFF_DH), lambda b, h, i: (b, 0, h)),
            pl.BlockSpec((1, nqb, ATT_HPS * DIFF_DV, ATT_BLK), lambda b, h, i: (b, 0, h, 0)),
            pl.BlockSpec((ATT_HPS, 2, ATT_BLK, ATT_BLK), lambda b, h, i: (h, 0, 0, 0), pipeline_mode=once),
            _full((DIFF_DV, 1)),
        ],
        out_specs=pl.BlockSpec((1, ATT_BLK, ATT_HPS * DIFF_DV), lambda b, h, i: (b, i, h)),
        out_shape=jax.ShapeDtypeStruct((B, S, DIFF_HEADS * DIFF_DV), BF16),
        scratch_shapes=head_scratch * ATT_HPS,
        compiler_params=pltpu.CompilerParams(dimension_semantics=("parallel", "parallel", "arbitrary"),
                                             vmem_limit_bytes=LARGE_VMEM_LIMIT),
        name="diff_attn",
    )(lam4, qt, k12, vt, bias, g_sub.reshape(-1, 1))


def _proj_ln_kernel(h_ref, a_ref, w_ref, lng_ref, lnb_ref, o_ref):
    npiece = 2
    piece = h_ref.shape[0] // npiece
    for c in range(npiece):
        rows = slice(c * piece, (c + 1) * piece)
        mix = jnp.dot(a_ref[rows, :], w_ref[...], preferred_element_type=F32)
        o_ref[rows, :] = _layer_norm(DEEPNORM_ALPHA * h_ref[rows, :] + mix, lng_ref[...], lnb_ref[...])


def _proj_ln(h, a, w_out, ln_g, ln_b, *, tb=1024):
    T, D = h.shape
    return pl.pallas_call(
        _proj_ln_kernel,
        grid=(T // tb,),
        in_specs=[pl.BlockSpec((tb, D), lambda i: (i, 0)), pl.BlockSpec((tb, a.shape[1]), lambda i: (i, 0)),
                  _full(w_out.shape), _full((1, D)), _full((1, D))],
        out_specs=pl.BlockSpec((tb, D), lambda i: (i, 0)),
        out_shape=jax.ShapeDtypeStruct((T, D), F32),
        compiler_params=pltpu.CompilerParams(dimension_semantics=("parallel",), vmem_limit_bytes=VMEM_LIMIT),
        name="attn_out_ln",
    )(h, a, w_out.astype(BF16), ln_g.reshape(1, -1), ln_b.reshape(1, -1))


def kernel(x, a_w_in, a_w_gate2, a_b_gate, a_g_norm, a_w_out, kv_w, b_w_q, b_lam_q1, b_lam_k1, b_lam_q2, b_lam_k2, b_g_sub, b_w_out, rel_table, moe_w_group, moe_b_group, moe_w_router, moe_b_router, moe_w_gate, moe_w_up, moe_w_down, ln_g, ln_b):
    B, S, D = x.shape
    h = x
    bias = None
    k12 = vt = None
    for layer in range(DEPTH):
        if layer < N_A_LAYERS:
            h = _gla_layer(h, a_w_in[layer], a_w_gate2[layer], a_b_gate[layer], a_g_norm[layer],
                           a_w_out[layer], ln_g[layer, 0], ln_b[layer, 0])
        else:
            j = layer - N_A_LAYERS
            lambda_init = 0.8 - 0.6 * math.exp(-0.3 * layer)
            if bias is None:
                bias = _bias_tiles(rel_table)
            qt = _q_proj(h, b_w_q[j])
            lam4 = jnp.stack([b_lam_q1[j], b_lam_k1[j], b_lam_q2[j], b_lam_k2[j]])
            a = _diff_attention(qt, k12, vt, bias, lam4, b_g_sub[j], lambda_init)
            h = _proj_ln(h.reshape(B * S, D), a.reshape(B * S, -1), b_w_out[j], ln_g[layer, 0],
                         ln_b[layer, 0]).reshape(B, S, D)
        h = _moe_layer(h.reshape(B * S, D), moe_w_group[layer], moe_b_group[layer], moe_w_router[layer],
                       moe_b_router[layer], moe_w_gate[layer], moe_w_up[layer], moe_w_down[layer],
                       ln_g[layer, 1], ln_b[layer, 1]).reshape(B, S, D)
        if layer == N_A_LAYERS - 1:
            k12, vt = _kv_proj(h, kv_w)
    return h
```
